```python
import functools
import numpy as np
import jax, jax.numpy as jnp
from jax import lax

D_MODEL = 2048
BATCH = 4
SEQ = 2048
DEPTH = 1
DEC_BATCH = 32
DEC_SEQ = 8
PAST_LEN = 8192
PAGE_SIZE = 128

GDN_QK_HEADS = 16
GDN_V_HEADS = 32
GDN_DK = 128
GDN_DV = 128
CONV_W = 4
GDN_CHUNK = 64
ATT_HEADS = 16
ATT_KV_HEADS = 2
ATT_DH = 128
IDX_HEADS = 16
IDX_DH = 128
TOPK_MAX = 256
Q_BLOCK = 128
NORM_EPS = 1e-6
L2_EPS = 1e-6

A_QK = GDN_QK_HEADS * GDN_DK
A_V = GDN_V_HEADS * GDN_DV
A_CONV_CH = 2 * A_QK + A_V
B_Q = ATT_HEADS * ATT_DH
B_KV = ATT_KV_HEADS * ATT_DH
IDX_Q = IDX_HEADS * IDX_DH
SPLIT_SIZES = (A_CONV_CH, A_V, GDN_V_HEADS, GDN_V_HEADS, B_Q, B_KV, B_KV, B_Q, IDX_Q, IDX_DH, IDX_HEADS, D_MODEL, D_MODEL)
D_IN = sum(SPLIT_SIZES)

kernel_name = 'hybrid_gdn_dsa_decoder_step'


def rms_norm(x, g):
    xf = x.astype(jnp.float32)
    y = xf * lax.rsqrt(jnp.mean(xf * xf, axis=-1, keepdims=True) + NORM_EPS)
    return (y * g.astype(jnp.float32)).astype(x.dtype)


def l2_normalize(x):
    xf = x.astype(jnp.float32)
    return xf * lax.rsqrt(jnp.sum(xf * xf, axis=-1, keepdims=True) + L2_EPS)


def split_columns(t):
    offsets = np.cumsum(SPLIT_SIZES)[:-1].tolist()
    return jnp.split(t, offsets, axis=-1)


def causal_short_conv(x, buf, w):
    L = x.shape[1]
    xp = jnp.concatenate([buf.astype(x.dtype), x], axis=1)
    y = sum(xp[:, i:i + L] * w[i] for i in range(CONV_W))
    return jax.nn.silu(y), xp[:, xp.shape[1] - (CONV_W - 1):]


def gated_delta_rule(q, k, v, g, beta, s0):
    B, L, H, DK = q.shape
    DV = v.shape[-1]
    C = min(GDN_CHUNK, L)
    n = -(-L // C)
    pad = n * C - L

    def blocks(t):
        t = jnp.moveaxis(t, 2, 1)
        t = jnp.pad(t, [(0, 0), (0, 0), (0, pad)] + [(0, 0)] * (t.ndim - 3))
        return t.reshape(B, H, n, C, *t.shape[3:])

    q, k, v, g, beta = (blocks(t) for t in (q, k, v, g, beta))
    G = jnp.cumsum(g, axis=-1)
    incl = jnp.tril(jnp.ones((C, C), bool))
    strict = jnp.tril(jnp.ones((C, C), bool), -1)
    decay = jnp.exp(jnp.where(incl, G[..., :, None] - G[..., None, :], -jnp.inf))
    kb = k * beta[..., None]
    a_strict = jnp.where(strict, jnp.einsum('bhnid,bhnjd->bhnij', kb, k) * decay, 0.0)
    rhs = jnp.concatenate([v * beta[..., None], kb * jnp.exp(G)[..., None]], axis=-1)
    sol = lax.linalg.triangular_solve(a_strict, rhs, left_side=True, lower=True, unit_diagonal=True)
    u_intra, w_state = sol[..., :DV], sol[..., DV:]
    qk = jnp.einsum('bhnid,bhnjd->bhnij', q, k) * decay
    q_dec = q * jnp.exp(G)[..., None]
    k_tail = k * jnp.exp(G[..., -1:] - G)[..., None]
    g_tail = jnp.exp(G[..., -1])

    def chunk_step(s, xs):
        u_c, w_c, qk_c, qd_c, kt_c, gt_c = xs
        u = u_c - jnp.einsum('bhck,bhkv->bhcv', w_c, s)
        o = jnp.einsum('bhck,bhkv->bhcv', qd_c, s) + jnp.einsum('bhij,bhjv->bhiv', qk_c, u)
        s = s * gt_c[..., None, None] + jnp.einsum('bhck,bhcv->bhkv', kt_c, u)
        return s, o

    xs = tuple(jnp.moveaxis(t, 2, 0) for t in (u_intra, w_state, qk, q_dec, k_tail, g_tail))
    s, o = lax.scan(chunk_step, s0, xs)
    o = jnp.moveaxis(o, 0, 2).reshape(B, H, n * C, DV)[:, :, :L]
    return jnp.moveaxis(o, 1, 2), s


def gdn_branch(a_qkv, a_z, a_b, a_a, conv_buf, s0, conv_w, a_log, dt_bias, gdn_norm_g):
    B, L, _ = a_qkv.shape
    qkv, conv_new = causal_short_conv(a_qkv, conv_buf, conv_w)
    q, k, v = jnp.split(qkv, [A_QK, 2 * A_QK], axis=-1)
    rep = GDN_V_HEADS // GDN_QK_HEADS
    q = jnp.repeat(l2_normalize(q.reshape(B, L, GDN_QK_HEADS, GDN_DK)), rep, axis=2) * (GDN_DK ** -0.5)
    k = jnp.repeat(l2_normalize(k.reshape(B, L, GDN_QK_HEADS, GDN_DK)), rep, axis=2)
    v = v.reshape(B, L, GDN_V_HEADS, GDN_DV).astype(jnp.float32)
    beta = jax.nn.sigmoid(a_b.astype(jnp.float32))
    g = -jnp.exp(a_log.astype(jnp.float32)) * jax.nn.softplus(a_a.astype(jnp.float32) + dt_bias.astype(jnp.float32))
    o, s = gated_delta_rule(q, k, v, g, beta, s0.astype(jnp.float32))
    z = a_z.reshape(B, L, GDN_V_HEADS, GDN_DV).astype(jnp.float32)
    y = rms_norm(o, gdn_norm_g) * jax.nn.silu(z)
    return y.reshape(B, L, A_V).astype(a_qkv.dtype), s.astype(s0.dtype), conv_new


def indexer_topk(qi, wi, kidx, q_pos, topk):
    logits = jnp.einsum('bqhd,bkd->bqhk', qi.astype(jnp.float32), kidx.astype(jnp.float32)) * (IDX_DH ** -0.5)
    score = jnp.einsum('bqhk,bqh->bqk', jax.nn.relu(logits), wi.astype(jnp.float32)) * (IDX_HEADS ** -0.5)
    causal = jnp.arange(kidx.shape[1])[None, :] <= q_pos[:, None]
    score = jnp.where(causal[None], score, -jnp.inf)
    _, idx = lax.top_k(score, topk)
    return idx, idx <= q_pos[None, :, None]


def sparse_attention(q, k_sel, v_sel, valid):
    B, Q = q.shape[:2]
    qg = q.reshape(B, Q, ATT_KV_HEADS, ATT_HEADS // ATT_KV_HEADS, ATT_DH)
    s = jnp.einsum('bqngd,bqknd->bqngk', qg, k_sel).astype(jnp.float32) * (ATT_DH ** -0.5)
    s = jnp.where(valid[:, :, None, None, :], s, -jnp.inf)
    p = jax.nn.softmax(s, axis=-1).astype(v_sel.dtype)
    o = jnp.einsum('bqngk,bqknd->bqngd', p, v_sel)
    return o.reshape(B, Q, ATT_HEADS * ATT_DH)


def gather_rows(rows, idx):
    return jax.vmap(lambda r, i: r[i])(rows, idx)


def dsa_prompt(q, k, v, qi, wi, kidx):
    B, L = q.shape[:2]
    topk = min(TOPK_MAX, L // 4)
    qb = Q_BLOCK if L % Q_BLOCK == 0 else L
    nb = L // qb

    def to_blocks(t):
        return jnp.swapaxes(t.reshape(B, nb, qb, *t.shape[2:]), 0, 1)

    def block(args):
        q_b, qi_b, wi_b, pos_b = args
        idx, valid = indexer_topk(qi_b, wi_b, kidx, pos_b, topk)
        return sparse_attention(q_b, gather_rows(k, idx), gather_rows(v, idx), valid)

    o = lax.map(block, (to_blocks(q), to_blocks(qi), to_blocks(wi), jnp.arange(L).reshape(nb, qb)))
    return jnp.swapaxes(o, 0, 1).reshape(B, L, ATT_HEADS * ATT_DH)


def dsa_sample(q, k, v, qi, wi, kidx, cache_k, cache_v, cache_kidx, page_table):
    B, L = q.shape[:2]
    past = page_table.shape[1] * PAGE_SIZE
    topk = min(TOPK_MAX, (past + L) // 4)
    kidx_past = cache_kidx[page_table].reshape(B, past, IDX_DH)
    kidx_all = jnp.concatenate([kidx_past, kidx.astype(kidx_past.dtype)], axis=1)
    idx, valid = indexer_topk(qi, wi, kidx_all, past + jnp.arange(L), topk)
    in_past = (idx < past)[..., None, None]
    p_idx = jnp.minimum(idx, past - 1)
    phys = gather_rows(page_table, p_idx // PAGE_SIZE)
    off = p_idx % PAGE_SIZE
    n_idx = jnp.clip(idx - past, 0, L - 1)
    k_sel = jnp.where(in_past, cache_k[phys, off], gather_rows(k, n_idx).astype(cache_k.dtype))
    v_sel = jnp.where(in_past, cache_v[phys, off], gather_rows(v, n_idx).astype(cache_v.dtype))
    return sparse_attention(q, k_sel, v_sel, valid)


def hybrid_layer(x, c, conv_buf, s0, attend, w_ada, b_ada, pre_norm_g, w_in, conv_w, a_log, dt_bias,
                 gdn_norm_g, w_pa, w_pb, w_out, post_norm_g):
    B, L, _ = x.shape
    mod = jnp.einsum('bc,cd->bd', jax.nn.silu(c), w_ada) + b_ada
    shift, scale, gate = jnp.split(mod, 3, axis=-1)
    h = rms_norm(x, pre_norm_g) * (1.0 + scale[:, None]) + shift[:, None]
    (a_qkv, a_z, a_b, a_a, b_q, b_k, b_v, b_z, i_q, i_k, i_w, g_a, g_b) = split_columns(h @ w_in)
    y_a, s_new, conv_new = gdn_branch(a_qkv, a_z, a_b, a_a, conv_buf, s0, conv_w, a_log, dt_bias, gdn_norm_g)
    q = b_q.reshape(B, L, ATT_HEADS, ATT_DH)
    k = b_k.reshape(B, L, ATT_KV_HEADS, ATT_DH)
    v = b_v.reshape(B, L, ATT_KV_HEADS, ATT_DH)
    qi = i_q.reshape(B, L, IDX_HEADS, IDX_DH)
    y_b = attend(q, k, v, qi, i_w, i_k) * jax.nn.silu(b_z)
    merged = jax.nn.sigmoid(g_a) * (y_a @ w_pa) + jax.nn.sigmoid(g_b) * (y_b @ w_pb)
    out = rms_norm(merged @ w_out, post_norm_g)
    return x + gate[:, None] * out, (k, v, i_k, s_new, conv_new)


def setup_inputs(seed: int = 0) -> dict:
    key = jax.random.key(seed)
    ks = jax.random.split(key, 24)
    f32 = jnp.float32
    n_pages = PAST_LEN // PAGE_SIZE
    n_pool = (DEC_BATCH * n_pages * 5) // 4

    def nrm(k, shape, s):
        return jax.random.normal(k, shape, f32) * s

    page_table = jax.random.permutation(ks[9], n_pool)[:DEC_BATCH * n_pages].reshape(DEC_BATCH, n_pages).astype(jnp.int32)
    return {
        'x_prompt': nrm(ks[0], (BATCH, SEQ, D_MODEL), 1.0),
        'x_sample': nrm(ks[1], (DEC_BATCH, DEC_SEQ, D_MODEL), 1.0),
        'c_prompt': nrm(ks[2], (BATCH, D_MODEL), 1.0),
        'c_sample': nrm(ks[3], (DEC_BATCH, D_MODEL), 1.0),
        'cache_k': nrm(ks[4], (DEPTH, n_pool, PAGE_SIZE, ATT_KV_HEADS, ATT_DH), 1.0),
        'cache_v': nrm(ks[5], (DEPTH, n_pool, PAGE_SIZE, ATT_KV_HEADS, ATT_DH), 1.0),
        'cache_kidx': nrm(ks[6], (DEPTH, n_pool, PAGE_SIZE, IDX_DH), 1.0),
        'state_gdn': nrm(ks[7], (DEPTH, DEC_BATCH, GDN_V_HEADS, GDN_DK, GDN_DV), 0.1),
        'state_conv': nrm(ks[8], (DEPTH, DEC_BATCH, CONV_W - 1, A_CONV_CH), 1.0),
        'page_table': page_table,
        'w_ada': nrm(ks[10], (DEPTH, D_MODEL, 3 * D_MODEL), 0.5 * D_MODEL ** -0.5),
        'b_ada': nrm(ks[11], (DEPTH, 3 * D_MODEL), 0.01),
        'pre_norm_g': 1.0 + nrm(ks[12], (DEPTH, D_MODEL), 0.02),
        'w_in': nrm(ks[13], (DEPTH, D_MODEL, D_IN), D_MODEL ** -0.5),
        'conv_w': nrm(ks[14], (DEPTH, CONV_W, A_CONV_CH), CONV_W ** -0.5),
        'a_log': jnp.log(jax.random.uniform(ks[15], (DEPTH, GDN_V_HEADS), f32, 1.0, 16.0)),
        'dt_bias': nrm(ks[16], (DEPTH, GDN_V_HEADS), 0.1),
        'gdn_norm_g': 1.0 + nrm(ks[17], (DEPTH, GDN_DV), 0.02),
        'w_pa': nrm(ks[18], (DEPTH, A_V, D_MODEL), A_V ** -0.5),
        'w_pb': nrm(ks[19], (DEPTH, B_Q, D_MODEL), B_Q ** -0.5),
        'w_out': nrm(ks[20], (DEPTH, D_MODEL, D_MODEL), D_MODEL ** -0.5),
        'post_norm_g': 1.0 + nrm(ks[21], (DEPTH, D_MODEL), 0.02),
    }


def reference(x_prompt, x_sample, c_prompt, c_sample, cache_k, cache_v, cache_kidx, state_gdn, state_conv,
              page_table, w_ada, b_ada, pre_norm_g, w_in, conv_w, a_log, dt_bias, gdn_norm_g, w_pa, w_pb,
              w_out, post_norm_g):
    y_p, y_s = x_prompt, x_sample
    bp = x_prompt.shape[0]
    new_p, new_s = [], []
    for l in range(DEPTH):
        w = (w_ada[l], b_ada[l], pre_norm_g[l], w_in[l], conv_w[l], a_log[l], dt_bias[l], gdn_norm_g[l],
             w_pa[l], w_pb[l], w_out[l], post_norm_g[l])
        conv0 = jnp.zeros((bp, CONV_W - 1, A_CONV_CH), x_prompt.dtype)
        s0 = jnp.zeros((bp, GDN_V_HEADS, GDN_DK, GDN_DV), state_gdn.dtype)
        y_p, st_p = hybrid_layer(y_p, c_prompt, conv0, s0, dsa_prompt, *w)
        attend_s = functools.partial(dsa_sample, cache_k=cache_k[l], cache_v=cache_v[l],
                                     cache_kidx=cache_kidx[l], page_table=page_table)
        y_s, st_s = hybrid_layer(y_s, c_sample, state_conv[l], state_gdn[l], attend_s, *w)
        new_p.append(st_p)
        new_s.append(st_s)

    def stacked(sts, i):
        return jnp.stack([st[i] for st in sts])

    return (y_p, y_s,
            stacked(new_p, 0), stacked(new_p, 1), stacked(new_p, 2), stacked(new_p, 3), stacked(new_p, 4),
            stacked(new_s, 0), stacked(new_s, 1), stacked(new_s, 2), stacked(new_s, 3), stacked(new_s, 4))
```

```python
import functools
import math

import jax
import jax.numpy as jnp
from jax import lax
from jax.experimental import pallas as pl
from jax.experimental.pallas import tpu as pltpu

F32 = jnp.float32
BF16 = jnp.bfloat16
I32 = jnp.int32

D_MODEL = 2048
PAGE_SIZE = 128
GDN_QK_HEADS = 16
GDN_V_HEADS = 32
GDN_DK = 128
GDN_DV = 128
CONV_W = 4
GDN_CHUNK = 64
ATT_HEADS = 16
ATT_KV_HEADS = 2
ATT_DH = 128
IDX_HEADS = 16
IDX_DH = 128
TOPK_MAX = 256
NORM_EPS = 1e-6
L2_EPS = 1e-6

A_QK = GDN_QK_HEADS * GDN_DK
A_V = GDN_V_HEADS * GDN_DV
A_CONV_CH = 2 * A_QK + A_V
B_Q = ATT_HEADS * ATT_DH
B_KV = ATT_KV_HEADS * ATT_DH
IDX_Q = IDX_HEADS * IDX_DH
SPLIT_SIZES = (A_CONV_CH, A_V, GDN_V_HEADS, GDN_V_HEADS, B_Q, B_KV, B_KV, B_Q, IDX_Q, IDX_DH, IDX_HEADS,
               D_MODEL, D_MODEL)
MAIN_SEGS = (0, 1, 4, 7, 8, 11, 12)
TAIL_SEGS = (5, 6, 9, 2, 3, 10)
OFF_AQKV = 0
OFF_AZ = OFF_AQKV + A_CONV_CH
OFF_BQ = OFF_AZ + A_V
OFF_BZ = OFF_BQ + B_Q
OFF_IQ = OFF_BZ + B_Q
OFF_GA = OFF_IQ + IDX_Q
OFF_GB = OFF_GA + D_MODEL
N_MAIN = OFF_GB + D_MODEL
N_TAIL = 2 * B_KV + IDX_DH + 128
LANE_BETA = 0
LANE_DECAY = GDN_V_HEADS
LANE_IW = 2 * GDN_V_HEADS

VMEM_LIMIT = 56 * 1024 * 1024
NEG_INF = float("-inf")
INT_MIN = -2 ** 31


def _cparams(sem):
    return pltpu.CompilerParams(dimension_semantics=sem, vmem_limit_bytes=VMEM_LIMIT)


def _dot(a, b):
    return jnp.dot(a, b, preferred_element_type=F32)


def _dot_nt(a, b):
    return lax.dot_general(a, b, (((1,), (1,)), ((), ())), preferred_element_type=F32)


def _dot_tn(a, b):
    return lax.dot_general(a, b, (((0,), (0,)), ((), ())), preferred_element_type=F32)


def _split2(x):
    hi = x.astype(BF16)
    lo = (x - hi.astype(F32)).astype(BF16)
    return hi, lo


def _split3(x):
    x1 = x.astype(BF16)
    r = x - x1.astype(F32)
    x2 = r.astype(BF16)
    x3 = (r - x2.astype(F32)).astype(BF16)
    return x1, x2, x3


def _dot_3pass(a, b):
    ah, al = _split2(a)
    bh, bl = _split2(b)
    return _dot(ah, bh) + _dot(al, bh) + _dot(ah, bl)


def _silu(x):
    return x * jax.nn.sigmoid(x)


def _mod_kernel(c_ref, w_ref, b_ref, o_ref):
    o_ref[...] = _dot_3pass(_silu(c_ref[...]), w_ref[...]) + b_ref[...]


def _mod_call(c_pad, w_ada, b_ada):
    rows, d = c_pad.shape
    n = w_ada.shape[1]
    tn = 768
    return pl.pallas_call(
        _mod_kernel,
        grid=(n // tn,),
        in_specs=[pl.BlockSpec((rows, d), lambda j: (0, 0)),
                  pl.BlockSpec((d, tn), lambda j: (0, j)),
                  pl.BlockSpec((1, tn), lambda j: (0, j))],
        out_specs=pl.BlockSpec((rows, tn), lambda j: (0, j)),
        out_shape=jax.ShapeDtypeStruct((rows, n), F32),
        compiler_params=_cparams(("arbitrary",)),
        name="adaln_mod",
    )(c_pad, w_ada, b_ada)


def _norm_kernel(x_ref, g_ref, scale_ref, shift_ref, o_ref):
    x = x_ref[...]
    y = x * lax.rsqrt(jnp.mean(x * x, axis=-1, keepdims=True) + NORM_EPS) * g_ref[...]
    o_ref[...] = (y * (1.0 + scale_ref[...]) + shift_ref[...]).astype(o_ref.dtype)


def _norm_call(x2d, g_row, scale3, shift3, tm, tiles_per_group):
    m, d = x2d.shape
    r = scale3.shape[1]
    mod_spec = pl.BlockSpec((None, r, d), lambda i: (i // tiles_per_group, 0, 0))
    return pl.pallas_call(
        _norm_kernel,
        grid=(m // tm,),
        in_specs=[pl.BlockSpec((tm, d), lambda i: (i, 0)),
                  pl.BlockSpec((1, d), lambda i: (0, 0)),
                  mod_spec, mod_spec],
        out_specs=pl.BlockSpec((tm, d), lambda i: (i, 0)),
        out_shape=jax.ShapeDtypeStruct((m, d), BF16),
        compiler_params=_cparams(("parallel",)),
        name="norm_modulate",
    )(x2d, g_row, scale3, shift3)


def _proj_kernel(h_ref, w_ref, o_ref):
    o_ref[...] = _dot(h_ref[...], w_ref[...]).astype(o_ref.dtype)


def _proj_call(h, w, tm, tn, out_dtype, name):
    m, k = h.shape
    n = w.shape[1]
    return pl.pallas_call(
        _proj_kernel,
        grid=(m // tm, n // tn),
        in_specs=[pl.BlockSpec((tm, k), lambda i, j: (i, 0)),
                  pl.BlockSpec((k, tn), lambda i, j: (0, j))],
        out_specs=pl.BlockSpec((tm, tn), lambda i, j: (i, j)),
        out_shape=jax.ShapeDtypeStruct((m, n), out_dtype),
        compiler_params=_cparams(("parallel", "arbitrary")),
        name=name,
    )(h, w)


def _proj_tail_kernel(h_ref, w_ref, k_ref, v_ref, kidx_ref, small_ref):
    r = _dot(h_ref[...], w_ref[...])
    k_ref[...] = r[:, 0:B_KV]
    v_ref[...] = r[:, B_KV:2 * B_KV]
    kidx_ref[...] = r[:, 2 * B_KV:2 * B_KV + IDX_DH]
    small_ref[...] = r[:, 2 * B_KV + IDX_DH:]


def _proj_tail_call(h, w_tail, tm):
    m, k = h.shape
    widths = (B_KV, B_KV, IDX_DH, 128)
    return pl.pallas_call(
        _proj_tail_kernel,
        grid=(m // tm,),
        in_specs=[pl.BlockSpec((tm, k), lambda i: (i, 0)),
                  pl.BlockSpec((k, N_TAIL), lambda i: (0, 0))],
        out_specs=[pl.BlockSpec((tm, w), lambda i: (i, 0)) for w in widths],
        out_shape=[jax.ShapeDtypeStruct((m, w), F32) for w in widths],
        compiler_params=_cparams(("parallel",)),
        name="in_proj_tail",
    )(h, w_tail)


def _gates_kernel(sm_ref, alog_ref, dtb_ref, o_ref, *, chunk):
    x = sm_ref[...]
    tm = x.shape[0]
    lane = lax.broadcasted_iota(I32, x.shape, 1)
    beta = jax.nn.sigmoid(x)
    z = x + dtb_ref[...]
    softplus = jnp.maximum(z, 0.0) + jnp.log1p(jnp.exp(-jnp.abs(z)))
    g = -jnp.exp(alog_ref[...]) * softplus
    g = jnp.where((lane >= LANE_DECAY) & (lane < LANE_DECAY + GDN_V_HEADS), g, 0.0)
    ri = lax.broadcasted_iota(I32, (tm, tm), 0)
    ci = lax.broadcasted_iota(I32, (tm, tm), 1)
    shift = int(math.log2(chunk))
    tri = jnp.where((ri >= ci) & ((ri >> shift) == (ci >> shift)), 1.0, 0.0).astype(BF16)
    g1, g2, g3 = _split3(g)
    gc = _dot(tri, g1) + _dot(tri, g2) + _dot(tri, g3)
    o_ref[...] = jnp.where(lane < GDN_V_HEADS, beta, gc)


def _gates_call(small, alog_row, dtb_row, chunk):
    m = small.shape[0]
    tm = 256
    return pl.pallas_call(
        functools.partial(_gates_kernel, chunk=chunk),
        grid=(m // tm,),
        in_specs=[pl.BlockSpec((tm, 128), lambda i: (i, 0)),
                  pl.BlockSpec((1, 128), lambda i: (0, 0)),
                  pl.BlockSpec((1, 128), lambda i: (0, 0))],
        out_specs=pl.BlockSpec((tm, 128), lambda i: (i, 0)),
        out_shape=jax.ShapeDtypeStruct((m, 128), F32),
        compiler_params=_cparams(("parallel",)),
        name="gdn_gates",
    )(small, alog_row, dtb_row)


def _unrolled_loop(n, unroll, fn):
    unroll = min(unroll, n)
    assert n % unroll == 0

    def body(t, carry):
        for u in range(unroll):
            fn(t * unroll + u)
        return carry

    lax.fori_loop(0, n // unroll, body, 0)


def _gdn_kernel(*refs, L, C, HP, has_state):
    ins = 13 if has_state else 12
    (q_ref, k_ref, v_ref, z_ref, bq_ref, bk_ref, bv_ref, wq_ref, wk_ref, wv_ref, gates_ref, gn_ref) = refs[:12]
    s0_ref = refs[12] if has_state else None
    (y_ref, sout_ref, xq_s, xk_s, xv_s, beta_s, g_s, st_s, a_s, p_s, rhs_s, kt_s, qkd_s, qd_s,
     m_s, n_s, qe_s, o_s) = refs[ins:]
    nh = 2 * HP
    nc = L // C
    hq0 = pl.program_id(1) * HP

    xq_s[0:8, :] = bq_ref[...]
    xk_s[0:8, :] = bk_ref[...]
    xv_s[0:8, :] = bv_ref[...]
    xq_s[8:, :] = q_ref[...].astype(F32)
    xk_s[8:, :] = k_ref[...].astype(F32)
    xv_s[8:, :] = v_ref[...].astype(F32)

    gates = gates_ref[...]
    lane = lax.broadcasted_iota(I32, gates.shape, 1)
    for idx in range(nh):
        hv = 2 * hq0 + idx
        bcol = jnp.sum(jnp.where(lane == hv + LANE_BETA, gates, 0.0), axis=-1, keepdims=True)
        gcol = jnp.sum(jnp.where(lane == hv + LANE_DECAY, gates, 0.0), axis=-1, keepdims=True)
        beta_s[idx] = jnp.broadcast_to(bcol, (L, 128))
        g_s[idx] = jnp.broadcast_to(gcol, (L, 128))
    if has_state:
        st_s[...] = s0_ref[...]
    else:
        st_s[...] = jnp.zeros(st_s.shape, F32)

    ii = lax.broadcasted_iota(I32, (C, C), 0)
    jj = lax.broadcasted_iota(I32, (C, C), 1)
    gn = gn_ref[...]

    def conv(win, w):
        acc = (win[5:5 + C] * w[0:1] + win[6:6 + C] * w[1:2] + win[7:7 + C] * w[2:3] + win[8:8 + C] * w[3:4])
        return _silu(acc)

    def l2n(x):
        return x * lax.rsqrt(jnp.sum(x * x, axis=-1, keepdims=True) + L2_EPS)

    def lanes(i, width):
        if isinstance(i, int):
            return slice(i * width, (i + 1) * width)
        return pl.ds(pl.multiple_of(i * width, width), width)

    def pair_terms(j):
        c, p = (j, 0) if HP == 1 else (j // HP, j % HP)
        r0 = pl.multiple_of(c * C, C)
        rows = pl.ds(r0, C)
        win = pl.ds(r0, C + 8)
        lq, lv = lanes(p, 128), lanes(p, 256)
        q = l2n(conv(xq_s[win, lq], wq_ref[:, lq])) * (GDN_DK ** -0.5)
        k = l2n(conv(xk_s[win, lq], wk_ref[:, lq]))
        vc = conv(xv_s[win, lv], wv_ref[:, lv])
        k16 = k.astype(BF16)
        kk = _dot_nt(k16, k16)
        qk = _dot_nt(q.astype(BF16), k16)
        for r in range(2):
            idx = 2 * p + r
            e = c * nh + idx
            gb = g_s[idx, rows, :]
            bb = beta_s[idx, rows, :]
            e_g = jnp.exp(gb)
            e_tail = jnp.exp(gb[C - 1:C, :] - gb)
            if C < 128:
                gpad = jnp.concatenate([gb, jnp.zeros((128 - C, 128), F32)], axis=0)
            else:
                gpad = gb
            dexp = gb[:, 0:C] - gpad.T[0:C, 0:C]
            dm = jnp.exp(jnp.where(ii >= jj, dexp, NEG_INF))
            a = jnp.where(ii > jj, kk * dm, 0.0) * bb[:, 0:C]
            a_s[e] = a.astype(a_s.dtype)
            p_s[e] = jnp.where(ii == jj, 1.0, 0.0) - a
            rhs_s[e] = jnp.concatenate([vc[:, r * 128:(r + 1) * 128] * bb, k * bb * e_g], axis=1).astype(rhs_s.dtype)
            kt_s[e] = (k * e_tail).astype(kt_s.dtype)
            qkd_s[e] = (qk * dm).astype(qkd_s.dtype)
            qd_s[e] = q * e_g

    def square(e):
        ak = a_s[e].astype(BF16)
        a_s[e] = _dot(ak, ak).astype(a_s.dtype)

    def inverse_stage(e, last):
        ak = a_s[e].astype(BF16)
        pk = p_s[e]
        p_s[e] = pk + _dot(pk.astype(BF16), ak)
        if not last:
            a_s[e] = _dot(ak, ak).astype(a_s.dtype)

    def solve(e):
        rhs_s[e] = _dot(p_s[e].astype(BF16), rhs_s[e].astype(BF16)).astype(rhs_s.dtype)

    def state_terms(e):
        sol16 = rhs_s[e].astype(BF16)
        mn = _dot_tn(kt_s[e].astype(BF16), sol16)
        qo = _dot(qkd_s[e].astype(BF16), sol16)
        n_s[e] = mn[:, 0:GDN_DV]
        m_s[e] = mn[:, GDN_DV:].astype(m_s.dtype)
        o_s[e] = qo[:, 0:GDN_DV]
        qe_s[e] = (qd_s[e] - qo[:, GDN_DV:]).astype(qe_s.dtype)

    def head_step(c, idx):
        e = c * nh + idx
        r0 = pl.multiple_of(c * C, C)
        rows = pl.ds(r0, C)
        s = st_s[idx]
        s16 = s.astype(BF16)
        o = _dot(qe_s[e].astype(BF16), s16) + o_s[e]
        e_last = jnp.exp(g_s[idx, pl.ds(r0 + C - 1, 1), :])
        st_s[idx] = s * e_last - _dot(m_s[e].astype(BF16), s16) + n_s[e]
        on = o * lax.rsqrt(jnp.mean(o * o, axis=-1, keepdims=True) + NORM_EPS) * gn
        zz = z_ref[rows, lanes(idx, 128)].astype(F32)
        y_ref[rows, lanes(idx, 128)] = (on * _silu(zz)).astype(y_ref.dtype)

    ne = nc * nh
    stages = int(math.log2(C)) - 1
    _unrolled_loop(nc * HP, 2, pair_terms)
    _unrolled_loop(ne, 16, square)
    for st in range(stages):
        _unrolled_loop(ne, 16, functools.partial(inverse_stage, last=(st == stages - 1)))
    _unrolled_loop(ne, 16, solve)
    _unrolled_loop(ne, 8, state_terms)

    def chunk_step(c, carry):
        if nh <= 4:
            for idx in range(nh):
                head_step(c, idx)
        else:
            _unrolled_loop(nh, 8, functools.partial(head_step, c))
        return carry

    lax.fori_loop(0, nc, chunk_step, 0)
    sout_ref[...] = st_s[...]


def _gdn_call(big, convbuf8, conv_w, gates, gn_row, s0, *, B, L, C, HP, y_dtype):
    has_state = s0 is not None
    ne = (L // C) * 2 * HP
    half = BF16 if C % 16 == 0 else F32
    nq = GDN_QK_HEADS // HP
    wq, wv = HP * 128, HP * 256
    kq0 = A_QK // wq
    v0 = (2 * A_QK) // wv
    z0 = OFF_AZ // wv
    in_specs = [
        pl.BlockSpec((L, wq), lambda b, h: (b, h)),
        pl.BlockSpec((L, wq), lambda b, h: (b, kq0 + h)),
        pl.BlockSpec((L, wv), lambda b, h: (b, v0 + h)),
        pl.BlockSpec((L, wv), lambda b, h: (b, z0 + h)),
        pl.BlockSpec((None, 8, wq), lambda b, h: (b, 0, h)),
        pl.BlockSpec((None, 8, wq), lambda b, h: (b, 0, kq0 + h)),
        pl.BlockSpec((None, 8, wv), lambda b, h: (b, 0, v0 + h)),
        pl.BlockSpec((CONV_W, wq), lambda b, h: (0, h)),
        pl.BlockSpec((CONV_W, wq), lambda b, h: (0, kq0 + h)),
        pl.BlockSpec((CONV_W, wv), lambda b, h: (0, v0 + h)),
        pl.BlockSpec((L, 128), lambda b, h: (b, 0)),
        pl.BlockSpec((1, 128), lambda b, h: (0, 0)),
    ]
    args = [big, big, big, big, convbuf8, convbuf8, convbuf8, conv_w, conv_w, conv_w, gates, gn_row]
    if has_state:
        in_specs.append(pl.BlockSpec((None, 2 * HP, GDN_DK, GDN_DV), lambda b, h: (b, h, 0, 0)))
        args.append(s0)
    return pl.pallas_call(
        functools.partial(_gdn_kernel, L=L, C=C, HP=HP, has_state=has_state),
        grid=(B, nq),
        in_specs=in_specs,
        out_specs=[pl.BlockSpec((L, wv), lambda b, h: (b, h)),
                   pl.BlockSpec((None, 2 * HP, GDN_DK, GDN_DV), lambda b, h: (b, h, 0, 0))],
        out_shape=[jax.ShapeDtypeStruct((B * L, A_V), y_dtype),
                   jax.ShapeDtypeStruct((B, GDN_V_HEADS, GDN_DK, GDN_DV), F32)],
        scratch_shapes=[pltpu.VMEM((L + 8, wq), F32), pltpu.VMEM((L + 8, wq), F32), pltpu.VMEM((L + 8, wv), F32),
                        pltpu.VMEM((2 * HP, L, 128), F32), pltpu.VMEM((2 * HP, L, 128), F32),
                        pltpu.VMEM((2 * HP, GDN_DK, GDN_DV), F32),
                        pltpu.VMEM((ne, C, C), half), pltpu.VMEM((ne, C, C), F32),
                        pltpu.VMEM((ne, C, GDN_DV + GDN_DK), half), pltpu.VMEM((ne, C, GDN_DK), half),
                        pltpu.VMEM((ne, C, C), half), pltpu.VMEM((ne, C, GDN_DK), F32),
                        pltpu.VMEM((ne, GDN_DK, GDN_DV), BF16), pltpu.VMEM((ne, GDN_DK, GDN_DV), F32),
                        pltpu.VMEM((ne, C, GDN_DK), half), pltpu.VMEM((ne, C, GDN_DV), F32)],
        compiler_params=_cparams(("parallel", "arbitrary")),
        name="gdn_prompt" if not has_state else "gdn_sample",
    )(*args)


KEY_NEG_INF = -2 ** 31 + 0x7FFFFF


def _key_to_float(key):
    bits = key ^ ((key >> 31) & jnp.int32(0x7FFFFFFF))
    return pltpu.bitcast(bits, F32)


def _kth_largest(score_ref, rows, topk):
    def step(it, t):
        cand = t + jnp.left_shift(jnp.int32(1), 31 - it)
        cnt = jnp.sum(jnp.where(score_ref[...] >= _key_to_float(cand), 1.0, 0.0), axis=-1, keepdims=True)
        return jnp.where((cnt >= float(topk)) | (cand <= KEY_NEG_INF), cand, t)

    return _key_to_float(lax.fori_loop(0, 32, step, jnp.full((rows, 1), INT_MIN, I32)))


def _selection_bias(score_ref, bias_ref, causal, rows, width, topk):
    t = _kth_largest(score_ref, rows, topk)
    score = score_ref[...]
    sel = (score >= t) & causal
    bias_ref[...] = jnp.where(sel, 0.0, NEG_INF)
    n_sel = jnp.sum(jnp.where(sel, 1.0, 0.0), axis=-1, keepdims=True)
    has_ties = jnp.max(n_sel) > float(topk)

    @pl.when(has_ties)
    def _():
        need = float(topk) - jnp.sum(jnp.where(score > t, 1.0, 0.0), axis=-1, keepdims=True)
        eq16 = jnp.where(score == t, 1.0, 0.0).astype(BF16)

        def tile(j, carry):
            c0 = pl.multiple_of(j * 128, 128)
            ri = lax.broadcasted_iota(I32, (width, 128), 0)
            ci = lax.broadcasted_iota(I32, (width, 128), 1) + c0
            upper = jnp.where(ri <= ci, 1.0, 0.0).astype(BF16)
            prefix = _dot(eq16, upper)
            s_t = score_ref[:, pl.ds(c0, 128)]
            keep = (s_t > t) | ((s_t == t) & (prefix <= need))
            old = bias_ref[:, pl.ds(c0, 128)]
            bias_ref[:, pl.ds(c0, 128)] = jnp.where(keep, old, NEG_INF)
            return carry

        lax.fori_loop(0, width // 128, tile, 0)


def _dsa_prompt_kernel(qi_ref, wi_ref, kidx_ref, q_ref, k_ref, v_ref, z_ref, o_ref, score_s, bias_s, *, QB, L, topk):
    q0 = pl.program_id(1) * QB
    kidx16 = kidx_ref[...].astype(BF16)
    wi = wi_ref[...] * ((IDX_DH ** -0.5) * (IDX_HEADS ** -0.5))
    score = jnp.zeros((QB, L), F32)
    for h in range(IDX_HEADS):
        lg = _dot_nt(qi_ref[:, h * IDX_DH:(h + 1) * IDX_DH], kidx16)
        score = score + jnp.maximum(lg, 0.0) * wi[:, LANE_IW + h:LANE_IW + h + 1]
    qpos = q0 + lax.broadcasted_iota(I32, (QB, L), 0)
    kpos = lax.broadcasted_iota(I32, (QB, L), 1)
    causal = kpos <= qpos
    score_s[...] = jnp.where(causal, score, NEG_INF)
    _selection_bias(score_s, bias_s, causal, QB, L, topk)

    groups = ATT_HEADS // ATT_KV_HEADS
    for n in range(ATT_KV_HEADS):
        kn = k_ref[:, n * ATT_DH:(n + 1) * ATT_DH].astype(BF16)
        vn = v_ref[:, n * ATT_DH:(n + 1) * ATT_DH].astype(BF16)
        for g in range(groups):
            h = n * groups + g
            cols = slice(h * ATT_DH, (h + 1) * ATT_DH)
            s = _dot_nt(q_ref[:, cols], kn) * (ATT_DH ** -0.5) + bias_s[...]
            m = jnp.max(s, axis=-1, keepdims=True)
            p = jnp.exp(s - m)
            denom = jnp.sum(p, axis=-1, keepdims=True)
            o = _dot(p.astype(BF16), vn) / denom
            zz = z_ref[:, cols].astype(F32)
            o_ref[:, cols] = (o * _silu(zz)).astype(o_ref.dtype)


def _dsa_prompt_call(big, small, kidx, k, v, *, B, L, QB):
    topk = min(TOPK_MAX, L // 4)
    nqb = L // QB
    iq0 = OFF_IQ // B_Q
    bq0 = OFF_BQ // B_Q
    bz0 = OFF_BZ // B_Q
    return pl.pallas_call(
        functools.partial(_dsa_prompt_kernel, QB=QB, L=L, topk=topk),
        grid=(B, nqb),
        in_specs=[pl.BlockSpec((QB, IDX_Q), lambda b, i: (b * nqb + i, iq0)),
                  pl.BlockSpec((QB, 128), lambda b, i: (b * nqb + i, 0)),
                  pl.BlockSpec((L, IDX_DH), lambda b, i: (b, 0)),
                  pl.BlockSpec((QB, B_Q), lambda b, i: (b * nqb + i, bq0)),
                  pl.BlockSpec((L, B_KV), lambda b, i: (b, 0)),
                  pl.BlockSpec((L, B_KV), lambda b, i: (b, 0)),
                  pl.BlockSpec((QB, B_Q), lambda b, i: (b * nqb + i, bz0))],
        out_specs=pl.BlockSpec((QB, B_Q), lambda b, i: (b * nqb + i, 0)),
        out_shape=jax.ShapeDtypeStruct((B * L, B_Q), BF16),
        scratch_shapes=[pltpu.VMEM((QB, L), F32), pltpu.VMEM((QB, L), F32)],
        compiler_params=_cparams(("parallel", "arbitrary")),
        name="dsa_prompt",
    )(big, small, kidx, big, k, v, big)


def _dsa_sample_kernel(pt_ref, qi_ref, wi_ref, kin_ref, q_ref, kn_ref, vn_ref, z_ref, ck_hbm, cv_hbm, ci_hbm,
                       o_ref, kbuf, vbuf, ibuf, sems, score_s, bias_s, *, NP, T, topk):
    b = pl.program_id(0)
    nb = pl.num_programs(0)
    slot = b % 2
    past = NP * PAGE_SIZE
    width = past + 128

    kvp = ATT_KV_HEADS * PAGE_SIZE

    def page_copies(bb, sl, p):
        page = pt_ref[bb * NP + p]
        rows = pl.ds(pl.multiple_of(p * PAGE_SIZE, PAGE_SIZE), PAGE_SIZE)
        src2 = pl.ds(pl.multiple_of(page * kvp, kvp), kvp)
        dst2 = pl.ds(pl.multiple_of(p * kvp, kvp), kvp)
        return (pltpu.make_async_copy(ck_hbm.at[src2], kbuf.at[sl, dst2], sems.at[0, sl]),
                pltpu.make_async_copy(cv_hbm.at[src2], vbuf.at[sl, dst2], sems.at[1, sl]),
                pltpu.make_async_copy(ci_hbm.at[page], ibuf.at[sl, rows], sems.at[2, sl]))

    def start_fetch(bb, sl):
        def body(p, carry):
            for cp in page_copies(bb, sl, p):
                cp.start()
            return carry
        lax.fori_loop(0, NP, body, 0)

    def wait_fetch(bb, sl):
        def body(p, carry):
            for cp in page_copies(bb, sl, p):
                cp.wait()
            return carry
        lax.fori_loop(0, NP, body, 0)

    @pl.when(b == 0)
    def _():
        for sl in range(2):
            kbuf[sl, ATT_KV_HEADS * past:, :] = jnp.zeros((ATT_KV_HEADS * 128, ATT_DH), F32)
            vbuf[sl, ATT_KV_HEADS * past:, :] = jnp.zeros((ATT_KV_HEADS * 128, ATT_DH), F32)
            ibuf[sl, past:, :] = jnp.zeros((128, IDX_DH), F32)
        start_fetch(0, 0)

    @pl.when(b + 1 < nb)
    def _():
        start_fetch(b + 1, 1 - slot)

    for n in range(ATT_KV_HEADS):
        new_rows = pl.ds(ATT_KV_HEADS * past + n, T, stride=ATT_KV_HEADS)
        kbuf[slot, new_rows, :] = kn_ref[:, n * ATT_DH:(n + 1) * ATT_DH]
        vbuf[slot, new_rows, :] = vn_ref[:, n * ATT_DH:(n + 1) * ATT_DH]
    ibuf[slot, past:past + T, :] = kin_ref[...]
    wait_fetch(b, slot)

    qi = qi_ref[...]
    qis = jnp.concatenate([qi[:, h * IDX_DH:(h + 1) * IDX_DH] for h in range(IDX_HEADS)], axis=0).astype(BF16)
    wi = wi_ref[...] * ((IDX_DH ** -0.5) * (IDX_HEADS ** -0.5))
    lg = _dot_nt(qis, ibuf[slot].astype(BF16))
    score = jnp.zeros((T, width), F32)
    for h in range(IDX_HEADS):
        score = score + jnp.maximum(lg[h * T:(h + 1) * T, :], 0.0) * wi[:, LANE_IW + h:LANE_IW + h + 1]
    qpos = past + lax.broadcasted_iota(I32, (T, width), 0)
    kpos = lax.broadcasted_iota(I32, (T, width), 1)
    causal = kpos <= qpos
    score_s[...] = jnp.where(causal, score, NEG_INF)
    _selection_bias(score_s, bias_s, causal, T, width, topk)

    groups = ATT_HEADS // ATT_KV_HEADS
    q = q_ref[...]
    bias = jnp.concatenate([bias_s[...]] * groups, axis=0)
    for n in range(ATT_KV_HEADS):
        head_rows = pl.ds(n, width, stride=ATT_KV_HEADS)
        kn = kbuf[slot, head_rows, :].astype(BF16)
        vn = vbuf[slot, head_rows, :].astype(BF16)
        qs = jnp.concatenate([q[:, (n * groups + g) * ATT_DH:(n * groups + g + 1) * ATT_DH] for g in range(groups)],
                             axis=0).astype(BF16)
        s = _dot_nt(qs, kn) * (ATT_DH ** -0.5) + bias
        m = jnp.max(s, axis=-1, keepdims=True)
        p = jnp.exp(s - m)
        denom = jnp.sum(p, axis=-1, keepdims=True)
        o = _dot(p.astype(BF16), vn) / denom
        for g in range(groups):
            h = n * groups + g
            cols = slice(h * ATT_DH, (h + 1) * ATT_DH)
            zz = z_ref[:, cols]
            o_ref[:, cols] = o[g * T:(g + 1) * T, :] * _silu(zz)


def _dsa_sample_call(big, small, kidx, k, v, cache_k, cache_v, cache_kidx, page_table, *, B, T):
    npages = page_table.shape[1]
    past = npages * PAGE_SIZE
    topk = min(TOPK_MAX, (past + T) // 4)
    width = past + 128
    iq0 = OFF_IQ // B_Q
    bq0 = OFF_BQ // B_Q
    bz0 = OFF_BZ // B_Q
    grid_spec = pltpu.PrefetchScalarGridSpec(
        num_scalar_prefetch=1,
        grid=(B,),
        in_specs=[pl.BlockSpec((T, IDX_Q), lambda b, pt: (b, iq0)),
                  pl.BlockSpec((T, 128), lambda b, pt: (b, 0)),
                  pl.BlockSpec((T, IDX_DH), lambda b, pt: (b, 0)),
                  pl.BlockSpec((T, B_Q), lambda b, pt: (b, bq0)),
                  pl.BlockSpec((T, B_KV), lambda b, pt: (b, 0)),
                  pl.BlockSpec((T, B_KV), lambda b, pt: (b, 0)),
                  pl.BlockSpec((T, B_Q), lambda b, pt: (b, bz0)),
                  pl.BlockSpec(memory_space=pl.ANY),
                  pl.BlockSpec(memory_space=pl.ANY),
                  pl.BlockSpec(memory_space=pl.ANY)],
        out_specs=pl.BlockSpec((T, B_Q), lambda b, pt: (b, 0)),
        scratch_shapes=[pltpu.VMEM((2, ATT_KV_HEADS * width, ATT_DH), F32),
                        pltpu.VMEM((2, ATT_KV_HEADS * width, ATT_DH), F32),
                        pltpu.VMEM((2, width, IDX_DH), F32), pltpu.SemaphoreType.DMA((3, 2)),
                        pltpu.VMEM((T, width), F32), pltpu.VMEM((T, width), F32)])
    return pl.pallas_call(
        functools.partial(_dsa_sample_kernel, NP=npages, T=T, topk=topk),
        grid_spec=grid_spec,
        out_shape=jax.ShapeDtypeStruct((B * T, B_Q), F32),
        compiler_params=_cparams(("arbitrary",)),
        name="dsa_sample",
    )(page_table.reshape(-1), big, small, kidx, big, k, v, big, cache_k, cache_v, cache_kidx)


def _merge_kernel(ya_ref, yb_ref, ga_ref, gb_ref, wpa_ref, wpb_ref, o_ref):
    pa = _dot(ya_ref[...].astype(BF16), wpa_ref[...])
    pb = _dot(yb_ref[...].astype(BF16), wpb_ref[...])
    o = jax.nn.sigmoid(ga_ref[...].astype(F32)) * pa + jax.nn.sigmoid(gb_ref[...].astype(F32)) * pb
    o_ref[...] = o.astype(o_ref.dtype)


def _merge_call(ya, yb, big, wpa, wpb, tm, tn):
    m = ya.shape[0]
    n = wpa.shape[1]
    ga0 = OFF_GA // tn
    gb0 = OFF_GB // tn
    return pl.pallas_call(
        _merge_kernel,
        grid=(m // tm, n // tn),
        in_specs=[pl.BlockSpec((tm, A_V), lambda i, j: (i, 0)),
                  pl.BlockSpec((tm, B_Q), lambda i, j: (i, 0)),
                  pl.BlockSpec((tm, tn), lambda i, j: (i, ga0 + j)),
                  pl.BlockSpec((tm, tn), lambda i, j: (i, gb0 + j)),
                  pl.BlockSpec((A_V, tn), lambda i, j: (0, j)),
                  pl.BlockSpec((B_Q, tn), lambda i, j: (0, j))],
        out_specs=pl.BlockSpec((tm, tn), lambda i, j: (i, j)),
        out_shape=jax.ShapeDtypeStruct((m, n), BF16),
        compiler_params=_cparams(("parallel", "arbitrary")),
        name="merge_proj",
    )(ya, yb, big, big, wpa, wpb)


def _final_kernel(m_ref, w_ref, x_ref, gate_ref, g_ref, o_ref):
    t = _dot(m_ref[...], w_ref[...])
    y = t * lax.rsqrt(jnp.mean(t * t, axis=-1, keepdims=True) + NORM_EPS) * g_ref[...]
    o_ref[...] = x_ref[...] + gate_ref[...] * y


def _final_call(merged, w_out, x2d, gate3, g_row, tm, tiles_per_group):
    m, d = x2d.shape
    r = gate3.shape[1]
    return pl.pallas_call(
        _final_kernel,
        grid=(m // tm,),
        in_specs=[pl.BlockSpec((tm, d), lambda i: (i, 0)),
                  pl.BlockSpec((d, d), lambda i: (0, 0)),
                  pl.BlockSpec((tm, d), lambda i: (i, 0)),
                  pl.BlockSpec((None, r, d), lambda i: (i // tiles_per_group, 0, 0)),
                  pl.BlockSpec((1, d), lambda i: (0, 0))],
        out_specs=pl.BlockSpec((tm, d), lambda i: (i, 0)),
        out_shape=jax.ShapeDtypeStruct((m, d), F32),
        compiler_params=_cparams(("parallel",)),
        name="out_proj",
    )(merged, w_out, x2d, gate3, g_row)


def _relayout_in_proj(w):
    offs = [0]
    for s in SPLIT_SIZES:
        offs.append(offs[-1] + s)

    def seg(i):
        return w[:, offs[i]:offs[i + 1]]

    w_main = jnp.concatenate([seg(i) for i in MAIN_SEGS], axis=1).astype(BF16)
    pad = N_TAIL - sum(SPLIT_SIZES[i] for i in TAIL_SEGS)
    w_tail = jnp.concatenate([seg(i) for i in TAIL_SEGS] + [jnp.zeros((w.shape[0], pad), w.dtype)],
                             axis=1).astype(BF16)
    return w_main, w_tail


def _lane_row(vec, lane0):
    return jnp.zeros((1, 128), F32).at[0, lane0:lane0 + vec.shape[0]].set(vec.astype(F32))


def _layer(x, mod, conv_hist, s0, attend, wts, *, gdn_chunk, gdn_hp, tm_proj, tn_proj, act_dtype):
    (pre_g, w_main, w_tail, conv_w, alog_row, dtb_row, gn_row, wpa, wpb, wout, post_g) = wts
    B, L, D = x.shape
    M = B * L
    x2d = x.reshape(M, D)
    shift, scale, gate = mod[:, 0:D], mod[:, D:2 * D], mod[:, 2 * D:3 * D]
    if L >= 256:
        tm_n = 256
        per = L // tm_n
        scale3, shift3, gate3 = scale[:, None, :], shift[:, None, :], gate[:, None, :]
    else:
        tm_n = M
        per = 1
        scale3, shift3, gate3 = (jnp.repeat(t, L, axis=0)[None] for t in (scale, shift, gate))
    h = _norm_call(x2d, pre_g, scale3, shift3, tm_n, per)
    tm = min(tm_proj, M)
    big = _proj_call(h, w_main, tm, tn_proj, act_dtype, "in_proj_main")
    k, v, kidx, small = _proj_tail_call(h, w_tail, tm)

    gates = _gates_call(small, alog_row, dtb_row, gdn_chunk)
    if conv_hist is None:
        convbuf8 = jnp.zeros((B, 8, A_CONV_CH), F32)
    else:
        convbuf8 = jnp.concatenate([jnp.zeros((B, 5, A_CONV_CH), F32), conv_hist.astype(F32)], axis=1)
    y_a, s_new = _gdn_call(big, convbuf8, conv_w, gates, gn_row, s0, B=B, L=L, C=gdn_chunk, HP=gdn_hp,
                           y_dtype=act_dtype)
    y_b = attend(big, small, kidx, k, v)
    tm_m = min(1024, M)
    merged = _merge_call(y_a, y_b, big, wpa, wpb, tm_m, 512)
    tm_f = min(512, M)
    per_f = (L // tm_f) if L >= 256 else 1
    y = _final_call(merged, wout, x2d, gate3, post_g, tm_f, per_f)
    return y.reshape(B, L, D), big, h, k, v, kidx, s_new


def kernel(x_prompt, x_sample, c_prompt, c_sample, cache_k, cache_v, cache_kidx, state_gdn, state_conv,
           page_table, w_ada, b_ada, pre_norm_g, w_in, conv_w, a_log, dt_bias, gdn_norm_g, w_pa, w_pb, w_out,
           post_norm_g):
    depth = w_in.shape[0]
    assert depth == 1, "single-layer trunk"
    BP, LP, D = x_prompt.shape
    BS, LS, _ = x_sample.shape
    n_pool = cache_k.shape[1]

    w_main, w_tail = _relayout_in_proj(w_in[0])
    wts = (pre_norm_g[0][None, :], w_main, w_tail, conv_w[0],
           _lane_row(a_log[0], LANE_DECAY), _lane_row(dt_bias[0], LANE_DECAY), gdn_norm_g[0][None, :],
           w_pa[0].astype(BF16), w_pb[0].astype(BF16), w_out[0].astype(BF16), post_norm_g[0][None, :])

    rows = BP + BS
    rows_pad = -(-rows // 16) * 16
    c_all = jnp.concatenate([c_prompt, c_sample, jnp.zeros((rows_pad - rows, D), F32)], axis=0)
    mod = _mod_call(c_all, w_ada[0], b_ada[0][None, :])
    mod_p, mod_s = mod[0:BP], mod[BP:BP + BS]

    attend_p = functools.partial(_dsa_prompt_call, B=BP, L=LP, QB=256)
    y_p, big_p, h_p, k_p, v_p, kidx_p, s_p = _layer(
        x_prompt, mod_p, None, None, attend_p, wts,
        gdn_chunk=min(GDN_CHUNK, LP), gdn_hp=1, tm_proj=1024, tn_proj=512, act_dtype=BF16)
    h_last = h_p.reshape(BP, LP, D)[:, LP - 8:, :].reshape(BP * 8, D)
    aqkv_last = _proj_call(h_last, w_main[:, 0:A_CONV_CH], BP * 8, 512, F32, "in_proj_conv_tail")
    conv_p = aqkv_last.reshape(BP, 8, A_CONV_CH)[:, 8 - (CONV_W - 1):, :]

    ck = cache_k[0].reshape(n_pool * PAGE_SIZE * ATT_KV_HEADS, ATT_DH)
    cv = cache_v[0].reshape(n_pool * PAGE_SIZE * ATT_KV_HEADS, ATT_DH)
    ci = cache_kidx[0]
    attend_s = functools.partial(_dsa_sample_call, cache_k=ck, cache_v=cv, cache_kidx=ci, page_table=page_table,
                                 B=BS, T=LS)
    y_s, big_s, _, k_s, v_s, kidx_s, s_s = _layer(
        x_sample, mod_s, state_conv[0], state_gdn[0], attend_s, wts,
        gdn_chunk=min(GDN_CHUNK, LS), gdn_hp=16, tm_proj=256, tn_proj=512, act_dtype=F32)
    conv_s = big_s[:, 0:A_CONV_CH].reshape(BS, LS, A_CONV_CH)[:, LS - (CONV_W - 1):, :]

    kv_shape_p = (1, BP, LP, ATT_KV_HEADS, ATT_DH)
    kv_shape_s = (1, BS, LS, ATT_KV_HEADS, ATT_DH)
    return (y_p, y_s,
            k_p.reshape(kv_shape_p), v_p.reshape(kv_shape_p), kidx_p.reshape(1, BP, LP, IDX_DH),
            s_p[None], conv_p[None],
            k_s.reshape(kv_shape_s), v_s.reshape(kv_shape_s), kidx_s.reshape(1, BS, LS, IDX_DH),
            s_s[None], conv_s[None])
```

```python
import functools
import math

import jax
import jax.numpy as jnp
from jax import lax
from jax.experimental import pallas as pl
from jax.experimental.pallas import tpu as pltpu

F32 = jnp.float32
BF16 = jnp.bfloat16
I32 = jnp.int32

D_MODEL = 2048
PAGE_SIZE = 128
GDN_QK_HEADS = 16
GDN_V_HEADS = 32
GDN_DK = 128
GDN_DV = 128
CONV_W = 4
GDN_CHUNK = 64
ATT_HEADS = 16
ATT_KV_HEADS = 2
ATT_DH = 128
IDX_HEADS = 16
IDX_DH = 128
TOPK_MAX = 256
NORM_EPS = 1e-6
L2_EPS = 1e-6

A_QK = GDN_QK_HEADS * GDN_DK
A_V = GDN_V_HEADS * GDN_DV
A_CONV_CH = 2 * A_QK + A_V
B_Q = ATT_HEADS * ATT_DH
B_KV = ATT_KV_HEADS * ATT_DH
IDX_Q = IDX_HEADS * IDX_DH
SPLIT_SIZES = (A_CONV_CH, A_V, GDN_V_HEADS, GDN_V_HEADS, B_Q, B_KV, B_KV, B_Q, IDX_Q, IDX_DH, IDX_HEADS,
               D_MODEL, D_MODEL)
B_SEGS = (4, 7, 8, 11, 12)
TAIL_SEGS = (5, 6, 9, 2, 3, 10)
OFF_AQKV = 0
OFF_AZ = OFF_AQKV + A_CONV_CH
N_A = OFF_AZ + A_V
OFF_BQ = 0
OFF_BZ = OFF_BQ + B_Q
OFF_IQ = OFF_BZ + B_Q
OFF_GA = OFF_IQ + IDX_Q
OFF_GB = OFF_GA + D_MODEL
N_B = OFF_GB + D_MODEL
N_TAIL = 2 * B_KV + IDX_DH + 128
LANE_BETA = 0
LANE_DECAY = GDN_V_HEADS
LANE_IW = 2 * GDN_V_HEADS

VMEM_LIMIT = 56 * 1024 * 1024
NEG_INF = float("-inf")
INT_MIN = -2 ** 31


def _cparams(sem):
    return pltpu.CompilerParams(dimension_semantics=sem, vmem_limit_bytes=VMEM_LIMIT)


def _dot(a, b):
    return jnp.dot(a, b, preferred_element_type=F32)


def _dot_nt(a, b):
    return lax.dot_general(a, b, (((1,), (1,)), ((), ())), preferred_element_type=F32)


def _dot_tn(a, b):
    return lax.dot_general(a, b, (((0,), (0,)), ((), ())), preferred_element_type=F32)


def _split2(x):
    hi = x.astype(BF16)
    lo = (x - hi.astype(F32)).astype(BF16)
    return hi, lo


def _split3(x):
    x1 = x.astype(BF16)
    r = x - x1.astype(F32)
    x2 = r.astype(BF16)
    x3 = (r - x2.astype(F32)).astype(BF16)
    return x1, x2, x3


def _dot_3pass(a, b):
    ah, al = _split2(a)
    bh, bl = _split2(b)
    return _dot(ah, bh) + _dot(al, bh) + _dot(ah, bl)


def _silu(x):
    return x * jax.nn.sigmoid(x)


def _mod_kernel(c_ref, w_ref, b_ref, o_ref):
    o_ref[...] = _dot_3pass(_silu(c_ref[...]), w_ref[...]) + b_ref[...]


def _mod_call(c_pad, w_ada, b_ada):
    rows, d = c_pad.shape
    n = w_ada.shape[1]
    tn = 768
    return pl.pallas_call(
        _mod_kernel,
        grid=(n // tn,),
        in_specs=[pl.BlockSpec((rows, d), lambda j: (0, 0)),
                  pl.BlockSpec((d, tn), lambda j: (0, j)),
                  pl.BlockSpec((1, tn), lambda j: (0, j))],
        out_specs=pl.BlockSpec((rows, tn), lambda j: (0, j)),
        out_shape=jax.ShapeDtypeStruct((rows, n), F32),
        compiler_params=_cparams(("arbitrary",)),
        name="adaln_mod",
    )(c_pad, w_ada, b_ada)


def _norm_kernel(x_ref, g_ref, scale_ref, shift_ref, o_ref):
    x = x_ref[...]
    y = x * lax.rsqrt(jnp.mean(x * x, axis=-1, keepdims=True) + NORM_EPS) * g_ref[...]
    o_ref[...] = (y * (1.0 + scale_ref[...]) + shift_ref[...]).astype(o_ref.dtype)


def _norm_call(x2d, g_row, scale3, shift3, tm, tiles_per_group):
    m, d = x2d.shape
    r = scale3.shape[1]
    mod_spec = pl.BlockSpec((None, r, d), lambda i: (i // tiles_per_group, 0, 0))
    return pl.pallas_call(
        _norm_kernel,
        grid=(m // tm,),
        in_specs=[pl.BlockSpec((tm, d), lambda i: (i, 0)),
                  pl.BlockSpec((1, d), lambda i: (0, 0)),
                  mod_spec, mod_spec],
        out_specs=pl.BlockSpec((tm, d), lambda i: (i, 0)),
        out_shape=jax.ShapeDtypeStruct((m, d), BF16),
        compiler_params=_cparams(("parallel",)),
        name="norm_modulate",
    )(x2d, g_row, scale3, shift3)


def _proj_kernel(h_ref, w_ref, o_ref):
    o_ref[...] = _dot(h_ref[...], w_ref[...]).astype(o_ref.dtype)


def _proj_call(h, w, tm, tn, out_dtype, name):
    m, k = h.shape
    n = w.shape[1]
    return pl.pallas_call(
        _proj_kernel,
        grid=(m // tm, n // tn),
        in_specs=[pl.BlockSpec((tm, k), lambda i, j: (i, 0)),
                  pl.BlockSpec((k, tn), lambda i, j: (0, j))],
        out_specs=pl.BlockSpec((tm, tn), lambda i, j: (i, j)),
        out_shape=jax.ShapeDtypeStruct((m, n), out_dtype),
        compiler_params=_cparams(("parallel", "arbitrary")),
        name=name,
    )(h, w)


def _proj_f32w_kernel(h_ref, w_ref, o_ref):
    o_ref[...] = _dot(h_ref[...], w_ref[...].astype(BF16)).astype(o_ref.dtype)


def _proj_f32w_call(h, w3, n_cols, tm, tn, out_dtype, name):
    m, k = h.shape
    return pl.pallas_call(
        _proj_f32w_kernel,
        grid=(m // tm, n_cols // tn),
        in_specs=[pl.BlockSpec((tm, k), lambda i, j: (i, 0)),
                  pl.BlockSpec((None, k, tn), lambda i, j: (0, 0, j))],
        out_specs=pl.BlockSpec((tm, tn), lambda i, j: (i, j)),
        out_shape=jax.ShapeDtypeStruct((m, n_cols), out_dtype),
        compiler_params=_cparams(("parallel", "arbitrary")),
        name=name,
    )(h, w3)


def _proj_tail_kernel(h_ref, w_ref, k_ref, v_ref, kidx_ref, small_ref):
    r = _dot(h_ref[...], w_ref[...])
    k_ref[...] = r[:, 0:B_KV]
    v_ref[...] = r[:, B_KV:2 * B_KV]
    kidx_ref[...] = r[:, 2 * B_KV:2 * B_KV + IDX_DH]
    small_ref[...] = r[:, 2 * B_KV + IDX_DH:]


def _proj_tail_call(h, w_tail, tm):
    m, k = h.shape
    widths = (B_KV, B_KV, IDX_DH, 128)
    return pl.pallas_call(
        _proj_tail_kernel,
        grid=(m // tm,),
        in_specs=[pl.BlockSpec((tm, k), lambda i: (i, 0)),
                  pl.BlockSpec((k, N_TAIL), lambda i: (0, 0))],
        out_specs=[pl.BlockSpec((tm, w), lambda i: (i, 0)) for w in widths],
        out_shape=[jax.ShapeDtypeStruct((m, w), F32) for w in widths],
        compiler_params=_cparams(("parallel",)),
        name="in_proj_tail",
    )(h, w_tail)


def _gates_kernel(sm_ref, alog_ref, dtb_ref, o_ref, *, chunk):
    x = sm_ref[...]
    tm = x.shape[0]
    lane = lax.broadcasted_iota(I32, x.shape, 1)
    beta = jax.nn.sigmoid(x)
    z = x + dtb_ref[...]
    softplus = jnp.maximum(z, 0.0) + jnp.log1p(jnp.exp(-jnp.abs(z)))
    g = -jnp.exp(alog_ref[...]) * softplus
    g = jnp.where((lane >= LANE_DECAY) & (lane < LANE_DECAY + GDN_V_HEADS), g, 0.0)
    ri = lax.broadcasted_iota(I32, (tm, tm), 0)
    ci = lax.broadcasted_iota(I32, (tm, tm), 1)
    shift = int(math.log2(chunk))
    tri = jnp.where((ri >= ci) & ((ri >> shift) == (ci >> shift)), 1.0, 0.0).astype(BF16)
    g1, g2, g3 = _split3(g)
    gc = _dot(tri, g1) + _dot(tri, g2) + _dot(tri, g3)
    o_ref[...] = jnp.where(lane < GDN_V_HEADS, beta, gc)


def _gates_call(small, alog_row, dtb_row, chunk):
    m = small.shape[0]
    tm = 256
    return pl.pallas_call(
        functools.partial(_gates_kernel, chunk=chunk),
        grid=(m // tm,),
        in_specs=[pl.BlockSpec((tm, 128), lambda i: (i, 0)),
                  pl.BlockSpec((1, 128), lambda i: (0, 0)),
                  pl.BlockSpec((1, 128), lambda i: (0, 0))],
        out_specs=pl.BlockSpec((tm, 128), lambda i: (i, 0)),
        out_shape=jax.ShapeDtypeStruct((m, 128), F32),
        compiler_params=_cparams(("parallel",)),
        name="gdn_gates",
    )(small, alog_row, dtb_row)


def _unrolled_loop(n, unroll, fn):
    unroll = min(unroll, n)
    assert n % unroll == 0

    def body(t, carry):
        for u in range(unroll):
            fn(t * unroll + u)
        return carry

    lax.fori_loop(0, n // unroll, body, 0)


def _gdn_kernel(*refs, L, C, HP, has_state):
    ins = 13 if has_state else 12
    (q_ref, k_ref, v_ref, z_ref, bq_ref, bk_ref, bv_ref, wq_ref, wk_ref, wv_ref, gates_ref, gn_ref) = refs[:12]
    s0_ref = refs[12] if has_state else None
    (y_ref, sout_ref, xq_s, xk_s, xv_s, beta_s, g_s, st_s, a_s, p_s, rhs_s, kt_s, qkd_s, qd_s,
     m_s, n_s, qe_s, o_s) = refs[ins:]
    nh = 2 * HP
    nc = L // C
    hq0 = pl.program_id(1) * HP

    xq_s[0:8, :] = bq_ref[...]
    xk_s[0:8, :] = bk_ref[...]
    xv_s[0:8, :] = bv_ref[...]
    xq_s[8:, :] = q_ref[...].astype(F32)
    xk_s[8:, :] = k_ref[...].astype(F32)
    xv_s[8:, :] = v_ref[...].astype(F32)

    gates = gates_ref[...]
    lane = lax.broadcasted_iota(I32, gates.shape, 1)
    for idx in range(nh):
        hv = 2 * hq0 + idx
        bcol = jnp.sum(jnp.where(lane == hv + LANE_BETA, gates, 0.0), axis=-1, keepdims=True)
        gcol = jnp.sum(jnp.where(lane == hv + LANE_DECAY, gates, 0.0), axis=-1, keepdims=True)
        beta_s[idx] = jnp.broadcast_to(bcol, (L, 128))
        g_s[idx] = jnp.broadcast_to(gcol, (L, 128))
    if has_state:
        st_s[...] = s0_ref[...]
    else:
        st_s[...] = jnp.zeros(st_s.shape, F32)

    ii = lax.broadcasted_iota(I32, (C, C), 0)
    jj = lax.broadcasted_iota(I32, (C, C), 1)
    gn = gn_ref[...]

    def conv(win, w):
        acc = (win[5:5 + C] * w[0:1] + win[6:6 + C] * w[1:2] + win[7:7 + C] * w[2:3] + win[8:8 + C] * w[3:4])
        return _silu(acc)

    def l2n(x):
        return x * lax.rsqrt(jnp.sum(x * x, axis=-1, keepdims=True) + L2_EPS)

    def lanes(i, width):
        if isinstance(i, int):
            return slice(i * width, (i + 1) * width)
        return pl.ds(pl.multiple_of(i * width, width), width)

    def pair_terms(j):
        c, p = (j, 0) if HP == 1 else (j // HP, j % HP)
        r0 = pl.multiple_of(c * C, C)
        rows = pl.ds(r0, C)
        win = pl.ds(r0, C + 8)
        lq, lv = lanes(p, 128), lanes(p, 256)
        q = l2n(conv(xq_s[win, lq], wq_ref[:, lq])) * (GDN_DK ** -0.5)
        k = l2n(conv(xk_s[win, lq], wk_ref[:, lq]))
        vc = conv(xv_s[win, lv], wv_ref[:, lv])
        k16 = k.astype(BF16)
        kk = _dot_nt(k16, k16)
        qk = _dot_nt(q.astype(BF16), k16)
        for r in range(2):
            idx = 2 * p + r
            e = c * nh + idx
            gb = g_s[idx, rows, :]
            bb = beta_s[idx, rows, :]
            e_g = jnp.exp(gb)
            e_tail = jnp.exp(gb[C - 1:C, :] - gb)
            if C < 128:
                gpad = jnp.concatenate([gb, jnp.zeros((128 - C, 128), F32)], axis=0)
            else:
                gpad = gb
            dexp = gb[:, 0:C] - gpad.T[0:C, 0:C]
            dm = jnp.exp(jnp.where(ii >= jj, dexp, NEG_INF))
            a = jnp.where(ii > jj, kk * dm, 0.0) * bb[:, 0:C]
            a_s[e] = a.astype(a_s.dtype)
            p_s[e] = jnp.where(ii == jj, 1.0, 0.0) - a
            rhs_s[e] = jnp.concatenate([vc[:, r * 128:(r + 1) * 128] * bb, k * bb * e_g], axis=1).astype(rhs_s.dtype)
            kt_s[e] = (k * e_tail).astype(kt_s.dtype)
            qkd_s[e] = (qk * dm).astype(qkd_s.dtype)
            qd_s[e] = q * e_g

    def square(e):
        ak = a_s[e].astype(BF16)
        a_s[e] = _dot(ak, ak).astype(a_s.dtype)

    def inverse_stage(e, last):
        ak = a_s[e].astype(BF16)
        pk = p_s[e]
        p_s[e] = pk + _dot(pk.astype(BF16), ak)
        if not last:
            a_s[e] = _dot(ak, ak).astype(a_s.dtype)

    def solve(e):
        rhs_s[e] = _dot(p_s[e].astype(BF16), rhs_s[e].astype(BF16)).astype(rhs_s.dtype)

    def state_terms(e):
        sol16 = rhs_s[e].astype(BF16)
        mn = _dot_tn(kt_s[e].astype(BF16), sol16)
        qo = _dot(qkd_s[e].astype(BF16), sol16)
        n_s[e] = mn[:, 0:GDN_DV]
        m_s[e] = mn[:, GDN_DV:].astype(m_s.dtype)
        o_s[e] = qo[:, 0:GDN_DV]
        qe_s[e] = (qd_s[e] - qo[:, GDN_DV:]).astype(qe_s.dtype)

    def head_step(c, idx):
        e = c * nh + idx
        r0 = pl.multiple_of(c * C, C)
        rows = pl.ds(r0, C)
        s = st_s[idx]
        s16 = s.astype(BF16)
        o = _dot(qe_s[e].astype(BF16), s16) + o_s[e]
        e_last = jnp.exp(g_s[idx, pl.ds(r0 + C - 1, 1), :])
        st_s[idx] = s * e_last - _dot(m_s[e].astype(BF16), s16) + n_s[e]
        on = o * lax.rsqrt(jnp.mean(o * o, axis=-1, keepdims=True) + NORM_EPS) * gn
        zz = z_ref[rows, lanes(idx, 128)].astype(F32)
        y_ref[rows, lanes(idx, 128)] = (on * _silu(zz)).astype(y_ref.dtype)

    ne = nc * nh
    stages = int(math.log2(C)) - 1
    _unrolled_loop(nc * HP, 2, pair_terms)
    _unrolled_loop(ne, 16, square)
    for st in range(stages):
        _unrolled_loop(ne, 16, functools.partial(inverse_stage, last=(st == stages - 1)))
    _unrolled_loop(ne, 16, solve)
    _unrolled_loop(ne, 8, state_terms)

    def chunk_step(c, carry):
        if nh <= 4:
            for idx in range(nh):
                head_step(c, idx)
        else:
            _unrolled_loop(nh, 8, functools.partial(head_step, c))
        return carry

    lax.fori_loop(0, nc, chunk_step, 0)
    sout_ref[...] = st_s[...]


def _gdn_call(big, convbuf8, conv_w, gates, gn_row, s0, *, B, L, C, HP, y_dtype):
    has_state = s0 is not None
    ne = (L // C) * 2 * HP
    half = BF16 if C % 16 == 0 else F32
    nq = GDN_QK_HEADS // HP
    wq, wv = HP * 128, HP * 256
    kq0 = A_QK // wq
    v0 = (2 * A_QK) // wv
    z0 = OFF_AZ // wv
    in_specs = [
        pl.BlockSpec((L, wq), lambda b, h: (b, h)),
        pl.BlockSpec((L, wq), lambda b, h: (b, kq0 + h)),
        pl.BlockSpec((L, wv), lambda b, h: (b, v0 + h)),
        pl.BlockSpec((L, wv), lambda b, h: (b, z0 + h)),
        pl.BlockSpec((None, 8, wq), lambda b, h: (b, 0, h)),
        pl.BlockSpec((None, 8, wq), lambda b, h: (b, 0, kq0 + h)),
        pl.BlockSpec((None, 8, wv), lambda b, h: (b, 0, v0 + h)),
        pl.BlockSpec((CONV_W, wq), lambda b, h: (0, h)),
        pl.BlockSpec((CONV_W, wq), lambda b, h: (0, kq0 + h)),
        pl.BlockSpec((CONV_W, wv), lambda b, h: (0, v0 + h)),
        pl.BlockSpec((L, 128), lambda b, h: (b, 0)),
        pl.BlockSpec((1, 128), lambda b, h: (0, 0)),
    ]
    args = [big, big, big, big, convbuf8, convbuf8, convbuf8, conv_w, conv_w, conv_w, gates, gn_row]
    if has_state:
        in_specs.append(pl.BlockSpec((None, 2 * HP, GDN_DK, GDN_DV), lambda b, h: (b, h, 0, 0)))
        args.append(s0)
    return pl.pallas_call(
        functools.partial(_gdn_kernel, L=L, C=C, HP=HP, has_state=has_state),
        grid=(B, nq),
        in_specs=in_specs,
        out_specs=[pl.BlockSpec((L, wv), lambda b, h: (b, h)),
                   pl.BlockSpec((None, 2 * HP, GDN_DK, GDN_DV), lambda b, h: (b, h, 0, 0))],
        out_shape=[jax.ShapeDtypeStruct((B * L, A_V), y_dtype),
                   jax.ShapeDtypeStruct((B, GDN_V_HEADS, GDN_DK, GDN_DV), F32)],
        scratch_shapes=[pltpu.VMEM((L + 8, wq), F32), pltpu.VMEM((L + 8, wq), F32), pltpu.VMEM((L + 8, wv), F32),
                        pltpu.VMEM((2 * HP, L, 128), F32), pltpu.VMEM((2 * HP, L, 128), F32),
                        pltpu.VMEM((2 * HP, GDN_DK, GDN_DV), F32),
                        pltpu.VMEM((ne, C, C), half), pltpu.VMEM((ne, C, C), F32),
                        pltpu.VMEM((ne, C, GDN_DV + GDN_DK), half), pltpu.VMEM((ne, C, GDN_DK), half),
                        pltpu.VMEM((ne, C, C), half), pltpu.VMEM((ne, C, GDN_DK), F32),
                        pltpu.VMEM((ne, GDN_DK, GDN_DV), BF16), pltpu.VMEM((ne, GDN_DK, GDN_DV), F32),
                        pltpu.VMEM((ne, C, GDN_DK), half), pltpu.VMEM((ne, C, GDN_DV), F32)],
        compiler_params=_cparams(("parallel", "arbitrary")),
        name="gdn_prompt" if not has_state else "gdn_sample",
    )(*args)


KEY_NEG_INF = -2 ** 31 + 0x7FFFFF


def _key_to_float(key):
    bits = key ^ ((key >> 31) & jnp.int32(0x7FFFFFFF))
    return pltpu.bitcast(bits, F32)


def _selection_bias(score_ref, bias_ref, qpos0, rows, width, topk):
    def step(it, t):
        cand = t + jnp.left_shift(jnp.int32(1), 31 - it)
        ge = score_ref[:, 0:width] >= _key_to_float(cand)
        cnt = jnp.sum(jnp.where(ge, 1.0, 0.0), axis=-1, keepdims=True)
        return jnp.where((cnt >= float(topk)) | (cand <= KEY_NEG_INF), cand, t)

    thr = _key_to_float(lax.fori_loop(0, 32, step, jnp.full((rows, 1), INT_MIN, I32)))
    score = score_ref[:, 0:width]
    causal = (lax.broadcasted_iota(I32, (rows, width), 1)
              <= qpos0 + lax.broadcasted_iota(I32, (rows, width), 0))
    sel = (score >= thr) & causal
    bias_ref[:, 0:width] = jnp.where(sel, 0.0, NEG_INF)
    n_sel = jnp.sum(jnp.where(sel, 1.0, 0.0), axis=-1, keepdims=True)
    has_ties = jnp.max(n_sel) > float(topk)

    @pl.when(has_ties)
    def _():
        need = float(topk) - jnp.sum(jnp.where(score > thr, 1.0, 0.0), axis=-1, keepdims=True)
        eq16 = jnp.where(score == thr, 1.0, 0.0).astype(BF16)

        def tile(j, carry):
            c0 = pl.multiple_of(j * 128, 128)
            ri = lax.broadcasted_iota(I32, (width, 128), 0)
            ci = lax.broadcasted_iota(I32, (width, 128), 1) + c0
            upper = jnp.where(ri <= ci, 1.0, 0.0).astype(BF16)
            prefix = _dot(eq16, upper)
            s_t = score_ref[:, pl.ds(c0, 128)]
            keep = (s_t > thr) | ((s_t == thr) & (prefix <= need))
            old = bias_ref[:, pl.ds(c0, 128)]
            bias_ref[:, pl.ds(c0, 128)] = jnp.where(keep, old, NEG_INF)
            return carry

        lax.fori_loop(0, width // 128, tile, 0)


SOFTMAX_LOG2_SCALE = (ATT_DH ** -0.5) * math.log2(math.e)


def _dsa_prompt_kernel(qi_ref, wi_ref, kidx_ref, q_ref, k_ref, v_ref, z_ref, o_ref,
                       score_s, bias_s, kidx16_s, k16_s, vext_s, m_s, acc_s, *, QB, KT, L, topk):
    i = pl.program_id(1)
    q0 = i * QB
    nt = (q0 + QB - 1) // KT + 1
    groups = ATT_HEADS // ATT_KV_HEADS

    @pl.when(i == 0)
    def _():
        kidx16_s[...] = kidx_ref[...].astype(BF16)
        k16_s[...] = k_ref[...].astype(BF16)
        for n in range(ATT_KV_HEADS):
            vext_s[n, :, 0:ATT_DH] = v_ref[:, n * ATT_DH:(n + 1) * ATT_DH].astype(BF16)
            vext_s[n, :, ATT_DH:] = jnp.ones((L, ATT_DH), BF16)

    wi = wi_ref[...] * ((IDX_DH ** -0.5) * (IDX_HEADS ** -0.5))
    rowpos = q0 + lax.broadcasted_iota(I32, (QB, KT), 0)
    col = lax.broadcasted_iota(I32, (QB, KT), 1)

    def tile_rows(t):
        return pl.ds(pl.multiple_of(t * KT, KT), KT)

    def index_tile(t, carry):
        kid = kidx16_s[tile_rows(t), :]
        acc = jnp.zeros((QB, KT), F32)
        for h in range(IDX_HEADS):
            lg = _dot_nt(qi_ref[:, h * IDX_DH:(h + 1) * IDX_DH], kid)
            acc = acc + jnp.maximum(lg, 0.0) * wi[:, LANE_IW + h:LANE_IW + h + 1]
        score_s[:, tile_rows(t)] = jnp.where(col + t * KT <= rowpos, acc, NEG_INF)
        return carry

    lax.fori_loop(0, nt, index_tile, 0)
    for w in range(1, L // KT + 1):
        @pl.when(nt == w)
        def _():
            _selection_bias(score_s, bias_s, q0, QB, w * KT, topk)

    m_s[...] = jnp.full(m_s.shape, NEG_INF, F32)
    acc_s[...] = jnp.zeros(acc_s.shape, F32)

    def attend_tile(t, carry):
        bias = bias_s[:, tile_rows(t)]
        for n in range(ATT_KV_HEADS):
            kn = k16_s[tile_rows(t), n * ATT_DH:(n + 1) * ATT_DH]
            vn = vext_s[n, tile_rows(t), :]
            for g in range(groups):
                h = n * groups + g
                sb = _dot_nt(q_ref[:, h * ATT_DH:(h + 1) * ATT_DH], kn) + bias
                m_old = m_s[h]
                m_new = jnp.maximum(m_old, jnp.max(sb, axis=-1, keepdims=True))
                m_ref = jnp.where(m_new == NEG_INF, 0.0, m_new)
                p = jnp.exp2((sb - jnp.concatenate([m_ref] * (KT // 128), axis=1)) * SOFTMAX_LOG2_SCALE)
                alpha = jnp.exp2((m_old - m_ref) * SOFTMAX_LOG2_SCALE)
                m_s[h] = m_new
                acc_s[h] = acc_s[h] * jnp.concatenate([alpha, alpha], axis=1) + _dot(p.astype(BF16), vn)
        return carry

    lax.fori_loop(0, nt, attend_tile, 0)
    for h in range(ATT_HEADS):
        cols = slice(h * ATT_DH, (h + 1) * ATT_DH)
        zz = z_ref[:, cols].astype(F32)
        acc = acc_s[h]
        o_ref[:, cols] = (acc[:, 0:ATT_DH] / acc[:, ATT_DH:] * _silu(zz)).astype(o_ref.dtype)


def _dsa_prompt_call(big, small, kidx, k, v, *, B, L, QB):
    topk = min(TOPK_MAX, L // 4)
    nqb = L // QB
    iq0 = OFF_IQ // B_Q
    bq0 = OFF_BQ // B_Q
    bz0 = OFF_BZ // B_Q
    return pl.pallas_call(
        functools.partial(_dsa_prompt_kernel, QB=QB, KT=256, L=L, topk=topk),
        grid=(B, nqb),
        in_specs=[pl.BlockSpec((QB, IDX_Q), lambda b, i: (b * nqb + i, iq0)),
                  pl.BlockSpec((QB, 128), lambda b, i: (b * nqb + i, 0)),
                  pl.BlockSpec((L, IDX_DH), lambda b, i: (b, 0)),
                  pl.BlockSpec((QB, B_Q), lambda b, i: (b * nqb + i, bq0)),
                  pl.BlockSpec((L, B_KV), lambda b, i: (b, 0)),
                  pl.BlockSpec((L, B_KV), lambda b, i: (b, 0)),
                  pl.BlockSpec((QB, B_Q), lambda b, i: (b * nqb + i, bz0))],
        out_specs=pl.BlockSpec((QB, B_Q), lambda b, i: (b * nqb + i, 0)),
        out_shape=jax.ShapeDtypeStruct((B * L, B_Q), BF16),
        scratch_shapes=[pltpu.VMEM((QB, L), F32), pltpu.VMEM((QB, L), F32),
                        pltpu.VMEM((L, IDX_DH), BF16), pltpu.VMEM((L, B_KV), BF16),
                        pltpu.VMEM((ATT_KV_HEADS, L, 2 * ATT_DH), BF16),
                        pltpu.VMEM((ATT_HEADS, QB, 128), F32), pltpu.VMEM((ATT_HEADS, QB, 2 * ATT_DH), F32)],
        compiler_params=_cparams(("parallel", "arbitrary")),
        name="dsa_prompt",
    )(big, small, kidx, big, k, v, big)


def _dsa_sample_kernel(pt_ref, qi_ref, wi_ref, kin_ref, q_ref, kn_ref, vn_ref, z_ref, ck_hbm, cv_hbm, ci_hbm,
                       o_ref, kbuf, vbuf, ibuf, sems, score_s, bias_s, *, NP, T, topk):
    b = pl.program_id(0)
    nb = pl.num_programs(0)
    slot = b % 2
    past = NP * PAGE_SIZE
    width = past + 128

    kvp = ATT_KV_HEADS * PAGE_SIZE

    def page_copies(bb, sl, p):
        page = pt_ref[bb * NP + p]
        rows = pl.ds(pl.multiple_of(p * PAGE_SIZE, PAGE_SIZE), PAGE_SIZE)
        src2 = pl.ds(pl.multiple_of(page * kvp, kvp), kvp)
        dst2 = pl.ds(pl.multiple_of(p * kvp, kvp), kvp)
        return (pltpu.make_async_copy(ck_hbm.at[src2], kbuf.at[sl, dst2], sems.at[0, sl]),
                pltpu.make_async_copy(cv_hbm.at[src2], vbuf.at[sl, dst2], sems.at[1, sl]),
                pltpu.make_async_copy(ci_hbm.at[page], ibuf.at[sl, rows], sems.at[2, sl]))

    def start_fetch(bb, sl):
        def body(p, carry):
            for cp in page_copies(bb, sl, p):
                cp.start()
            return carry
        lax.fori_loop(0, NP, body, 0)

    def wait_fetch(bb, sl):
        def body(p, carry):
            for cp in page_copies(bb, sl, p):
                cp.wait()
            return carry
        lax.fori_loop(0, NP, body, 0)

    @pl.when(b == 0)
    def _():
        for sl in range(2):
            kbuf[sl, ATT_KV_HEADS * past:, :] = jnp.zeros((ATT_KV_HEADS * 128, ATT_DH), F32)
            vbuf[sl, ATT_KV_HEADS * past:, :] = jnp.zeros((ATT_KV_HEADS * 128, ATT_DH), F32)
            ibuf[sl, past:, :] = jnp.zeros((128, IDX_DH), F32)
        start_fetch(0, 0)

    @pl.when(b + 1 < nb)
    def _():
        start_fetch(b + 1, 1 - slot)

    for n in range(ATT_KV_HEADS):
        new_rows = pl.ds(ATT_KV_HEADS * past + n, T, stride=ATT_KV_HEADS)
        kbuf[slot, new_rows, :] = kn_ref[:, n * ATT_DH:(n + 1) * ATT_DH]
        vbuf[slot, new_rows, :] = vn_ref[:, n * ATT_DH:(n + 1) * ATT_DH]
    ibuf[slot, past:past + T, :] = kin_ref[...]
    wait_fetch(b, slot)

    qi = qi_ref[...]
    qis = jnp.concatenate([qi[:, h * IDX_DH:(h + 1) * IDX_DH] for h in range(IDX_HEADS)], axis=0).astype(BF16)
    wi = wi_ref[...] * ((IDX_DH ** -0.5) * (IDX_HEADS ** -0.5))
    lg = _dot_nt(qis, ibuf[slot].astype(BF16))
    score = jnp.zeros((T, width), F32)
    for h in range(IDX_HEADS):
        score = score + jnp.maximum(lg[h * T:(h + 1) * T, :], 0.0) * wi[:, LANE_IW + h:LANE_IW + h + 1]
    qpos = past + lax.broadcasted_iota(I32, (T, width), 0)
    kpos = lax.broadcasted_iota(I32, (T, width), 1)
    causal = kpos <= qpos
    score_s[...] = jnp.where(causal, score, NEG_INF)
    _selection_bias(score_s, bias_s, past, T, width, topk)

    groups = ATT_HEADS // ATT_KV_HEADS
    q = q_ref[...]
    bias = jnp.concatenate([bias_s[...]] * groups, axis=0)
    for n in range(ATT_KV_HEADS):
        head_rows = pl.ds(n, width, stride=ATT_KV_HEADS)
        kn = kbuf[slot, head_rows, :].astype(BF16)
        vn = vbuf[slot, head_rows, :].astype(BF16)
        qs = jnp.concatenate([q[:, (n * groups + g) * ATT_DH:(n * groups + g + 1) * ATT_DH] for g in range(groups)],
                             axis=0).astype(BF16)
        s = _dot_nt(qs, kn) * (ATT_DH ** -0.5) + bias
        m = jnp.max(s, axis=-1, keepdims=True)
        p = jnp.exp(s - m)
        denom = jnp.sum(p, axis=-1, keepdims=True)
        o = _dot(p.astype(BF16), vn) / denom
        for g in range(groups):
            h = n * groups + g
            cols = slice(h * ATT_DH, (h + 1) * ATT_DH)
            zz = z_ref[:, cols]
            o_ref[:, cols] = o[g * T:(g + 1) * T, :] * _silu(zz)


def _dsa_sample_call(big, small, kidx, k, v, cache_k, cache_v, cache_kidx, page_table, *, B, T):
    npages = page_table.shape[1]
    past = npages * PAGE_SIZE
    topk = min(TOPK_MAX, (past + T) // 4)
    width = past + 128
    iq0 = OFF_IQ // B_Q
    bq0 = OFF_BQ // B_Q
    bz0 = OFF_BZ // B_Q
    grid_spec = pltpu.PrefetchScalarGridSpec(
        num_scalar_prefetch=1,
        grid=(B,),
        in_specs=[pl.BlockSpec((T, IDX_Q), lambda b, pt: (b, iq0)),
                  pl.BlockSpec((T, 128), lambda b, pt: (b, 0)),
                  pl.BlockSpec((T, IDX_DH), lambda b, pt: (b, 0)),
                  pl.BlockSpec((T, B_Q), lambda b, pt: (b, bq0)),
                  pl.BlockSpec((T, B_KV), lambda b, pt: (b, 0)),
                  pl.BlockSpec((T, B_KV), lambda b, pt: (b, 0)),
                  pl.BlockSpec((T, B_Q), lambda b, pt: (b, bz0)),
                  pl.BlockSpec(memory_space=pl.ANY),
                  pl.BlockSpec(memory_space=pl.ANY),
                  pl.BlockSpec(memory_space=pl.ANY)],
        out_specs=pl.BlockSpec((T, B_Q), lambda b, pt: (b, 0)),
        scratch_shapes=[pltpu.VMEM((2, ATT_KV_HEADS * width, ATT_DH), F32),
                        pltpu.VMEM((2, ATT_KV_HEADS * width, ATT_DH), F32),
                        pltpu.VMEM((2, width, IDX_DH), F32), pltpu.SemaphoreType.DMA((3, 2)),
                        pltpu.VMEM((T, width), F32), pltpu.VMEM((T, width), F32)])
    return pl.pallas_call(
        functools.partial(_dsa_sample_kernel, NP=npages, T=T, topk=topk),
        grid_spec=grid_spec,
        out_shape=jax.ShapeDtypeStruct((B * T, B_Q), F32),
        compiler_params=_cparams(("arbitrary",)),
        name="dsa_sample",
    )(page_table.reshape(-1), big, small, kidx, big, k, v, big, cache_k, cache_v, cache_kidx)


def _merge_kernel(ya_ref, yb_ref, ga_ref, gb_ref, wpa_ref, wpb_ref, o_ref):
    pa = _dot(ya_ref[...].astype(BF16), wpa_ref[...])
    pb = _dot(yb_ref[...].astype(BF16), wpb_ref[...])
    o = jax.nn.sigmoid(ga_ref[...].astype(F32)) * pa + jax.nn.sigmoid(gb_ref[...].astype(F32)) * pb
    o_ref[...] = o.astype(o_ref.dtype)


def _merge_call(ya, yb, big, wpa, wpb, tm, tn):
    m = ya.shape[0]
    n = wpa.shape[1]
    ga0 = OFF_GA // tn
    gb0 = OFF_GB // tn
    return pl.pallas_call(
        _merge_kernel,
        grid=(m // tm, n // tn),
        in_specs=[pl.BlockSpec((tm, A_V), lambda i, j: (i, 0)),
                  pl.BlockSpec((tm, B_Q), lambda i, j: (i, 0)),
                  pl.BlockSpec((tm, tn), lambda i, j: (i, ga0 + j)),
                  pl.BlockSpec((tm, tn), lambda i, j: (i, gb0 + j)),
                  pl.BlockSpec((A_V, tn), lambda i, j: (0, j)),
                  pl.BlockSpec((B_Q, tn), lambda i, j: (0, j))],
        out_specs=pl.BlockSpec((tm, tn), lambda i, j: (i, j)),
        out_shape=jax.ShapeDtypeStruct((m, n), BF16),
        compiler_params=_cparams(("parallel", "arbitrary")),
        name="merge_proj",
    )(ya, yb, big, big, wpa, wpb)


def _final_kernel(m_ref, w_ref, x_ref, gate_ref, g_ref, o_ref):
    t = _dot(m_ref[...], w_ref[...])
    y = t * lax.rsqrt(jnp.mean(t * t, axis=-1, keepdims=True) + NORM_EPS) * g_ref[...]
    o_ref[...] = x_ref[...] + gate_ref[...] * y


def _final_call(merged, w_out, x2d, gate3, g_row, tm, tiles_per_group):
    m, d = x2d.shape
    r = gate3.shape[1]
    return pl.pallas_call(
        _final_kernel,
        grid=(m // tm,),
        in_specs=[pl.BlockSpec((tm, d), lambda i: (i, 0)),
                  pl.BlockSpec((d, d), lambda i: (0, 0)),
                  pl.BlockSpec((tm, d), lambda i: (i, 0)),
                  pl.BlockSpec((None, r, d), lambda i: (i // tiles_per_group, 0, 0)),
                  pl.BlockSpec((1, d), lambda i: (0, 0))],
        out_specs=pl.BlockSpec((tm, d), lambda i: (i, 0)),
        out_shape=jax.ShapeDtypeStruct((m, d), F32),
        compiler_params=_cparams(("parallel",)),
        name="out_proj",
    )(merged, w_out, x2d, gate3, g_row)


def _relayout_in_proj(w):
    offs = [0]
    for s in SPLIT_SIZES:
        offs.append(offs[-1] + s)

    def seg(i):
        return w[:, offs[i]:offs[i + 1]]

    w_b = jnp.concatenate([seg(i).astype(BF16) for i in B_SEGS], axis=1)
    pad = N_TAIL - sum(SPLIT_SIZES[i] for i in TAIL_SEGS)
    w_tail = jnp.concatenate([seg(i).astype(BF16) for i in TAIL_SEGS] + [jnp.zeros((w.shape[0], pad), BF16)],
                             axis=1)
    return w_b, w_tail


def _lane_row(vec, lane0):
    return jnp.zeros((1, 128), F32).at[0, lane0:lane0 + vec.shape[0]].set(vec.astype(F32))


def _layer(x, mod, conv_hist, s0, attend, wts, *, gdn_chunk, gdn_hp, tm_proj, tn_proj, act_dtype):
    (pre_g, w_in3, w_b, w_tail, conv_w, alog_row, dtb_row, gn_row, wpa, wpb, wout, post_g) = wts
    B, L, D = x.shape
    M = B * L
    x2d = x.reshape(M, D)
    shift, scale, gate = mod[:, 0:D], mod[:, D:2 * D], mod[:, 2 * D:3 * D]
    if L >= 256:
        tm_n = 256
        per = L // tm_n
        scale3, shift3, gate3 = scale[:, None, :], shift[:, None, :], gate[:, None, :]
    else:
        tm_n = M
        per = 1
        scale3, shift3, gate3 = (jnp.repeat(t, L, axis=0)[None] for t in (scale, shift, gate))
    h = _norm_call(x2d, pre_g, scale3, shift3, tm_n, per)
    tm = min(tm_proj, M)
    big_a = _proj_f32w_call(h, w_in3, N_A, tm, tn_proj, act_dtype, "in_proj_a")
    big_b = _proj_call(h, w_b, min(tm, 1024), tn_proj, act_dtype, "in_proj_b")
    k, v, kidx, small = _proj_tail_call(h, w_tail, min(tm, 1024))

    gates = _gates_call(small, alog_row, dtb_row, gdn_chunk)
    if conv_hist is None:
        convbuf8 = jnp.zeros((B, 8, A_CONV_CH), F32)
    else:
        convbuf8 = jnp.concatenate([jnp.zeros((B, 5, A_CONV_CH), F32), conv_hist.astype(F32)], axis=1)
    y_a, s_new = _gdn_call(big_a, convbuf8, conv_w, gates, gn_row, s0, B=B, L=L, C=gdn_chunk, HP=gdn_hp,
                           y_dtype=act_dtype)
    y_b = attend(big_b, small, kidx, k, v)
    tm_m = min(1024, M)
    merged = _merge_call(y_a, y_b, big_b, wpa, wpb, tm_m, 512)
    tm_f = min(512, M)
    per_f = (L // tm_f) if L >= 256 else 1
    y = _final_call(merged, wout, x2d, gate3, post_g, tm_f, per_f)
    return y.reshape(B, L, D), big_a, h, k, v, kidx, s_new


def kernel(x_prompt, x_sample, c_prompt, c_sample, cache_k, cache_v, cache_kidx, state_gdn, state_conv,
           page_table, w_ada, b_ada, pre_norm_g, w_in, conv_w, a_log, dt_bias, gdn_norm_g, w_pa, w_pb, w_out,
           post_norm_g):
    depth = w_in.shape[0]
    assert depth == 1, "single-layer trunk"
    BP, LP, D = x_prompt.shape
    BS, LS, _ = x_sample.shape
    n_pool = cache_k.shape[1]

    w_b, w_tail = _relayout_in_proj(w_in[0])
    wts = (pre_norm_g[0][None, :], w_in, w_b, w_tail, conv_w[0],
           _lane_row(a_log[0], LANE_DECAY), _lane_row(dt_bias[0], LANE_DECAY), gdn_norm_g[0][None, :],
           w_pa[0].astype(BF16), w_pb[0].astype(BF16), w_out[0].astype(BF16), post_norm_g[0][None, :])

    rows = BP + BS
    rows_pad = -(-rows // 16) * 16
    c_all = jnp.concatenate([c_prompt, c_sample, jnp.zeros((rows_pad - rows, D), F32)], axis=0)
    mod = _mod_call(c_all, w_ada[0], b_ada[0][None, :])
    mod_p, mod_s = mod[0:BP], mod[BP:BP + BS]

    attend_p = functools.partial(_dsa_prompt_call, B=BP, L=LP, QB=128)
    y_p, big_p, h_p, k_p, v_p, kidx_p, s_p = _layer(
        x_prompt, mod_p, None, None, attend_p, wts,
        gdn_chunk=min(GDN_CHUNK, LP), gdn_hp=1, tm_proj=2048, tn_proj=512, act_dtype=BF16)
    h_last = h_p.reshape(BP, LP, D)[:, LP - 8:, :].reshape(BP * 8, D)
    aqkv_last = _proj_f32w_call(h_last, w_in, A_CONV_CH, BP * 8, 512, F32, "in_proj_conv_tail")
    conv_p = aqkv_last.reshape(BP, 8, A_CONV_CH)[:, 8 - (CONV_W - 1):, :]

    ck = cache_k[0].reshape(n_pool * PAGE_SIZE * ATT_KV_HEADS, ATT_DH)
    cv = cache_v[0].reshape(n_pool * PAGE_SIZE * ATT_KV_HEADS, ATT_DH)
    ci = cache_kidx[0]
    attend_s = functools.partial(_dsa_sample_call, cache_k=ck, cache_v=cv, cache_kidx=ci, page_table=page_table,
                                 B=BS, T=LS)
    y_s, big_s, _, k_s, v_s, kidx_s, s_s = _layer(
        x_sample, mod_s, state_conv[0], state_gdn[0], attend_s, wts,
        gdn_chunk=min(GDN_CHUNK, LS), gdn_hp=16, tm_proj=256, tn_proj=512, act_dtype=F32)
    conv_s = big_s[:, 0:A_CONV_CH].reshape(BS, LS, A_CONV_CH)[:, LS - (CONV_W - 1):, :]

    kv_shape_p = (1, BP, LP, ATT_KV_HEADS, ATT_DH)
    kv_shape_s = (1, BS, LS, ATT_KV_HEADS, ATT_DH)
    return (y_p, y_s,
            k_p.reshape(kv_shape_p), v_p.reshape(kv_shape_p), kidx_p.reshape(1, BP, LP, IDX_DH),
            s_p[None], conv_p[None],
            k_s.reshape(kv_shape_s), v_s.reshape(kv_shape_s), kidx_s.reshape(1, BS, LS, IDX_DH),
            s_s[None], conv_s[None])
```

```python
import functools
import math

import jax
import jax.numpy as jnp
from jax import lax
from jax.experimental import pallas as pl
from jax.experimental.pallas import tpu as pltpu

F32 = jnp.float32
BF16 = jnp.bfloat16
I32 = jnp.int32

D_MODEL = 2048
PAGE_SIZE = 128
GDN_QK_HEADS = 16
GDN_V_HEADS = 32
GDN_DK = 128
GDN_DV = 128
CONV_W = 4
GDN_CHUNK = 64
ATT_HEADS = 16
ATT_KV_HEADS = 2
ATT_DH = 128
IDX_HEADS = 16
IDX_DH = 128
TOPK_MAX = 256
NORM_EPS = 1e-6
L2_EPS = 1e-6

A_QK = GDN_QK_HEADS * GDN_DK
A_V = GDN_V_HEADS * GDN_DV
A_CONV_CH = 2 * A_QK + A_V
B_Q = ATT_HEADS * ATT_DH
B_KV = ATT_KV_HEADS * ATT_DH
IDX_Q = IDX_HEADS * IDX_DH
SPLIT_SIZES = (A_CONV_CH, A_V, GDN_V_HEADS, GDN_V_HEADS, B_Q, B_KV, B_KV, B_Q, IDX_Q, IDX_DH, IDX_HEADS,
               D_MODEL, D_MODEL)
B_SEGS = (4, 7, 8, 11, 12)
TAIL_SEGS = (5, 6, 9, 2, 3, 10)
OFF_AQKV = 0
OFF_AZ = OFF_AQKV + A_CONV_CH
N_A = OFF_AZ + A_V
OFF_BQ = 0
OFF_BZ = OFF_BQ + B_Q
OFF_IQ = OFF_BZ + B_Q
OFF_GA = OFF_IQ + IDX_Q
OFF_GB = OFF_GA + D_MODEL
N_B = OFF_GB + D_MODEL
N_TAIL = 2 * B_KV + IDX_DH + 128
LANE_BETA = 0
LANE_DECAY = GDN_V_HEADS
LANE_IW = 2 * GDN_V_HEADS

VMEM_LIMIT = 56 * 1024 * 1024
NEG_INF = float("-inf")
INT_MIN = -2 ** 31


def _cparams(sem):
    return pltpu.CompilerParams(dimension_semantics=sem, vmem_limit_bytes=VMEM_LIMIT)


def _dot(a, b):
    return jnp.dot(a, b, preferred_element_type=F32)


def _dot_nt(a, b):
    return lax.dot_general(a, b, (((1,), (1,)), ((), ())), preferred_element_type=F32)


def _dot_tn(a, b):
    return lax.dot_general(a, b, (((0,), (0,)), ((), ())), preferred_element_type=F32)


def _split2(x):
    hi = x.astype(BF16)
    lo = (x - hi.astype(F32)).astype(BF16)
    return hi, lo


def _split3(x):
    x1 = x.astype(BF16)
    r = x - x1.astype(F32)
    x2 = r.astype(BF16)
    x3 = (r - x2.astype(F32)).astype(BF16)
    return x1, x2, x3


def _dot_3pass(a, b):
    ah, al = _split2(a)
    bh, bl = _split2(b)
    return _dot(ah, bh) + _dot(al, bh) + _dot(ah, bl)


def _silu(x):
    h = 0.5 * x
    return h + h * jnp.tanh(h)


def _mod_kernel(c_ref, w_ref, b_ref, o_ref):
    o_ref[...] = _dot_3pass(_silu(c_ref[...]), w_ref[...]) + b_ref[...]


def _mod_call(c_pad, w_ada, b_ada):
    rows, d = c_pad.shape
    n = w_ada.shape[1]
    tn = 768
    return pl.pallas_call(
        _mod_kernel,
        grid=(n // tn,),
        in_specs=[pl.BlockSpec((rows, d), lambda j: (0, 0)),
                  pl.BlockSpec((d, tn), lambda j: (0, j)),
                  pl.BlockSpec((1, tn), lambda j: (0, j))],
        out_specs=pl.BlockSpec((rows, tn), lambda j: (0, j)),
        out_shape=jax.ShapeDtypeStruct((rows, n), F32),
        compiler_params=_cparams(("arbitrary",)),
        name="adaln_mod",
    )(c_pad, w_ada, b_ada)


def _norm_kernel(x_ref, g_ref, scale_ref, shift_ref, o_ref):
    x = x_ref[...]
    y = x * lax.rsqrt(jnp.mean(x * x, axis=-1, keepdims=True) + NORM_EPS) * g_ref[...]
    o_ref[...] = (y * (1.0 + scale_ref[...]) + shift_ref[...]).astype(o_ref.dtype)


def _norm_call(x2d, g_row, scale3, shift3, tm, tiles_per_group):
    m, d = x2d.shape
    r = scale3.shape[1]
    mod_spec = pl.BlockSpec((None, r, d), lambda i: (i // tiles_per_group, 0, 0))
    return pl.pallas_call(
        _norm_kernel,
        grid=(m // tm,),
        in_specs=[pl.BlockSpec((tm, d), lambda i: (i, 0)),
                  pl.BlockSpec((1, d), lambda i: (0, 0)),
                  mod_spec, mod_spec],
        out_specs=pl.BlockSpec((tm, d), lambda i: (i, 0)),
        out_shape=jax.ShapeDtypeStruct((m, d), BF16),
        compiler_params=_cparams(("parallel",)),
        name="norm_modulate",
    )(x2d, g_row, scale3, shift3)


def _proj_kernel(h_ref, w_ref, o_ref):
    o_ref[...] = _dot(h_ref[...], w_ref[...]).astype(o_ref.dtype)


def _proj_call(h, w, tm, tn, out_dtype, name):
    m, k = h.shape
    n = w.shape[1]
    return pl.pallas_call(
        _proj_kernel,
        grid=(m // tm, n // tn),
        in_specs=[pl.BlockSpec((tm, k), lambda i, j: (i, 0)),
                  pl.BlockSpec((k, tn), lambda i, j: (0, j))],
        out_specs=pl.BlockSpec((tm, tn), lambda i, j: (i, j)),
        out_shape=jax.ShapeDtypeStruct((m, n), out_dtype),
        compiler_params=_cparams(("parallel", "arbitrary")),
        name=name,
    )(h, w)


def _proj_f32w_kernel(h_ref, w_ref, o_ref):
    o_ref[...] = _dot(h_ref[...], w_ref[...].astype(BF16)).astype(o_ref.dtype)


def _proj_f32w_call(h, w3, n_cols, tm, tn, out_dtype, name):
    m, k = h.shape
    return pl.pallas_call(
        _proj_f32w_kernel,
        grid=(m // tm, n_cols // tn),
        in_specs=[pl.BlockSpec((tm, k), lambda i, j: (i, 0)),
                  pl.BlockSpec((None, k, tn), lambda i, j: (0, 0, j))],
        out_specs=pl.BlockSpec((tm, tn), lambda i, j: (i, j)),
        out_shape=jax.ShapeDtypeStruct((m, n_cols), out_dtype),
        compiler_params=_cparams(("parallel", "arbitrary")),
        name=name,
    )(h, w3)


def _proj_tail_kernel(h_ref, w_ref, k_ref, v_ref, kidx_ref, small_ref):
    r = _dot(h_ref[...], w_ref[...])
    k_ref[...] = r[:, 0:B_KV]
    v_ref[...] = r[:, B_KV:2 * B_KV]
    kidx_ref[...] = r[:, 2 * B_KV:2 * B_KV + IDX_DH]
    small_ref[...] = r[:, 2 * B_KV + IDX_DH:]


def _proj_tail_call(h, w_tail, tm):
    m, k = h.shape
    widths = (B_KV, B_KV, IDX_DH, 128)
    return pl.pallas_call(
        _proj_tail_kernel,
        grid=(m // tm,),
        in_specs=[pl.BlockSpec((tm, k), lambda i: (i, 0)),
                  pl.BlockSpec((k, N_TAIL), lambda i: (0, 0))],
        out_specs=[pl.BlockSpec((tm, w), lambda i: (i, 0)) for w in widths],
        out_shape=[jax.ShapeDtypeStruct((m, w), F32) for w in widths],
        compiler_params=_cparams(("parallel",)),
        name="in_proj_tail",
    )(h, w_tail)


def _gates_kernel(sm_ref, alog_ref, dtb_ref, o_ref, *, chunk):
    x = sm_ref[...]
    tm = x.shape[0]
    lane = lax.broadcasted_iota(I32, x.shape, 1)
    beta = jax.nn.sigmoid(x)
    z = x + dtb_ref[...]
    softplus = jnp.maximum(z, 0.0) + jnp.log1p(jnp.exp(-jnp.abs(z)))
    g = -jnp.exp(alog_ref[...]) * softplus
    g = jnp.where((lane >= LANE_DECAY) & (lane < LANE_DECAY + GDN_V_HEADS), g, 0.0)
    ri = lax.broadcasted_iota(I32, (tm, tm), 0)
    ci = lax.broadcasted_iota(I32, (tm, tm), 1)
    shift = int(math.log2(chunk))
    tri = jnp.where((ri >= ci) & ((ri >> shift) == (ci >> shift)), 1.0, 0.0).astype(BF16)
    g1, g2, g3 = _split3(g)
    gc = _dot(tri, g1) + _dot(tri, g2) + _dot(tri, g3)
    o_ref[...] = jnp.where(lane < GDN_V_HEADS, beta, gc)


def _gates_call(small, alog_row, dtb_row, chunk):
    m = small.shape[0]
    tm = 256
    return pl.pallas_call(
        functools.partial(_gates_kernel, chunk=chunk),
        grid=(m // tm,),
        in_specs=[pl.BlockSpec((tm, 128), lambda i: (i, 0)),
                  pl.BlockSpec((1, 128), lambda i: (0, 0)),
                  pl.BlockSpec((1, 128), lambda i: (0, 0))],
        out_specs=pl.BlockSpec((tm, 128), lambda i: (i, 0)),
        out_shape=jax.ShapeDtypeStruct((m, 128), F32),
        compiler_params=_cparams(("parallel",)),
        name="gdn_gates",
    )(small, alog_row, dtb_row)


def _unrolled_loop(n, unroll, fn):
    unroll = min(unroll, n)
    assert n % unroll == 0

    def body(t, carry):
        for u in range(unroll):
            fn(t * unroll + u)
        return carry

    lax.fori_loop(0, n // unroll, body, 0)


def _gdn_kernel(*refs, L, C, HP, NSEG, has_state):
    ins = 13 if has_state else 12
    (q_ref, k_ref, v_ref, z_ref, bq_ref, bk_ref, bv_ref, wq_ref, wk_ref, wv_ref, gates_ref, gn_ref) = refs[:12]
    s0_ref = refs[12] if has_state else None
    (y_ref, sout_ref, xq_s, xk_s, xv_s, beta_s, g_s, st_s, a_s, p_s, rhs_s, kt_s, qkd_s, qd_s,
     m_s, n_s, qe_s, o_s) = refs[ins:]
    nh = 2 * HP
    nc = L // C
    hq0 = pl.program_id(1) * HP

    def first_segment():
        xq_s[0:8, :] = bq_ref[...]
        xk_s[0:8, :] = bk_ref[...]
        xv_s[0:8, :] = bv_ref[...]
        if has_state:
            st_s[...] = s0_ref[...]
        else:
            st_s[...] = jnp.zeros(st_s.shape, F32)

    def next_segment():
        xq_s[0:8, :] = xq_s[L:L + 8, :]
        xk_s[0:8, :] = xk_s[L:L + 8, :]
        xv_s[0:8, :] = xv_s[L:L + 8, :]

    if NSEG == 1:
        first_segment()
    else:
        pl.when(pl.program_id(2) == 0)(first_segment)
        pl.when(pl.program_id(2) > 0)(next_segment)
    xq_s[8:, :] = q_ref[...].astype(F32)
    xk_s[8:, :] = k_ref[...].astype(F32)
    xv_s[8:, :] = v_ref[...].astype(F32)

    gates = gates_ref[...]
    lane = lax.broadcasted_iota(I32, gates.shape, 1)
    for idx in range(nh):
        hv = 2 * hq0 + idx
        bcol = jnp.sum(jnp.where(lane == hv + LANE_BETA, gates, 0.0), axis=-1, keepdims=True)
        gcol = jnp.sum(jnp.where(lane == hv + LANE_DECAY, gates, 0.0), axis=-1, keepdims=True)
        beta_s[idx] = jnp.broadcast_to(bcol, (L, 128))
        g_s[idx] = jnp.broadcast_to(gcol, (L, 128))

    ii = lax.broadcasted_iota(I32, (C, C), 0)
    jj = lax.broadcasted_iota(I32, (C, C), 1)
    gn = gn_ref[...]

    def conv(x_s, r0, cols, w):
        win = x_s[pl.ds(r0, C + 8), cols]
        acc = (win[5:5 + C] * w[0:1] + win[6:6 + C] * w[1:2] + win[7:7 + C] * w[2:3] + win[8:8 + C] * w[3:4])
        return _silu(acc)

    def l2n(x):
        return x * lax.rsqrt(jnp.sum(x * x, axis=-1, keepdims=True) + L2_EPS)

    def lanes(i, width):
        if isinstance(i, int):
            return slice(i * width, (i + 1) * width)
        return pl.ds(pl.multiple_of(i * width, width), width)

    def pair_terms(j):
        c, p = (j, 0) if HP == 1 else (j // HP, j % HP)
        r0 = pl.multiple_of(c * C, C)
        rows = pl.ds(r0, C)
        lq, lv = lanes(p, 128), lanes(p, 256)
        q = l2n(conv(xq_s, r0, lq, wq_ref[:, lq])) * (GDN_DK ** -0.5)
        k = l2n(conv(xk_s, r0, lq, wk_ref[:, lq]))
        vc = conv(xv_s, r0, lv, wv_ref[:, lv])
        k16 = k.astype(BF16)
        kk = _dot_nt(k16, k16)
        qk = _dot_nt(q.astype(BF16), k16)
        for r in range(2):
            idx = 2 * p + r
            e = c * nh + idx
            gb = g_s[idx, rows, :]
            bb = beta_s[idx, rows, :]
            e_g = jnp.exp(gb)
            e_tail = jnp.exp(gb[C - 1:C, :] - gb)
            if C < 128:
                gpad = jnp.concatenate([gb, jnp.zeros((128 - C, 128), F32)], axis=0)
            else:
                gpad = gb
            dexp = gb[:, 0:C] - gpad.T[0:C, 0:C]
            dm = jnp.exp(jnp.where(ii >= jj, dexp, NEG_INF))
            a = jnp.where(ii > jj, kk * dm, 0.0) * bb[:, 0:C]
            a_s[e] = a.astype(a_s.dtype)
            p_s[e] = jnp.where(ii == jj, 1.0, 0.0) - a
            rhs_s[e] = jnp.concatenate([vc[:, r * 128:(r + 1) * 128] * bb, k * bb * e_g], axis=1).astype(rhs_s.dtype)
            kt_s[e] = (k * e_tail).astype(kt_s.dtype)
            qkd_s[e] = (qk * dm).astype(qkd_s.dtype)
            qd_s[e] = q * e_g

    def square(e):
        ak = a_s[e].astype(BF16)
        a_s[e] = _dot(ak, ak).astype(a_s.dtype)

    def inverse_stage(e, last):
        ak = a_s[e].astype(BF16)
        pk = p_s[e]
        p_s[e] = pk + _dot(pk.astype(BF16), ak)
        if not last:
            a_s[e] = _dot(ak, ak).astype(a_s.dtype)

    def solve(e):
        rhs_s[e] = _dot(p_s[e].astype(BF16), rhs_s[e].astype(BF16)).astype(rhs_s.dtype)

    def state_terms(e):
        sol16 = rhs_s[e].astype(BF16)
        mn = _dot_tn(kt_s[e].astype(BF16), sol16)
        qo = _dot(qkd_s[e].astype(BF16), sol16)
        n_s[e] = mn[:, 0:GDN_DV]
        m_s[e] = mn[:, GDN_DV:].astype(m_s.dtype)
        o_s[e] = qo[:, 0:GDN_DV]
        qe_s[e] = (qd_s[e] - qo[:, GDN_DV:]).astype(qe_s.dtype)

    def head_step(c, idx):
        e = c * nh + idx
        r0 = pl.multiple_of(c * C, C)
        rows = pl.ds(r0, C)
        s = st_s[idx]
        s16 = s.astype(BF16)
        o = _dot(qe_s[e].astype(BF16), s16) + o_s[e]
        e_last = jnp.exp(g_s[idx, pl.ds(r0 + C - 1, 1), :])
        st_s[idx] = s * e_last - _dot(m_s[e].astype(BF16), s16) + n_s[e]
        on = o * lax.rsqrt(jnp.mean(o * o, axis=-1, keepdims=True) + NORM_EPS) * gn
        zz = z_ref[rows, lanes(idx, 128)].astype(F32)
        y_ref[rows, lanes(idx, 128)] = (on * _silu(zz)).astype(y_ref.dtype)

    ne = nc * nh
    stages = int(math.log2(C)) - 1
    _unrolled_loop(nc * HP, 2, pair_terms)
    _unrolled_loop(ne, 32, square)
    for st in range(stages):
        _unrolled_loop(ne, 32, functools.partial(inverse_stage, last=(st == stages - 1)))
    _unrolled_loop(ne, 16, solve)
    _unrolled_loop(ne, 8, state_terms)

    def chunk_step(c, carry):
        if nh <= 4:
            for idx in range(nh):
                head_step(c, idx)
        else:
            _unrolled_loop(nh, 8, functools.partial(head_step, c))
        return carry

    lax.fori_loop(0, nc, chunk_step, 0)
    sout_ref[...] = st_s[...]


def _gdn_call(big, convbuf8, conv_w, gates, gn_row, s0, *, B, L, C, HP, NSEG, y_dtype):
    has_state = s0 is not None
    ls = L // NSEG
    ne = (ls // C) * 2 * HP
    half = BF16 if C % 16 == 0 else F32
    nq = GDN_QK_HEADS // HP
    wq, wv = HP * 128, HP * 256
    kq0 = A_QK // wq
    v0 = (2 * A_QK) // wv
    z0 = OFF_AZ // wv
    in_specs = [
        pl.BlockSpec((ls, wq), lambda b, h, s: (b * NSEG + s, h)),
        pl.BlockSpec((ls, wq), lambda b, h, s: (b * NSEG + s, kq0 + h)),
        pl.BlockSpec((ls, wv), lambda b, h, s: (b * NSEG + s, v0 + h)),
        pl.BlockSpec((ls, wv), lambda b, h, s: (b * NSEG + s, z0 + h)),
        pl.BlockSpec((None, 8, wq), lambda b, h, s: (b, 0, h)),
        pl.BlockSpec((None, 8, wq), lambda b, h, s: (b, 0, kq0 + h)),
        pl.BlockSpec((None, 8, wv), lambda b, h, s: (b, 0, v0 + h)),
        pl.BlockSpec((CONV_W, wq), lambda b, h, s: (0, h)),
        pl.BlockSpec((CONV_W, wq), lambda b, h, s: (0, kq0 + h)),
        pl.BlockSpec((CONV_W, wv), lambda b, h, s: (0, v0 + h)),
        pl.BlockSpec((ls, 128), lambda b, h, s: (b * NSEG + s, 0)),
        pl.BlockSpec((1, 128), lambda b, h, s: (0, 0)),
    ]
    args = [big, big, big, big, convbuf8, convbuf8, convbuf8, conv_w, conv_w, conv_w, gates, gn_row]
    if has_state:
        in_specs.append(pl.BlockSpec((None, 2 * HP, GDN_DK, GDN_DV), lambda b, h, s: (b, h, 0, 0)))
        args.append(s0)
    return pl.pallas_call(
        functools.partial(_gdn_kernel, L=ls, C=C, HP=HP, NSEG=NSEG, has_state=has_state),
        grid=(B, nq, NSEG),
        in_specs=in_specs,
        out_specs=[pl.BlockSpec((ls, wv), lambda b, h, s: (b * NSEG + s, h)),
                   pl.BlockSpec((None, 2 * HP, GDN_DK, GDN_DV), lambda b, h, s: (b, h, 0, 0))],
        out_shape=[jax.ShapeDtypeStruct((B * L, A_V), y_dtype),
                   jax.ShapeDtypeStruct((B, GDN_V_HEADS, GDN_DK, GDN_DV), F32)],
        scratch_shapes=[pltpu.VMEM((ls + 8, wq), F32), pltpu.VMEM((ls + 8, wq), F32), pltpu.VMEM((ls + 8, wv), F32),
                        pltpu.VMEM((2 * HP, ls, 128), F32), pltpu.VMEM((2 * HP, ls, 128), F32),
                        pltpu.VMEM((2 * HP, GDN_DK, GDN_DV), F32),
                        pltpu.VMEM((ne, C, C), half), pltpu.VMEM((ne, C, C), F32),
                        pltpu.VMEM((ne, C, GDN_DV + GDN_DK), half), pltpu.VMEM((ne, C, GDN_DK), half),
                        pltpu.VMEM((ne, C, C), half), pltpu.VMEM((ne, C, GDN_DK), F32),
                        pltpu.VMEM((ne, GDN_DK, GDN_DV), BF16), pltpu.VMEM((ne, GDN_DK, GDN_DV), F32),
                        pltpu.VMEM((ne, C, GDN_DK), half), pltpu.VMEM((ne, C, GDN_DV), F32)],
        compiler_params=_cparams(("parallel", "arbitrary", "arbitrary")),
        name="gdn_prompt" if not has_state else "gdn_sample",
    )(*args)


KEY_NEG_INF = -2 ** 31 + 0x7FFFFF


def _key_to_float(key):
    bits = key ^ ((key >> 31) & jnp.int32(0x7FFFFFFF))
    return pltpu.bitcast(bits, F32)


def _selection_bias(score_ref, bias_ref, qpos0, rows, width, topk):
    def step(it, t):
        cand = t + jnp.left_shift(jnp.int32(1), 31 - it)
        ge = score_ref[:, 0:width] >= _key_to_float(cand)
        cnt = jnp.sum(jnp.where(ge, 1.0, 0.0), axis=-1, keepdims=True)
        return jnp.where((cnt >= float(topk)) | (cand <= KEY_NEG_INF), cand, t)

    thr = _key_to_float(lax.fori_loop(0, 32, step, jnp.full((rows, 1), INT_MIN, I32)))
    score = score_ref[:, 0:width]
    causal = (lax.broadcasted_iota(I32, (rows, width), 1)
              <= qpos0 + lax.broadcasted_iota(I32, (rows, width), 0))
    sel = (score >= thr) & causal
    bias_ref[:, 0:width] = jnp.where(sel, 0.0, NEG_INF)
    n_sel = jnp.sum(jnp.where(sel, 1.0, 0.0), axis=-1, keepdims=True)
    has_ties = jnp.max(n_sel) > float(topk)

    @pl.when(has_ties)
    def _():
        need = float(topk) - jnp.sum(jnp.where(score > thr, 1.0, 0.0), axis=-1, keepdims=True)
        eq16 = jnp.where(score == thr, 1.0, 0.0).astype(BF16)

        def tile(j, carry):
            c0 = pl.multiple_of(j * 128, 128)
            ri = lax.broadcasted_iota(I32, (width, 128), 0)
            ci = lax.broadcasted_iota(I32, (width, 128), 1) + c0
            upper = jnp.where(ri <= ci, 1.0, 0.0).astype(BF16)
            prefix = _dot(eq16, upper)
            s_t = score_ref[:, pl.ds(c0, 128)]
            keep = (s_t > thr) | ((s_t == thr) & (prefix <= need))
            old = bias_ref[:, pl.ds(c0, 128)]
            bias_ref[:, pl.ds(c0, 128)] = jnp.where(keep, old, NEG_INF)
            return carry

        lax.fori_loop(0, width // 128, tile, 0)


SOFTMAX_LOG2_SCALE = (ATT_DH ** -0.5) * math.log2(math.e)


def _dsa_prompt_kernel(qi_ref, wi_ref, kidx_ref, q_ref, k_ref, v_ref, z_ref, o_ref,
                       score_s, bias_s, kidx16_s, k16_s, vext_s, m_s, acc_s, *, QB, KT, L, topk):
    i = pl.program_id(1)
    q0 = i * QB
    nt = (q0 + QB - 1) // KT + 1
    groups = ATT_HEADS // ATT_KV_HEADS

    @pl.when(i == 0)
    def _():
        kidx16_s[...] = kidx_ref[...].astype(BF16)
        k16_s[...] = k_ref[...].astype(BF16)
        for n in range(ATT_KV_HEADS):
            vext_s[n, :, 0:ATT_DH] = v_ref[:, n * ATT_DH:(n + 1) * ATT_DH].astype(BF16)
            vext_s[n, :, ATT_DH:] = jnp.ones((L, ATT_DH), BF16)

    wi = wi_ref[...] * ((IDX_DH ** -0.5) * (IDX_HEADS ** -0.5))
    rowpos = q0 + lax.broadcasted_iota(I32, (QB, KT), 0)
    col = lax.broadcasted_iota(I32, (QB, KT), 1)

    def tile_rows(t):
        return pl.ds(pl.multiple_of(t * KT, KT), KT)

    def index_tile(t, carry):
        kid = kidx16_s[tile_rows(t), :]
        acc = jnp.zeros((QB, KT), F32)
        for h in range(IDX_HEADS):
            lg = _dot_nt(qi_ref[:, h * IDX_DH:(h + 1) * IDX_DH], kid)
            acc = acc + jnp.maximum(lg, 0.0) * wi[:, LANE_IW + h:LANE_IW + h + 1]
        score_s[:, tile_rows(t)] = jnp.where(col + t * KT <= rowpos, acc, NEG_INF)
        return carry

    lax.fori_loop(0, nt, index_tile, 0)
    for w in range(1, L // KT + 1):
        @pl.when(nt == w)
        def _():
            _selection_bias(score_s, bias_s, q0, QB, w * KT, topk)

    m_s[...] = jnp.full(m_s.shape, NEG_INF, F32)
    acc_s[...] = jnp.zeros(acc_s.shape, F32)

    def attend_tile(t, carry):
        bias = bias_s[:, tile_rows(t)]
        for n in range(ATT_KV_HEADS):
            kn = k16_s[tile_rows(t), n * ATT_DH:(n + 1) * ATT_DH]
            vn = vext_s[n, tile_rows(t), :]
            for g in range(groups):
                h = n * groups + g
                sb = _dot_nt(q_ref[:, h * ATT_DH:(h + 1) * ATT_DH], kn) + bias
                m_old = m_s[h]
                m_new = jnp.maximum(m_old, jnp.max(sb, axis=-1, keepdims=True))
                m_ref = jnp.where(m_new == NEG_INF, 0.0, m_new)
                p = jnp.exp2((sb - jnp.concatenate([m_ref] * (KT // 128), axis=1)) * SOFTMAX_LOG2_SCALE)
                alpha = jnp.exp2((m_old - m_ref) * SOFTMAX_LOG2_SCALE)
                m_s[h] = m_new
                acc_s[h] = acc_s[h] * jnp.concatenate([alpha, alpha], axis=1) + _dot(p.astype(BF16), vn)
        return carry

    lax.fori_loop(0, nt, attend_tile, 0)
    for h in range(ATT_HEADS):
        cols = slice(h * ATT_DH, (h + 1) * ATT_DH)
        zz = z_ref[:, cols].astype(F32)
        acc = acc_s[h]
        o_ref[:, cols] = (acc[:, 0:ATT_DH] / acc[:, ATT_DH:] * _silu(zz)).astype(o_ref.dtype)


def _dsa_prompt_call(big, small, kidx, k, v, *, B, L, QB):
    topk = min(TOPK_MAX, L // 4)
    nqb = L // QB
    iq0 = OFF_IQ // B_Q
    bq0 = OFF_BQ // B_Q
    bz0 = OFF_BZ // B_Q
    return pl.pallas_call(
        functools.partial(_dsa_prompt_kernel, QB=QB, KT=256, L=L, topk=topk),
        grid=(B, nqb),
        in_specs=[pl.BlockSpec((QB, IDX_Q), lambda b, i: (b * nqb + i, iq0)),
                  pl.BlockSpec((QB, 128), lambda b, i: (b * nqb + i, 0)),
                  pl.BlockSpec((L, IDX_DH), lambda b, i: (b, 0)),
                  pl.BlockSpec((QB, B_Q), lambda b, i: (b * nqb + i, bq0)),
                  pl.BlockSpec((L, B_KV), lambda b, i: (b, 0)),
                  pl.BlockSpec((L, B_KV), lambda b, i: (b, 0)),
                  pl.BlockSpec((QB, B_Q), lambda b, i: (b * nqb + i, bz0))],
        out_specs=pl.BlockSpec((QB, B_Q), lambda b, i: (b * nqb + i, 0)),
        out_shape=jax.ShapeDtypeStruct((B * L, B_Q), BF16),
        scratch_shapes=[pltpu.VMEM((QB, L), F32), pltpu.VMEM((QB, L), F32),
                        pltpu.VMEM((L, IDX_DH), BF16), pltpu.VMEM((L, B_KV), BF16),
                        pltpu.VMEM((ATT_KV_HEADS, L, 2 * ATT_DH), BF16),
                        pltpu.VMEM((ATT_HEADS, QB, 128), F32), pltpu.VMEM((ATT_HEADS, QB, 2 * ATT_DH), F32)],
        compiler_params=_cparams(("parallel", "arbitrary")),
        name="dsa_prompt",
    )(big, small, kidx, big, k, v, big)


def _dsa_sample_kernel(pt_ref, qi_ref, wi_ref, kin_ref, q_ref, kn_ref, vn_ref, z_ref, ck_hbm, cv_hbm, ci_hbm,
                       o_ref, kbuf, vbuf, ibuf, sems, score_s, bias_s, *, NP, T, topk):
    b = pl.program_id(0)
    nb = pl.num_programs(0)
    slot = b % 2
    past = NP * PAGE_SIZE
    width = past + 128

    kvp = ATT_KV_HEADS * PAGE_SIZE

    def page_copies(bb, sl, p):
        page = pt_ref[bb * NP + p]
        rows = pl.ds(pl.multiple_of(p * PAGE_SIZE, PAGE_SIZE), PAGE_SIZE)
        src2 = pl.ds(pl.multiple_of(page * kvp, kvp), kvp)
        dst2 = pl.ds(pl.multiple_of(p * kvp, kvp), kvp)
        return (pltpu.make_async_copy(ck_hbm.at[src2], kbuf.at[sl, dst2], sems.at[0, sl]),
                pltpu.make_async_copy(cv_hbm.at[src2], vbuf.at[sl, dst2], sems.at[1, sl]),
                pltpu.make_async_copy(ci_hbm.at[page], ibuf.at[sl, rows], sems.at[2, sl]))

    def start_fetch(bb, sl):
        def body(p, carry):
            for cp in page_copies(bb, sl, p):
                cp.start()
            return carry
        lax.fori_loop(0, NP, body, 0)

    def wait_fetch(bb, sl):
        def body(p, carry):
            for cp in page_copies(bb, sl, p):
                cp.wait()
            return carry
        lax.fori_loop(0, NP, body, 0)

    @pl.when(b == 0)
    def _():
        for sl in range(2):
            kbuf[sl, ATT_KV_HEADS * past:, :] = jnp.zeros((ATT_KV_HEADS * 128, ATT_DH), F32)
            vbuf[sl, ATT_KV_HEADS * past:, :] = jnp.zeros((ATT_KV_HEADS * 128, ATT_DH), F32)
            ibuf[sl, past:, :] = jnp.zeros((128, IDX_DH), F32)
        start_fetch(0, 0)

    @pl.when(b + 1 < nb)
    def _():
        start_fetch(b + 1, 1 - slot)

    for n in range(ATT_KV_HEADS):
        new_rows = pl.ds(ATT_KV_HEADS * past + n, T, stride=ATT_KV_HEADS)
        kbuf[slot, new_rows, :] = kn_ref[:, n * ATT_DH:(n + 1) * ATT_DH]
        vbuf[slot, new_rows, :] = vn_ref[:, n * ATT_DH:(n + 1) * ATT_DH]
    ibuf[slot, past:past + T, :] = kin_ref[...]
    wait_fetch(b, slot)

    qi = qi_ref[...]
    qis = jnp.concatenate([qi[:, h * IDX_DH:(h + 1) * IDX_DH] for h in range(IDX_HEADS)], axis=0).astype(BF16)
    wi = wi_ref[...] * ((IDX_DH ** -0.5) * (IDX_HEADS ** -0.5))
    lg = _dot_nt(qis, ibuf[slot].astype(BF16))
    score = jnp.zeros((T, width), F32)
    for h in range(IDX_HEADS):
        score = score + jnp.maximum(lg[h * T:(h + 1) * T, :], 0.0) * wi[:, LANE_IW + h:LANE_IW + h + 1]
    qpos = past + lax.broadcasted_iota(I32, (T, width), 0)
    kpos = lax.broadcasted_iota(I32, (T, width), 1)
    causal = kpos <= qpos
    score_s[...] = jnp.where(causal, score, NEG_INF)
    _selection_bias(score_s, bias_s, past, T, width, topk)

    groups = ATT_HEADS // ATT_KV_HEADS
    q = q_ref[...]
    bias = jnp.concatenate([bias_s[...]] * groups, axis=0)
    for n in range(ATT_KV_HEADS):
        head_rows = pl.ds(n, width, stride=ATT_KV_HEADS)
        kn = kbuf[slot, head_rows, :].astype(BF16)
        vn = vbuf[slot, head_rows, :].astype(BF16)
        qs = jnp.concatenate([q[:, (n * groups + g) * ATT_DH:(n * groups + g + 1) * ATT_DH] for g in range(groups)],
                             axis=0).astype(BF16)
        s = _dot_nt(qs, kn) * (ATT_DH ** -0.5) + bias
        m = jnp.max(s, axis=-1, keepdims=True)
        p = jnp.exp(s - m)
        denom = jnp.sum(p, axis=-1, keepdims=True)
        o = _dot(p.astype(BF16), vn) / denom
        for g in range(groups):
            h = n * groups + g
            cols = slice(h * ATT_DH, (h + 1) * ATT_DH)
            zz = z_ref[:, cols]
            o_ref[:, cols] = o[g * T:(g + 1) * T, :] * _silu(zz)


def _dsa_sample_call(big, small, kidx, k, v, cache_k, cache_v, cache_kidx, page_table, *, B, T):
    npages = page_table.shape[1]
    past = npages * PAGE_SIZE
    topk = min(TOPK_MAX, (past + T) // 4)
    width = past + 128
    iq0 = OFF_IQ // B_Q
    bq0 = OFF_BQ // B_Q
    bz0 = OFF_BZ // B_Q
    grid_spec = pltpu.PrefetchScalarGridSpec(
        num_scalar_prefetch=1,
        grid=(B,),
        in_specs=[pl.BlockSpec((T, IDX_Q), lambda b, pt: (b, iq0)),
                  pl.BlockSpec((T, 128), lambda b, pt: (b, 0)),
                  pl.BlockSpec((T, IDX_DH), lambda b, pt: (b, 0)),
                  pl.BlockSpec((T, B_Q), lambda b, pt: (b, bq0)),
                  pl.BlockSpec((T, B_KV), lambda b, pt: (b, 0)),
                  pl.BlockSpec((T, B_KV), lambda b, pt: (b, 0)),
                  pl.BlockSpec((T, B_Q), lambda b, pt: (b, bz0)),
                  pl.BlockSpec(memory_space=pl.ANY),
                  pl.BlockSpec(memory_space=pl.ANY),
                  pl.BlockSpec(memory_space=pl.ANY)],
        out_specs=pl.BlockSpec((T, B_Q), lambda b, pt: (b, 0)),
        scratch_shapes=[pltpu.VMEM((2, ATT_KV_HEADS * width, ATT_DH), F32),
                        pltpu.VMEM((2, ATT_KV_HEADS * width, ATT_DH), F32),
                        pltpu.VMEM((2, width, IDX_DH), F32), pltpu.SemaphoreType.DMA((3, 2)),
                        pltpu.VMEM((T, width), F32), pltpu.VMEM((T, width), F32)])
    return pl.pallas_call(
        functools.partial(_dsa_sample_kernel, NP=npages, T=T, topk=topk),
        grid_spec=grid_spec,
        out_shape=jax.ShapeDtypeStruct((B * T, B_Q), F32),
        compiler_params=_cparams(("arbitrary",)),
        name="dsa_sample",
    )(page_table.reshape(-1), big, small, kidx, big, k, v, big, cache_k, cache_v, cache_kidx)


def _merge_kernel(ya_ref, yb_ref, ga_ref, gb_ref, wpa_ref, wpb_ref, o_ref):
    pa = _dot(ya_ref[...].astype(BF16), wpa_ref[...])
    pb = _dot(yb_ref[...].astype(BF16), wpb_ref[...])
    o = jax.nn.sigmoid(ga_ref[...].astype(F32)) * pa + jax.nn.sigmoid(gb_ref[...].astype(F32)) * pb
    o_ref[...] = o.astype(o_ref.dtype)


def _merge_call(ya, yb, big, wpa, wpb, tm, tn):
    m = ya.shape[0]
    n = wpa.shape[1]
    ga0 = OFF_GA // tn
    gb0 = OFF_GB // tn
    return pl.pallas_call(
        _merge_kernel,
        grid=(m // tm, n // tn),
        in_specs=[pl.BlockSpec((tm, A_V), lambda i, j: (i, 0)),
                  pl.BlockSpec((tm, B_Q), lambda i, j: (i, 0)),
                  pl.BlockSpec((tm, tn), lambda i, j: (i, ga0 + j)),
                  pl.BlockSpec((tm, tn), lambda i, j: (i, gb0 + j)),
                  pl.BlockSpec((A_V, tn), lambda i, j: (0, j)),
                  pl.BlockSpec((B_Q, tn), lambda i, j: (0, j))],
        out_specs=pl.BlockSpec((tm, tn), lambda i, j: (i, j)),
        out_shape=jax.ShapeDtypeStruct((m, n), BF16),
        compiler_params=_cparams(("parallel", "arbitrary")),
        name="merge_proj",
    )(ya, yb, big, big, wpa, wpb)


def _final_kernel(m_ref, w_ref, x_ref, gate_ref, g_ref, o_ref):
    t = _dot(m_ref[...], w_ref[...])
    y = t * lax.rsqrt(jnp.mean(t * t, axis=-1, keepdims=True) + NORM_EPS) * g_ref[...]
    o_ref[...] = x_ref[...] + gate_ref[...] * y


def _final_call(merged, w_out, x2d, gate3, g_row, tm, tiles_per_group):
    m, d = x2d.shape
    r = gate3.shape[1]
    return pl.pallas_call(
        _final_kernel,
        grid=(m // tm,),
        in_specs=[pl.BlockSpec((tm, d), lambda i: (i, 0)),
                  pl.BlockSpec((d, d), lambda i: (0, 0)),
                  pl.BlockSpec((tm, d), lambda i: (i, 0)),
                  pl.BlockSpec((None, r, d), lambda i: (i // tiles_per_group, 0, 0)),
                  pl.BlockSpec((1, d), lambda i: (0, 0))],
        out_specs=pl.BlockSpec((tm, d), lambda i: (i, 0)),
        out_shape=jax.ShapeDtypeStruct((m, d), F32),
        compiler_params=_cparams(("parallel",)),
        name="out_proj",
    )(merged, w_out, x2d, gate3, g_row)


SEG_OFFS = tuple(sum(SPLIT_SIZES[:i]) for i in range(len(SPLIT_SIZES) + 1))


def _lane_window(w_ref, start, size):
    a0 = (start // 128) * 128
    a1 = min(-(-(start + size) // 128) * 128, w_ref.shape[1])
    return w_ref[:, a0:a1][:, start - a0:start - a0 + size]


def _wprep_kernel(w_ref, wb_ref, wt_ref):
    col = 0
    for i in B_SEGS:
        wb_ref[:, col:col + SPLIT_SIZES[i]] = _lane_window(w_ref, SEG_OFFS[i], SPLIT_SIZES[i]).astype(BF16)
        col += SPLIT_SIZES[i]
    col = 0
    for i in TAIL_SEGS[:3]:
        wt_ref[:, col:col + SPLIT_SIZES[i]] = _lane_window(w_ref, SEG_OFFS[i], SPLIT_SIZES[i]).astype(BF16)
        col += SPLIT_SIZES[i]
    beta0, iw0 = SEG_OFFS[2], SEG_OFFS[10]
    assert beta0 % 128 == LANE_BETA and iw0 % 128 == LANE_IW and SEG_OFFS[3] - beta0 == LANE_DECAY
    t_gate = w_ref[:, beta0:beta0 + 128]
    t_iw = w_ref[:, iw0 - LANE_IW:iw0 - LANE_IW + 128]
    lane = lax.broadcasted_iota(I32, t_gate.shape, 1)
    small = jnp.where(lane < LANE_IW, t_gate, jnp.where(lane < LANE_IW + IDX_HEADS, t_iw, 0.0))
    wt_ref[:, col:] = small.astype(BF16)


def _relayout_in_proj(w3):
    _, d, n = w3.shape
    tr = 64
    return pl.pallas_call(
        _wprep_kernel,
        grid=(d // tr,),
        in_specs=[pl.BlockSpec((None, tr, n), lambda i: (0, i, 0))],
        out_specs=[pl.BlockSpec((tr, N_B), lambda i: (i, 0)), pl.BlockSpec((tr, N_TAIL), lambda i: (i, 0))],
        out_shape=[jax.ShapeDtypeStruct((d, N_B), BF16), jax.ShapeDtypeStruct((d, N_TAIL), BF16)],
        compiler_params=_cparams(("parallel",)),
        name="w_in_regroup",
    )(w3)


def _cast_kernel(x_ref, o_ref):
    o_ref[...] = x_ref[...].astype(o_ref.dtype)


def _cast_bf16_call(w3):
    _, r, n = w3.shape
    tr = 256
    return pl.pallas_call(
        _cast_kernel,
        grid=(r // tr,),
        in_specs=[pl.BlockSpec((None, tr, n), lambda i: (0, i, 0))],
        out_specs=pl.BlockSpec((tr, n), lambda i: (i, 0)),
        out_shape=jax.ShapeDtypeStruct((r, n), BF16),
        compiler_params=_cparams(("parallel",)),
        name="w_cast",
    )(w3)


def _lane_row(vec, lane0):
    return jnp.zeros((1, 128), F32).at[0, lane0:lane0 + vec.shape[0]].set(vec.astype(F32))


def _layer(x, mod, conv_hist, s0, attend, wts, *, gdn_chunk, gdn_hp, gdn_nseg, tm_proj, tn_proj, act_dtype):
    (pre_g, w_in3, w_b, w_tail, conv_w, alog_row, dtb_row, gn_row, wpa, wpb, wout, post_g) = wts
    B, L, D = x.shape
    M = B * L
    x2d = x.reshape(M, D)
    shift, scale, gate = mod[:, 0:D], mod[:, D:2 * D], mod[:, 2 * D:3 * D]
    if L >= 256:
        tm_n = 256
        per = L // tm_n
        scale3, shift3, gate3 = scale[:, None, :], shift[:, None, :], gate[:, None, :]
    else:
        tm_n = M
        per = 1
        scale3, shift3, gate3 = (jnp.repeat(t, L, axis=0)[None] for t in (scale, shift, gate))
    h = _norm_call(x2d, pre_g, scale3, shift3, tm_n, per)
    tm = min(tm_proj, M)
    big_a = _proj_f32w_call(h, w_in3, N_A, tm, tn_proj, act_dtype, "in_proj_a")
    big_b = _proj_call(h, w_b, min(tm, 1024), tn_proj, act_dtype, "in_proj_b")
    k, v, kidx, small = _proj_tail_call(h, w_tail, min(tm, 1024))

    gates = _gates_call(small, alog_row, dtb_row, gdn_chunk)
    if conv_hist is None:
        convbuf8 = jnp.zeros((B, 8, A_CONV_CH), F32)
    else:
        convbuf8 = jnp.concatenate([jnp.zeros((B, 5, A_CONV_CH), F32), conv_hist.astype(F32)], axis=1)
    y_a, s_new = _gdn_call(big_a, convbuf8, conv_w, gates, gn_row, s0, B=B, L=L, C=gdn_chunk, HP=gdn_hp,
                           NSEG=gdn_nseg, y_dtype=act_dtype)
    y_b = attend(big_b, small, kidx, k, v)
    tm_m = min(1024, M)
    merged = _merge_call(y_a, y_b, big_b, wpa, wpb, tm_m, 512)
    tm_f = min(512, M)
    per_f = (L // tm_f) if L >= 256 else 1
    y = _final_call(merged, wout, x2d, gate3, post_g, tm_f, per_f)
    return y.reshape(B, L, D), big_a, h, k, v, kidx, s_new


def kernel(x_prompt, x_sample, c_prompt, c_sample, cache_k, cache_v, cache_kidx, state_gdn, state_conv,
           page_table, w_ada, b_ada, pre_norm_g, w_in, conv_w, a_log, dt_bias, gdn_norm_g, w_pa, w_pb, w_out,
           post_norm_g):
    depth = w_in.shape[0]
    assert depth == 1, "single-layer trunk"
    BP, LP, D = x_prompt.shape
    BS, LS, _ = x_sample.shape
    n_pool = cache_k.shape[1]

    w_b, w_tail = _relayout_in_proj(w_in)
    wts = (pre_norm_g[0][None, :], w_in, w_b, w_tail, conv_w[0],
           _lane_row(a_log[0], LANE_DECAY), _lane_row(dt_bias[0], LANE_DECAY), gdn_norm_g[0][None, :],
           _cast_bf16_call(w_pa), _cast_bf16_call(w_pb), _cast_bf16_call(w_out), post_norm_g[0][None, :])

    rows = BP + BS
    rows_pad = -(-rows // 16) * 16
    c_all = jnp.concatenate([c_prompt, c_sample, jnp.zeros((rows_pad - rows, D), F32)], axis=0)
    mod = _mod_call(c_all, w_ada[0], b_ada[0][None, :])
    mod_p, mod_s = mod[0:BP], mod[BP:BP + BS]

    attend_p = functools.partial(_dsa_prompt_call, B=BP, L=LP, QB=128)
    y_p, big_p, h_p, k_p, v_p, kidx_p, s_p = _layer(
        x_prompt, mod_p, None, None, attend_p, wts,
        gdn_chunk=min(GDN_CHUNK, LP), gdn_hp=2, gdn_nseg=2, tm_proj=2048, tn_proj=512, act_dtype=BF16)
    h_last = h_p.reshape(BP, LP, D)[:, LP - 8:, :].reshape(BP * 8, D)
    aqkv_last = _proj_f32w_call(h_last, w_in, A_CONV_CH, BP * 8, 512, F32, "in_proj_conv_tail")
    conv_p = aqkv_last.reshape(BP, 8, A_CONV_CH)[:, 8 - (CONV_W - 1):, :]

    ck = cache_k[0].reshape(n_pool * PAGE_SIZE * ATT_KV_HEADS, ATT_DH)
    cv = cache_v[0].reshape(n_pool * PAGE_SIZE * ATT_KV_HEADS, ATT_DH)
    ci = cache_kidx[0]
    attend_s = functools.partial(_dsa_sample_call, cache_k=ck, cache_v=cv, cache_kidx=ci, page_table=page_table,
                                 B=BS, T=LS)
    y_s, big_s, _, k_s, v_s, kidx_s, s_s = _layer(
        x_sample, mod_s, state_conv[0], state_gdn[0], attend_s, wts,
        gdn_chunk=min(GDN_CHUNK, LS), gdn_hp=16, gdn_nseg=1, tm_proj=256, tn_proj=512, act_dtype=F32)
    conv_s = big_s[:, 0:A_CONV_CH].reshape(BS, LS, A_CONV_CH)[:, LS - (CONV_W - 1):, :]

    kv_shape_p = (1, BP, LP, ATT_KV_HEADS, ATT_DH)
    kv_shape_s = (1, BS, LS, ATT_KV_HEADS, ATT_DH)
    return (y_p, y_s,
            k_p.reshape(kv_shape_p), v_p.reshape(kv_shape_p), kidx_p.reshape(1, BP, LP, IDX_DH),
            s_p[None], conv_p[None],
            k_s.reshape(kv_shape_s), v_s.reshape(kv_shape_s), kidx_s.reshape(1, BS, LS, IDX_DH),
            s_s[None], conv_s[None])
```

```python
import functools
import math

import jax
import jax.numpy as jnp
from jax import lax
from jax.experimental import pallas as pl
from jax.experimental.pallas import tpu as pltpu

F32 = jnp.float32
BF16 = jnp.bfloat16
I32 = jnp.int32

D_MODEL = 2048
PAGE_SIZE = 128
GDN_QK_HEADS = 16
GDN_V_HEADS = 32
GDN_DK = 128
GDN_DV = 128
CONV_W = 4
GDN_CHUNK = 64
ATT_HEADS = 16
ATT_KV_HEADS = 2
ATT_DH = 128
IDX_HEADS = 16
IDX_DH = 128
TOPK_MAX = 256
NORM_EPS = 1e-6
L2_EPS = 1e-6

A_QK = GDN_QK_HEADS * GDN_DK
A_V = GDN_V_HEADS * GDN_DV
A_CONV_CH = 2 * A_QK + A_V
B_Q = ATT_HEADS * ATT_DH
B_KV = ATT_KV_HEADS * ATT_DH
IDX_Q = IDX_HEADS * IDX_DH
SPLIT_SIZES = (A_CONV_CH, A_V, GDN_V_HEADS, GDN_V_HEADS, B_Q, B_KV, B_KV, B_Q, IDX_Q, IDX_DH, IDX_HEADS,
               D_MODEL, D_MODEL)
B_SEGS = (4, 7, 8, 11, 12)
OFF_AQKV = 0
OFF_AZ = OFF_AQKV + A_CONV_CH
N_A = OFF_AZ + A_V
OFF_BQ = 0
OFF_BZ = OFF_BQ + B_Q
OFF_IQ = OFF_BZ + B_Q
OFF_GA = OFF_IQ + IDX_Q
OFF_GB = OFF_GA + D_MODEL
N_B = OFF_GB + D_MODEL
LANE_BETA = 0
LANE_DECAY = GDN_V_HEADS
LANE_IW = 2 * GDN_V_HEADS

VMEM_LIMIT = 56 * 1024 * 1024
NEG_INF = float("-inf")
INT_MIN = -2 ** 31


def _cparams(sem):
    return pltpu.CompilerParams(dimension_semantics=sem, vmem_limit_bytes=VMEM_LIMIT)


def _dot(a, b):
    return jnp.dot(a, b, preferred_element_type=F32)


def _dot_nt(a, b):
    return lax.dot_general(a, b, (((1,), (1,)), ((), ())), preferred_element_type=F32)


def _dot_tn(a, b):
    return lax.dot_general(a, b, (((0,), (0,)), ((), ())), preferred_element_type=F32)


def _split2(x):
    hi = x.astype(BF16)
    lo = (x - hi.astype(F32)).astype(BF16)
    return hi, lo


def _split3(x):
    x1 = x.astype(BF16)
    r = x - x1.astype(F32)
    x2 = r.astype(BF16)
    x3 = (r - x2.astype(F32)).astype(BF16)
    return x1, x2, x3


def _dot_3pass(a, b):
    ah, al = _split2(a)
    bh, bl = _split2(b)
    return _dot(ah, bh) + _dot(al, bh) + _dot(ah, bl)


def _silu(x):
    h = 0.5 * x
    return h + h * jnp.tanh(h)


def _mod_kernel(c_ref, w_ref, b_ref, o_ref):
    o_ref[...] = _dot_3pass(_silu(c_ref[...]), w_ref[...]) + b_ref[...]


def _mod_call(c_pad, w_ada, b_ada):
    rows, d = c_pad.shape
    n = w_ada.shape[1]
    tn = 768
    return pl.pallas_call(
        _mod_kernel,
        grid=(n // tn,),
        in_specs=[pl.BlockSpec((rows, d), lambda j: (0, 0)),
                  pl.BlockSpec((d, tn), lambda j: (0, j)),
                  pl.BlockSpec((1, tn), lambda j: (0, j))],
        out_specs=pl.BlockSpec((rows, tn), lambda j: (0, j)),
        out_shape=jax.ShapeDtypeStruct((rows, n), F32),
        compiler_params=_cparams(("arbitrary",)),
        name="adaln_mod",
    )(c_pad, w_ada, b_ada)


def _norm_kernel(x_ref, g_ref, scale_ref, shift_ref, o_ref):
    x = x_ref[...]
    y = x * lax.rsqrt(jnp.mean(x * x, axis=-1, keepdims=True) + NORM_EPS) * g_ref[...]
    o_ref[...] = (y * (1.0 + scale_ref[...]) + shift_ref[...]).astype(o_ref.dtype)


def _norm_call(x2d, g_row, scale3, shift3, tm, tiles_per_group):
    m, d = x2d.shape
    r = scale3.shape[1]
    mod_spec = pl.BlockSpec((None, r, d), lambda i: (i // tiles_per_group, 0, 0))
    return pl.pallas_call(
        _norm_kernel,
        grid=(m // tm,),
        in_specs=[pl.BlockSpec((tm, d), lambda i: (i, 0)),
                  pl.BlockSpec((1, d), lambda i: (0, 0)),
                  mod_spec, mod_spec],
        out_specs=pl.BlockSpec((tm, d), lambda i: (i, 0)),
        out_shape=jax.ShapeDtypeStruct((m, d), BF16),
        compiler_params=_cparams(("parallel",)),
        name="norm_modulate",
    )(x2d, g_row, scale3, shift3)


SEG_OFFS = tuple(sum(SPLIT_SIZES[:i]) for i in range(len(SPLIT_SIZES) + 1))


def _proj_nt_kernel(h_ref, w_ref, o_ref):
    o_ref[...] = _dot_nt(h_ref[...], w_ref[...].astype(BF16)).astype(o_ref.dtype)


def _proj_nt_call(h, wt3, row_of_tile, n_cols, tm, tn, out_dtype, name):
    m, k = h.shape
    return pl.pallas_call(
        _proj_nt_kernel,
        grid=(m // tm, n_cols // tn),
        in_specs=[pl.BlockSpec((tm, k), lambda i, j: (i, 0)),
                  pl.BlockSpec((pl.Element(tn), pl.Element(k)),
                               lambda i, j: (pl.multiple_of(row_of_tile(j), 16), 0))],
        out_specs=pl.BlockSpec((tm, tn), lambda i, j: (i, j)),
        out_shape=jax.ShapeDtypeStruct((m, n_cols), out_dtype),
        compiler_params=_cparams(("parallel", "arbitrary")),
        name=name,
    )(h, wt3)


def _group_b_row(j, tn):
    per = B_Q // tn
    seg = j // per
    start = SEG_OFFS[B_SEGS[-1]]
    for s in range(len(B_SEGS) - 2, -1, -1):
        start = jnp.where(seg == s, SEG_OFFS[B_SEGS[s]], start)
    return start + (j % per) * tn


def _proj_tail_kernel(h_ref, wkv_ref, wik_ref, wg_ref, wiw_ref, k_ref, v_ref, kidx_ref, small_ref):
    h = h_ref[...]
    kv = _dot_nt(h, wkv_ref[...].astype(BF16))
    k_ref[...] = kv[:, 0:B_KV]
    v_ref[...] = kv[:, B_KV:]
    kidx_ref[...] = _dot_nt(h, wik_ref[...].astype(BF16))
    gate = _dot_nt(h, wg_ref[...].astype(BF16))
    iw = _dot_nt(h, wiw_ref[...].astype(BF16))
    lane = lax.broadcasted_iota(I32, gate.shape, 1)
    small_ref[...] = jnp.where(lane < LANE_IW, gate, jnp.where(lane < LANE_IW + IDX_HEADS, iw, 0.0))


def _proj_tail_call(h, wt3, tm):
    m, k = h.shape
    widths = (B_KV, B_KV, IDX_DH, 128)
    beta0, kv0, ik0, iw0 = SEG_OFFS[2], SEG_OFFS[5], SEG_OFFS[9], SEG_OFFS[10]
    assert SEG_OFFS[3] - beta0 == LANE_DECAY and SEG_OFFS[4] - beta0 == LANE_IW and SEG_OFFS[6] - kv0 == B_KV

    def rows(start, size):
        return pl.BlockSpec((pl.Element(size), pl.Element(k)), lambda i: (start, 0))

    return pl.pallas_call(
        _proj_tail_kernel,
        grid=(m // tm,),
        in_specs=[pl.BlockSpec((tm, k), lambda i: (i, 0)),
                  rows(kv0, 2 * B_KV), rows(ik0, IDX_DH), rows(beta0, 128), rows(iw0 - LANE_IW, 128)],
        out_specs=[pl.BlockSpec((tm, w), lambda i: (i, 0)) for w in widths],
        out_shape=[jax.ShapeDtypeStruct((m, w), F32) for w in widths],
        compiler_params=_cparams(("parallel",)),
        name="in_proj_tail",
    )(h, wt3, wt3, wt3, wt3)


def _gates_kernel(sm_ref, alog_ref, dtb_ref, o_ref, *, chunk):
    x = sm_ref[...]
    tm = x.shape[0]
    lane = lax.broadcasted_iota(I32, x.shape, 1)
    beta = jax.nn.sigmoid(x)
    z = x + dtb_ref[...]
    softplus = jnp.maximum(z, 0.0) + jnp.log1p(jnp.exp(-jnp.abs(z)))
    g = -jnp.exp(alog_ref[...]) * softplus
    g = jnp.where((lane >= LANE_DECAY) & (lane < LANE_DECAY + GDN_V_HEADS), g, 0.0)
    ri = lax.broadcasted_iota(I32, (tm, tm), 0)
    ci = lax.broadcasted_iota(I32, (tm, tm), 1)
    shift = int(math.log2(chunk))
    tri = jnp.where((ri >= ci) & ((ri >> shift) == (ci >> shift)), 1.0, 0.0).astype(BF16)
    g1, g2, g3 = _split3(g)
    gc = _dot(tri, g1) + _dot(tri, g2) + _dot(tri, g3)
    o_ref[...] = jnp.where(lane < GDN_V_HEADS, beta, gc)


def _gates_call(small, alog_row, dtb_row, chunk):
    m = small.shape[0]
    tm = 256
    return pl.pallas_call(
        functools.partial(_gates_kernel, chunk=chunk),
        grid=(m // tm,),
        in_specs=[pl.BlockSpec((tm, 128), lambda i: (i, 0)),
                  pl.BlockSpec((1, 128), lambda i: (0, 0)),
                  pl.BlockSpec((1, 128), lambda i: (0, 0))],
        out_specs=pl.BlockSpec((tm, 128), lambda i: (i, 0)),
        out_shape=jax.ShapeDtypeStruct((m, 128), F32),
        compiler_params=_cparams(("parallel",)),
        name="gdn_gates",
    )(small, alog_row, dtb_row)


def _unrolled_loop(n, unroll, fn):
    unroll = min(unroll, n)
    assert n % unroll == 0

    def body(t, carry):
        for u in range(unroll):
            fn(t * unroll + u)
        return carry

    lax.fori_loop(0, n // unroll, body, 0)


def _gdn_kernel(*refs, L, C, HP, NSEG, has_state):
    ins = 13 if has_state else 12
    (q_ref, k_ref, v_ref, z_ref, bq_ref, bk_ref, bv_ref, wq_ref, wk_ref, wv_ref, gates_ref, gn_ref) = refs[:12]
    s0_ref = refs[12] if has_state else None
    (y_ref, sout_ref, xq_s, xk_s, xv_s, beta_s, g_s, st_s, a_s, p_s, rhs_s, kt_s, qkd_s, qd_s,
     m_s, n_s, qe_s, o_s) = refs[ins:]
    nh = 2 * HP
    nc = L // C
    hq0 = pl.program_id(1) * HP

    def first_segment():
        xq_s[0:8, :] = bq_ref[...]
        xk_s[0:8, :] = bk_ref[...]
        xv_s[0:8, :] = bv_ref[...]
        if has_state:
            st_s[...] = s0_ref[...]
        else:
            st_s[...] = jnp.zeros(st_s.shape, F32)

    def next_segment():
        xq_s[0:8, :] = xq_s[L:L + 8, :]
        xk_s[0:8, :] = xk_s[L:L + 8, :]
        xv_s[0:8, :] = xv_s[L:L + 8, :]

    if NSEG == 1:
        first_segment()
    else:
        pl.when(pl.program_id(2) == 0)(first_segment)
        pl.when(pl.program_id(2) > 0)(next_segment)
    xq_s[8:, :] = q_ref[...].astype(F32)
    xk_s[8:, :] = k_ref[...].astype(F32)
    xv_s[8:, :] = v_ref[...].astype(F32)

    gates = gates_ref[...]
    lane = lax.broadcasted_iota(I32, gates.shape, 1)
    for idx in range(nh):
        hv = 2 * hq0 + idx
        bcol = jnp.sum(jnp.where(lane == hv + LANE_BETA, gates, 0.0), axis=-1, keepdims=True)
        gcol = jnp.sum(jnp.where(lane == hv + LANE_DECAY, gates, 0.0), axis=-1, keepdims=True)
        beta_s[idx] = jnp.broadcast_to(bcol, (L, 128))
        g_s[idx] = jnp.broadcast_to(gcol, (L, 128))

    ii = lax.broadcasted_iota(I32, (C, C), 0)
    jj = lax.broadcasted_iota(I32, (C, C), 1)
    gn = gn_ref[...]

    def conv(x_s, r0, cols, w):
        win = x_s[pl.ds(r0, C + 8), cols]
        acc = (win[5:5 + C] * w[0:1] + win[6:6 + C] * w[1:2] + win[7:7 + C] * w[2:3] + win[8:8 + C] * w[3:4])
        return _silu(acc)

    def l2n(x):
        return x * lax.rsqrt(jnp.sum(x * x, axis=-1, keepdims=True) + L2_EPS)

    def lanes(i, width):
        if isinstance(i, int):
            return slice(i * width, (i + 1) * width)
        return pl.ds(pl.multiple_of(i * width, width), width)

    def pair_terms(j):
        c, p = (j, 0) if HP == 1 else (j // HP, j % HP)
        r0 = pl.multiple_of(c * C, C)
        rows = pl.ds(r0, C)
        lq, lv = lanes(p, 128), lanes(p, 256)
        q = l2n(conv(xq_s, r0, lq, wq_ref[:, lq])) * (GDN_DK ** -0.5)
        k = l2n(conv(xk_s, r0, lq, wk_ref[:, lq]))
        vc = conv(xv_s, r0, lv, wv_ref[:, lv])
        k16 = k.astype(BF16)
        kk = _dot_nt(k16, k16)
        qk = _dot_nt(q.astype(BF16), k16)
        for r in range(2):
            idx = 2 * p + r
            e = c * nh + idx
            gb = g_s[idx, rows, :]
            bb = beta_s[idx, rows, :]
            e_g = jnp.exp(gb)
            e_tail = jnp.exp(gb[C - 1:C, :] - gb)
            if C < 128:
                gpad = jnp.concatenate([gb, jnp.zeros((128 - C, 128), F32)], axis=0)
            else:
                gpad = gb
            dexp = gb[:, 0:C] - gpad.T[0:C, 0:C]
            dm = jnp.exp(jnp.where(ii >= jj, dexp, NEG_INF))
            a = jnp.where(ii > jj, kk * dm, 0.0) * bb[:, 0:C]
            a_s[e] = a.astype(a_s.dtype)
            p_s[e] = jnp.where(ii == jj, 1.0, 0.0) - a
            rhs_s[e] = jnp.concatenate([vc[:, r * 128:(r + 1) * 128] * bb, k * bb * e_g], axis=1).astype(rhs_s.dtype)
            kt_s[e] = (k * e_tail).astype(kt_s.dtype)
            qkd_s[e] = (qk * dm).astype(qkd_s.dtype)
            qd_s[e] = q * e_g

    def square(e):
        ak = a_s[e].astype(BF16)
        a_s[e] = _dot(ak, ak).astype(a_s.dtype)

    def inverse_stage(e, last):
        ak = a_s[e].astype(BF16)
        pk = p_s[e]
        p_s[e] = pk + _dot(pk.astype(BF16), ak)
        if not last:
            a_s[e] = _dot(ak, ak).astype(a_s.dtype)

    def solve(e):
        rhs_s[e] = _dot(p_s[e].astype(BF16), rhs_s[e].astype(BF16)).astype(rhs_s.dtype)

    def state_terms(e):
        sol16 = rhs_s[e].astype(BF16)
        mn = _dot_tn(kt_s[e].astype(BF16), sol16)
        qo = _dot(qkd_s[e].astype(BF16), sol16)
        n_s[e] = mn[:, 0:GDN_DV]
        m_s[e] = mn[:, GDN_DV:].astype(m_s.dtype)
        o_s[e] = qo[:, 0:GDN_DV]
        qe_s[e] = (qd_s[e] - qo[:, GDN_DV:]).astype(qe_s.dtype)

    def head_step(c, idx):
        e = c * nh + idx
        r0 = pl.multiple_of(c * C, C)
        rows = pl.ds(r0, C)
        s = st_s[idx]
        s16 = s.astype(BF16)
        o = _dot(qe_s[e].astype(BF16), s16) + o_s[e]
        e_last = jnp.exp(g_s[idx, pl.ds(r0 + C - 1, 1), :])
        st_s[idx] = s * e_last - _dot(m_s[e].astype(BF16), s16) + n_s[e]
        on = o * lax.rsqrt(jnp.mean(o * o, axis=-1, keepdims=True) + NORM_EPS) * gn
        zz = z_ref[rows, lanes(idx, 128)].astype(F32)
        y_ref[rows, lanes(idx, 128)] = (on * _silu(zz)).astype(y_ref.dtype)

    ne = nc * nh
    stages = int(math.log2(C)) - 1
    _unrolled_loop(nc * HP, 2, pair_terms)
    _unrolled_loop(ne, 32, square)
    for st in range(stages):
        _unrolled_loop(ne, 32, functools.partial(inverse_stage, last=(st == stages - 1)))
    _unrolled_loop(ne, 16, solve)
    _unrolled_loop(ne, 8, state_terms)

    def chunk_step(c, carry):
        if nh <= 4:
            for idx in range(nh):
                head_step(c, idx)
        else:
            _unrolled_loop(nh, 8, functools.partial(head_step, c))
        return carry

    lax.fori_loop(0, nc, chunk_step, 0)
    sout_ref[...] = st_s[...]


def _gdn_call(big, convbuf8, conv_w, gates, gn_row, s0, *, B, L, C, HP, NSEG, y_dtype):
    has_state = s0 is not None
    ls = L // NSEG
    ne = (ls // C) * 2 * HP
    half = BF16 if C % 16 == 0 else F32
    nq = GDN_QK_HEADS // HP
    wq, wv = HP * 128, HP * 256
    kq0 = A_QK // wq
    v0 = (2 * A_QK) // wv
    z0 = OFF_AZ // wv
    in_specs = [
        pl.BlockSpec((ls, wq), lambda b, h, s: (b * NSEG + s, h)),
        pl.BlockSpec((ls, wq), lambda b, h, s: (b * NSEG + s, kq0 + h)),
        pl.BlockSpec((ls, wv), lambda b, h, s: (b * NSEG + s, v0 + h)),
        pl.BlockSpec((ls, wv), lambda b, h, s: (b * NSEG + s, z0 + h)),
        pl.BlockSpec((None, 8, wq), lambda b, h, s: (b, 0, h)),
        pl.BlockSpec((None, 8, wq), lambda b, h, s: (b, 0, kq0 + h)),
        pl.BlockSpec((None, 8, wv), lambda b, h, s: (b, 0, v0 + h)),
        pl.BlockSpec((CONV_W, wq), lambda b, h, s: (0, h)),
        pl.BlockSpec((CONV_W, wq), lambda b, h, s: (0, kq0 + h)),
        pl.BlockSpec((CONV_W, wv), lambda b, h, s: (0, v0 + h)),
        pl.BlockSpec((ls, 128), lambda b, h, s: (b * NSEG + s, 0)),
        pl.BlockSpec((1, 128), lambda b, h, s: (0, 0)),
    ]
    args = [big, big, big, big, convbuf8, convbuf8, convbuf8, conv_w, conv_w, conv_w, gates, gn_row]
    if has_state:
        in_specs.append(pl.BlockSpec((None, 2 * HP, GDN_DK, GDN_DV), lambda b, h, s: (b, h, 0, 0)))
        args.append(s0)
    return pl.pallas_call(
        functools.partial(_gdn_kernel, L=ls, C=C, HP=HP, NSEG=NSEG, has_state=has_state),
        grid=(B, nq, NSEG),
        in_specs=in_specs,
        out_specs=[pl.BlockSpec((ls, wv), lambda b, h, s: (b * NSEG + s, h)),
                   pl.BlockSpec((None, 2 * HP, GDN_DK, GDN_DV), lambda b, h, s: (b, h, 0, 0))],
        out_shape=[jax.ShapeDtypeStruct((B * L, A_V), y_dtype),
                   jax.ShapeDtypeStruct((B, GDN_V_HEADS, GDN_DK, GDN_DV), F32)],
        scratch_shapes=[pltpu.VMEM((ls + 8, wq), F32), pltpu.VMEM((ls + 8, wq), F32), pltpu.VMEM((ls + 8, wv), F32),
                        pltpu.VMEM((2 * HP, ls, 128), F32), pltpu.VMEM((2 * HP, ls, 128), F32),
                        pltpu.VMEM((2 * HP, GDN_DK, GDN_DV), F32),
                        pltpu.VMEM((ne, C, C), half), pltpu.VMEM((ne, C, C), F32),
                        pltpu.VMEM((ne, C, GDN_DV + GDN_DK), half), pltpu.VMEM((ne, C, GDN_DK), half),
                        pltpu.VMEM((ne, C, C), half), pltpu.VMEM((ne, C, GDN_DK), F32),
                        pltpu.VMEM((ne, GDN_DK, GDN_DV), BF16), pltpu.VMEM((ne, GDN_DK, GDN_DV), F32),
                        pltpu.VMEM((ne, C, GDN_DK), half), pltpu.VMEM((ne, C, GDN_DV), F32)],
        compiler_params=_cparams(("parallel", "arbitrary", "arbitrary")),
        name="gdn_prompt" if not has_state else "gdn_sample",
    )(*args)


KEY_NEG_INF = -2 ** 31 + 0x7FFFFF


def _key_to_float(key):
    bits = key ^ ((key >> 31) & jnp.int32(0x7FFFFFFF))
    return pltpu.bitcast(bits, F32)


def _selection_bias(score_ref, bias_ref, qpos0, rows, width, topk):
    def step(it, t):
        cand = t + jnp.left_shift(jnp.int32(1), 31 - it)
        ge = score_ref[:, 0:width] >= _key_to_float(cand)
        cnt = jnp.sum(jnp.where(ge, 1.0, 0.0), axis=-1, keepdims=True)
        return jnp.where((cnt >= float(topk)) | (cand <= KEY_NEG_INF), cand, t)

    thr = _key_to_float(lax.fori_loop(0, 32, step, jnp.full((rows, 1), INT_MIN, I32)))
    score = score_ref[:, 0:width]
    causal = (lax.broadcasted_iota(I32, (rows, width), 1)
              <= qpos0 + lax.broadcasted_iota(I32, (rows, width), 0))
    sel = (score >= thr) & causal
    bias_ref[:, 0:width] = jnp.where(sel, 0.0, NEG_INF)
    n_sel = jnp.sum(jnp.where(sel, 1.0, 0.0), axis=-1, keepdims=True)
    has_ties = jnp.max(n_sel) > float(topk)

    @pl.when(has_ties)
    def _():
        need = float(topk) - jnp.sum(jnp.where(score > thr, 1.0, 0.0), axis=-1, keepdims=True)
        eq16 = jnp.where(score == thr, 1.0, 0.0).astype(BF16)

        def tile(j, carry):
            c0 = pl.multiple_of(j * 128, 128)
            ri = lax.broadcasted_iota(I32, (width, 128), 0)
            ci = lax.broadcasted_iota(I32, (width, 128), 1) + c0
            upper = jnp.where(ri <= ci, 1.0, 0.0).astype(BF16)
            prefix = _dot(eq16, upper)
            s_t = score_ref[:, pl.ds(c0, 128)]
            keep = (s_t > thr) | ((s_t == thr) & (prefix <= need))
            old = bias_ref[:, pl.ds(c0, 128)]
            bias_ref[:, pl.ds(c0, 128)] = jnp.where(keep, old, NEG_INF)
            return carry

        lax.fori_loop(0, width // 128, tile, 0)


SOFTMAX_LOG2_SCALE = (ATT_DH ** -0.5) * math.log2(math.e)


def _dsa_prompt_kernel(qi_ref, wi_ref, kidx_ref, q_ref, k_ref, v_ref, z_ref, o_ref,
                       score_s, bias_s, kidx16_s, k16_s, vext_s, m_s, acc_s, *, QB, KT, L, topk):
    i = pl.program_id(1)
    q0 = i * QB
    nt = (q0 + QB - 1) // KT + 1
    groups = ATT_HEADS // ATT_KV_HEADS

    @pl.when(i == 0)
    def _():
        kidx16_s[...] = kidx_ref[...].astype(BF16)
        k16_s[...] = k_ref[...].astype(BF16)
        for n in range(ATT_KV_HEADS):
            vext_s[n, :, 0:ATT_DH] = v_ref[:, n * ATT_DH:(n + 1) * ATT_DH].astype(BF16)
            vext_s[n, :, ATT_DH:] = jnp.ones((L, ATT_DH), BF16)

    wi = wi_ref[...] * ((IDX_DH ** -0.5) * (IDX_HEADS ** -0.5))
    rowpos = q0 + lax.broadcasted_iota(I32, (QB, KT), 0)
    col = lax.broadcasted_iota(I32, (QB, KT), 1)

    def tile_rows(t):
        return pl.ds(pl.multiple_of(t * KT, KT), KT)

    def index_tile(t, carry):
        kid = kidx16_s[tile_rows(t), :]
        acc = jnp.zeros((QB, KT), F32)
        for h in range(IDX_HEADS):
            lg = _dot_nt(qi_ref[:, h * IDX_DH:(h + 1) * IDX_DH], kid)
            acc = acc + jnp.maximum(lg, 0.0) * wi[:, LANE_IW + h:LANE_IW + h + 1]
        score_s[:, tile_rows(t)] = jnp.where(col + t * KT <= rowpos, acc, NEG_INF)
        return carry

    lax.fori_loop(0, nt, index_tile, 0)
    for w in range(1, L // KT + 1):
        @pl.when(nt == w)
        def _():
            _selection_bias(score_s, bias_s, q0, QB, w * KT, topk)

    m_s[...] = jnp.full(m_s.shape, NEG_INF, F32)
    acc_s[...] = jnp.zeros(acc_s.shape, F32)

    def attend_tile(t, carry):
        bias = bias_s[:, tile_rows(t)]
        for n in range(ATT_KV_HEADS):
            kn = k16_s[tile_rows(t), n * ATT_DH:(n + 1) * ATT_DH]
            vn = vext_s[n, tile_rows(t), :]
            for g in range(groups):
                h = n * groups + g
                sb = _dot_nt(q_ref[:, h * ATT_DH:(h + 1) * ATT_DH], kn) + bias
                m_old = m_s[h]
                m_new = jnp.maximum(m_old, jnp.max(sb, axis=-1, keepdims=True))
                m_ref = jnp.where(m_new == NEG_INF, 0.0, m_new)
                p = jnp.exp2((sb - jnp.concatenate([m_ref] * (KT // 128), axis=1)) * SOFTMAX_LOG2_SCALE)
                alpha = jnp.exp2((m_old - m_ref) * SOFTMAX_LOG2_SCALE)
                m_s[h] = m_new
                acc_s[h] = acc_s[h] * jnp.concatenate([alpha, alpha], axis=1) + _dot(p.astype(BF16), vn)
        return carry

    lax.fori_loop(0, nt, attend_tile, 0)
    for h in range(ATT_HEADS):
        cols = slice(h * ATT_DH, (h + 1) * ATT_DH)
        zz = z_ref[:, cols].astype(F32)
        acc = acc_s[h]
        o_ref[:, cols] = (acc[:, 0:ATT_DH] / acc[:, ATT_DH:] * _silu(zz)).astype(o_ref.dtype)


def _dsa_prompt_call(big, small, kidx, k, v, *, B, L, QB):
    topk = min(TOPK_MAX, L // 4)
    nqb = L // QB
    iq0 = OFF_IQ // B_Q
    bq0 = OFF_BQ // B_Q
    bz0 = OFF_BZ // B_Q
    return pl.pallas_call(
        functools.partial(_dsa_prompt_kernel, QB=QB, KT=256, L=L, topk=topk),
        grid=(B, nqb),
        in_specs=[pl.BlockSpec((QB, IDX_Q), lambda b, i: (b * nqb + i, iq0)),
                  pl.BlockSpec((QB, 128), lambda b, i: (b * nqb + i, 0)),
                  pl.BlockSpec((L, IDX_DH), lambda b, i: (b, 0)),
                  pl.BlockSpec((QB, B_Q), lambda b, i: (b * nqb + i, bq0)),
                  pl.BlockSpec((L, B_KV), lambda b, i: (b, 0)),
                  pl.BlockSpec((L, B_KV), lambda b, i: (b, 0)),
                  pl.BlockSpec((QB, B_Q), lambda b, i: (b * nqb + i, bz0))],
        out_specs=pl.BlockSpec((QB, B_Q), lambda b, i: (b * nqb + i, 0)),
        out_shape=jax.ShapeDtypeStruct((B * L, B_Q), BF16),
        scratch_shapes=[pltpu.VMEM((QB, L), F32), pltpu.VMEM((QB, L), F32),
                        pltpu.VMEM((L, IDX_DH), BF16), pltpu.VMEM((L, B_KV), BF16),
                        pltpu.VMEM((ATT_KV_HEADS, L, 2 * ATT_DH), BF16),
                        pltpu.VMEM((ATT_HEADS, QB, 128), F32), pltpu.VMEM((ATT_HEADS, QB, 2 * ATT_DH), F32)],
        compiler_params=_cparams(("parallel", "arbitrary")),
        name="dsa_prompt",
    )(big, small, kidx, big, k, v, big)


def _dsa_sample_kernel(pt_ref, qi_ref, wi_ref, kin_ref, q_ref, kn_ref, vn_ref, z_ref, ck_hbm, cv_hbm, ci_hbm,
                       o_ref, kbuf, vbuf, ibuf, sems, score_s, bias_s, *, NP, T, topk):
    b = pl.program_id(0)
    nb = pl.num_programs(0)
    slot = b % 2
    past = NP * PAGE_SIZE
    width = past + 128

    kvp = ATT_KV_HEADS * PAGE_SIZE

    def page_copies(bb, sl, p):
        page = pt_ref[bb * NP + p]
        rows = pl.ds(pl.multiple_of(p * PAGE_SIZE, PAGE_SIZE), PAGE_SIZE)
        src2 = pl.ds(pl.multiple_of(page * kvp, kvp), kvp)
        dst2 = pl.ds(pl.multiple_of(p * kvp, kvp), kvp)
        return (pltpu.make_async_copy(ck_hbm.at[src2], kbuf.at[sl, dst2], sems.at[0, sl]),
                pltpu.make_async_copy(cv_hbm.at[src2], vbuf.at[sl, dst2], sems.at[1, sl]),
                pltpu.make_async_copy(ci_hbm.at[page], ibuf.at[sl, rows], sems.at[2, sl]))

    def start_fetch(bb, sl):
        def body(p, carry):
            for cp in page_copies(bb, sl, p):
                cp.start()
            return carry
        lax.fori_loop(0, NP, body, 0)

    def wait_fetch(bb, sl):
        def body(p, carry):
            for cp in page_copies(bb, sl, p):
                cp.wait()
            return carry
        lax.fori_loop(0, NP, body, 0)

    @pl.when(b == 0)
    def _():
        for sl in range(2):
            kbuf[sl, ATT_KV_HEADS * past:, :] = jnp.zeros((ATT_KV_HEADS * 128, ATT_DH), F32)
            vbuf[sl, ATT_KV_HEADS * past:, :] = jnp.zeros((ATT_KV_HEADS * 128, ATT_DH), F32)
            ibuf[sl, past:, :] = jnp.zeros((128, IDX_DH), F32)
        start_fetch(0, 0)

    @pl.when(b + 1 < nb)
    def _():
        start_fetch(b + 1, 1 - slot)

    for n in range(ATT_KV_HEADS):
        new_rows = pl.ds(ATT_KV_HEADS * past + n, T, stride=ATT_KV_HEADS)
        kbuf[slot, new_rows, :] = kn_ref[:, n * ATT_DH:(n + 1) * ATT_DH]
        vbuf[slot, new_rows, :] = vn_ref[:, n * ATT_DH:(n + 1) * ATT_DH]
    ibuf[slot, past:past + T, :] = kin_ref[...]
    wait_fetch(b, slot)

    qi = qi_ref[...]
    qis = jnp.concatenate([qi[:, h * IDX_DH:(h + 1) * IDX_DH] for h in range(IDX_HEADS)], axis=0).astype(BF16)
    wi = wi_ref[...] * ((IDX_DH ** -0.5) * (IDX_HEADS ** -0.5))
    lg = _dot_nt(qis, ibuf[slot].astype(BF16))
    score = jnp.zeros((T, width), F32)
    for h in range(IDX_HEADS):
        score = score + jnp.maximum(lg[h * T:(h + 1) * T, :], 0.0) * wi[:, LANE_IW + h:LANE_IW + h + 1]
    qpos = past + lax.broadcasted_iota(I32, (T, width), 0)
    kpos = lax.broadcasted_iota(I32, (T, width), 1)
    causal = kpos <= qpos
    score_s[...] = jnp.where(causal, score, NEG_INF)
    _selection_bias(score_s, bias_s, past, T, width, topk)

    groups = ATT_HEADS // ATT_KV_HEADS
    q = q_ref[...]
    bias = jnp.concatenate([bias_s[...]] * groups, axis=0)
    for n in range(ATT_KV_HEADS):
        head_rows = pl.ds(n, width, stride=ATT_KV_HEADS)
        kn = kbuf[slot, head_rows, :].astype(BF16)
        vn = vbuf[slot, head_rows, :].astype(BF16)
        qs = jnp.concatenate([q[:, (n * groups + g) * ATT_DH:(n * groups + g + 1) * ATT_DH] for g in range(groups)],
                             axis=0).astype(BF16)
        s = _dot_nt(qs, kn) * (ATT_DH ** -0.5) + bias
        m = jnp.max(s, axis=-1, keepdims=True)
        p = jnp.exp(s - m)
        denom = jnp.sum(p, axis=-1, keepdims=True)
        o = _dot(p.astype(BF16), vn) / denom
        for g in range(groups):
            h = n * groups + g
            cols = slice(h * ATT_DH, (h + 1) * ATT_DH)
            zz = z_ref[:, cols]
            o_ref[:, cols] = o[g * T:(g + 1) * T, :] * _silu(zz)


def _dsa_sample_call(big, small, kidx, k, v, cache_k, cache_v, cache_kidx, page_table, *, B, T):
    npages = page_table.shape[1]
    past = npages * PAGE_SIZE
    topk = min(TOPK_MAX, (past + T) // 4)
    width = past + 128
    iq0 = OFF_IQ // B_Q
    bq0 = OFF_BQ // B_Q
    bz0 = OFF_BZ // B_Q
    grid_spec = pltpu.PrefetchScalarGridSpec(
        num_scalar_prefetch=1,
        grid=(B,),
        in_specs=[pl.BlockSpec((T, IDX_Q), lambda b, pt: (b, iq0)),
                  pl.BlockSpec((T, 128), lambda b, pt: (b, 0)),
                  pl.BlockSpec((T, IDX_DH), lambda b, pt: (b, 0)),
                  pl.BlockSpec((T, B_Q), lambda b, pt: (b, bq0)),
                  pl.BlockSpec((T, B_KV), lambda b, pt: (b, 0)),
                  pl.BlockSpec((T, B_KV), lambda b, pt: (b, 0)),
                  pl.BlockSpec((T, B_Q), lambda b, pt: (b, bz0)),
                  pl.BlockSpec(memory_space=pl.ANY),
                  pl.BlockSpec(memory_space=pl.ANY),
                  pl.BlockSpec(memory_space=pl.ANY)],
        out_specs=pl.BlockSpec((T, B_Q), lambda b, pt: (b, 0)),
        scratch_shapes=[pltpu.VMEM((2, ATT_KV_HEADS * width, ATT_DH), F32),
                        pltpu.VMEM((2, ATT_KV_HEADS * width, ATT_DH), F32),
                        pltpu.VMEM((2, width, IDX_DH), F32), pltpu.SemaphoreType.DMA((3, 2)),
                        pltpu.VMEM((T, width), F32), pltpu.VMEM((T, width), F32)])
    return pl.pallas_call(
        functools.partial(_dsa_sample_kernel, NP=npages, T=T, topk=topk),
        grid_spec=grid_spec,
        out_shape=jax.ShapeDtypeStruct((B * T, B_Q), F32),
        compiler_params=_cparams(("arbitrary",)),
        name="dsa_sample",
    )(page_table.reshape(-1), big, small, kidx, big, k, v, big, cache_k, cache_v, cache_kidx)


def _merge_kernel(ya_ref, yb_ref, ga_ref, gb_ref, wpa_ref, wpb_ref, o_ref):
    pa = _dot(ya_ref[...].astype(BF16), wpa_ref[...])
    pb = _dot(yb_ref[...].astype(BF16), wpb_ref[...])
    o = jax.nn.sigmoid(ga_ref[...].astype(F32)) * pa + jax.nn.sigmoid(gb_ref[...].astype(F32)) * pb
    o_ref[...] = o.astype(o_ref.dtype)


def _merge_call(ya, yb, big, wpa, wpb, tm, tn):
    m = ya.shape[0]
    n = wpa.shape[1]
    ga0 = OFF_GA // tn
    gb0 = OFF_GB // tn
    return pl.pallas_call(
        _merge_kernel,
        grid=(m // tm, n // tn),
        in_specs=[pl.BlockSpec((tm, A_V), lambda i, j: (i, 0)),
                  pl.BlockSpec((tm, B_Q), lambda i, j: (i, 0)),
                  pl.BlockSpec((tm, tn), lambda i, j: (i, ga0 + j)),
                  pl.BlockSpec((tm, tn), lambda i, j: (i, gb0 + j)),
                  pl.BlockSpec((A_V, tn), lambda i, j: (0, j)),
                  pl.BlockSpec((B_Q, tn), lambda i, j: (0, j))],
        out_specs=pl.BlockSpec((tm, tn), lambda i, j: (i, j)),
        out_shape=jax.ShapeDtypeStruct((m, n), BF16),
        compiler_params=_cparams(("parallel", "arbitrary")),
        name="merge_proj",
    )(ya, yb, big, big, wpa, wpb)


def _final_kernel(m_ref, w_ref, x_ref, gate_ref, g_ref, o_ref):
    t = _dot(m_ref[...], w_ref[...])
    y = t * lax.rsqrt(jnp.mean(t * t, axis=-1, keepdims=True) + NORM_EPS) * g_ref[...]
    o_ref[...] = x_ref[...] + gate_ref[...] * y


def _final_call(merged, w_out, x2d, gate3, g_row, tm, tiles_per_group):
    m, d = x2d.shape
    r = gate3.shape[1]
    return pl.pallas_call(
        _final_kernel,
        grid=(m // tm,),
        in_specs=[pl.BlockSpec((tm, d), lambda i: (i, 0)),
                  pl.BlockSpec((d, d), lambda i: (0, 0)),
                  pl.BlockSpec((tm, d), lambda i: (i, 0)),
                  pl.BlockSpec((None, r, d), lambda i: (i // tiles_per_group, 0, 0)),
                  pl.BlockSpec((1, d), lambda i: (0, 0))],
        out_specs=pl.BlockSpec((tm, d), lambda i: (i, 0)),
        out_shape=jax.ShapeDtypeStruct((m, d), F32),
        compiler_params=_cparams(("parallel",)),
        name="out_proj",
    )(merged, w_out, x2d, gate3, g_row)


def _cast_kernel(x_ref, o_ref):
    o_ref[...] = x_ref[...].astype(o_ref.dtype)


def _cast_bf16_call(w3):
    _, r, n = w3.shape
    tr = 256
    return pl.pallas_call(
        _cast_kernel,
        grid=(r // tr,),
        in_specs=[pl.BlockSpec((None, tr, n), lambda i: (0, i, 0))],
        out_specs=pl.BlockSpec((tr, n), lambda i: (i, 0)),
        out_shape=jax.ShapeDtypeStruct((r, n), BF16),
        compiler_params=_cparams(("parallel",)),
        name="w_cast",
    )(w3)


def _lane_row(vec, lane0):
    return jnp.zeros((1, 128), F32).at[0, lane0:lane0 + vec.shape[0]].set(vec.astype(F32))


def _layer(x, mod, conv_hist, s0, attend, wts, *, gdn_chunk, gdn_hp, gdn_nseg, tm_proj, tn_proj, act_dtype):
    (pre_g, wt3, conv_w, alog_row, dtb_row, gn_row, wpa, wpb, wout, post_g) = wts
    B, L, D = x.shape
    M = B * L
    x2d = x.reshape(M, D)
    shift, scale, gate = mod[:, 0:D], mod[:, D:2 * D], mod[:, 2 * D:3 * D]
    if L >= 256:
        tm_n = 256
        per = L // tm_n
        scale3, shift3, gate3 = scale[:, None, :], shift[:, None, :], gate[:, None, :]
    else:
        tm_n = M
        per = 1
        scale3, shift3, gate3 = (jnp.repeat(t, L, axis=0)[None] for t in (scale, shift, gate))
    h = _norm_call(x2d, pre_g, scale3, shift3, tm_n, per)
    tm = min(tm_proj, M)
    big_a = _proj_nt_call(h, wt3, lambda j: j * tn_proj, N_A, tm, tn_proj, act_dtype, "in_proj_a")
    big_b = _proj_nt_call(h, wt3, functools.partial(_group_b_row, tn=tn_proj), N_B, tm, tn_proj, act_dtype,
                          "in_proj_b")
    k, v, kidx, small = _proj_tail_call(h, wt3, min(tm, 1024))

    gates = _gates_call(small, alog_row, dtb_row, gdn_chunk)
    if conv_hist is None:
        convbuf8 = jnp.zeros((B, 8, A_CONV_CH), F32)
    else:
        convbuf8 = jnp.concatenate([jnp.zeros((B, 5, A_CONV_CH), F32), conv_hist.astype(F32)], axis=1)
    y_a, s_new = _gdn_call(big_a, convbuf8, conv_w, gates, gn_row, s0, B=B, L=L, C=gdn_chunk, HP=gdn_hp,
                           NSEG=gdn_nseg, y_dtype=act_dtype)
    y_b = attend(big_b, small, kidx, k, v)
    tm_m = min(1024, M)
    merged = _merge_call(y_a, y_b, big_b, wpa, wpb, tm_m, 512)
    tm_f = min(512, M)
    per_f = (L // tm_f) if L >= 256 else 1
    y = _final_call(merged, wout, x2d, gate3, post_g, tm_f, per_f)
    return y.reshape(B, L, D), big_a, h, k, v, kidx, s_new


def kernel(x_prompt, x_sample, c_prompt, c_sample, cache_k, cache_v, cache_kidx, state_gdn, state_conv,
           page_table, w_ada, b_ada, pre_norm_g, w_in, conv_w, a_log, dt_bias, gdn_norm_g, w_pa, w_pb, w_out,
           post_norm_g):
    depth = w_in.shape[0]
    assert depth == 1, "single-layer trunk"
    BP, LP, D = x_prompt.shape
    BS, LS, _ = x_sample.shape
    n_pool = cache_k.shape[1]

    wt3 = jnp.swapaxes(w_in, 1, 2)[0]
    wts =(pre_norm_g[0][None, :], wt3, conv_w[0],
           _lane_row(a_log[0], LANE_DECAY), _lane_row(dt_bias[0], LANE_DECAY), gdn_norm_g[0][None, :],
           _cast_bf16_call(w_pa), _cast_bf16_call(w_pb), _cast_bf16_call(w_out), post_norm_g[0][None, :])

    rows = BP + BS
    rows_pad = -(-rows // 16) * 16
    c_all = jnp.concatenate([c_prompt, c_sample, jnp.zeros((rows_pad - rows, D), F32)], axis=0)
    mod = _mod_call(c_all, w_ada[0], b_ada[0][None, :])
    mod_p, mod_s = mod[0:BP], mod[BP:BP + BS]

    attend_p = functools.partial(_dsa_prompt_call, B=BP, L=LP, QB=128)
    y_p, big_p, h_p, k_p, v_p, kidx_p, s_p = _layer(
        x_prompt, mod_p, None, None, attend_p, wts,
        gdn_chunk=min(GDN_CHUNK, LP), gdn_hp=2, gdn_nseg=2, tm_proj=2048, tn_proj=512, act_dtype=BF16)
    h_last = h_p.reshape(BP, LP, D)[:, LP - 8:, :].reshape(BP * 8, D)
    aqkv_last = _proj_nt_call(h_last, wt3, lambda j: j * 512, A_CONV_CH, BP * 8, 512, F32, "in_proj_conv_tail")
    conv_p = aqkv_last.reshape(BP, 8, A_CONV_CH)[:, 8 - (CONV_W - 1):, :]

    ck = cache_k[0].reshape(n_pool * PAGE_SIZE * ATT_KV_HEADS, ATT_DH)
    cv = cache_v[0].reshape(n_pool * PAGE_SIZE * ATT_KV_HEADS, ATT_DH)
    ci = cache_kidx[0]
    attend_s = functools.partial(_dsa_sample_call, cache_k=ck, cache_v=cv, cache_kidx=ci, page_table=page_table,
                                 B=BS, T=LS)
    y_s, big_s, _, k_s, v_s, kidx_s, s_s = _layer(
        x_sample, mod_s, state_conv[0], state_gdn[0], attend_s, wts,
        gdn_chunk=min(GDN_CHUNK, LS), gdn_hp=16, gdn_nseg=1, tm_proj=256, tn_proj=512, act_dtype=F32)
    conv_s = big_s[:, 0:A_CONV_CH].reshape(BS, LS, A_CONV_CH)[:, LS - (CONV_W - 1):, :]

    kv_shape_p = (1, BP, LP, ATT_KV_HEADS, ATT_DH)
    kv_shape_s = (1, BS, LS, ATT_KV_HEADS, ATT_DH)
    return (y_p, y_s,
            k_p.reshape(kv_shape_p), v_p.reshape(kv_shape_p), kidx_p.reshape(1, BP, LP, IDX_DH),
            s_p[None], conv_p[None],
            k_s.reshape(kv_shape_s), v_s.reshape(kv_shape_s), kidx_s.reshape(1, BS, LS, IDX_DH),
            s_s[None], conv_s[None])
```

```python
import functools
import math

import jax
import jax.numpy as jnp
from jax import lax
from jax.experimental import pallas as pl
from jax.experimental.pallas import tpu as pltpu

F32 = jnp.float32
BF16 = jnp.bfloat16
I32 = jnp.int32

D_MODEL = 2048
PAGE_SIZE = 128
GDN_QK_HEADS = 16
GDN_V_HEADS = 32
GDN_DK = 128
GDN_DV = 128
CONV_W = 4
GDN_CHUNK = 64
ATT_HEADS = 16
ATT_KV_HEADS = 2
ATT_DH = 128
IDX_HEADS = 16
IDX_DH = 128
TOPK_MAX = 256
NORM_EPS = 1e-6
L2_EPS = 1e-6

A_QK = GDN_QK_HEADS * GDN_DK
A_V = GDN_V_HEADS * GDN_DV
A_CONV_CH = 2 * A_QK + A_V
B_Q = ATT_HEADS * ATT_DH
B_KV = ATT_KV_HEADS * ATT_DH
IDX_Q = IDX_HEADS * IDX_DH
SPLIT_SIZES = (A_CONV_CH, A_V, GDN_V_HEADS, GDN_V_HEADS, B_Q, B_KV, B_KV, B_Q, IDX_Q, IDX_DH, IDX_HEADS,
               D_MODEL, D_MODEL)
B_SEGS = (4, 7, 8, 11, 12)
OFF_AQKV = 0
OFF_AZ = OFF_AQKV + A_CONV_CH
N_A = OFF_AZ + A_V
OFF_BQ = 0
OFF_BZ = OFF_BQ + B_Q
OFF_IQ = OFF_BZ + B_Q
OFF_GA = OFF_IQ + IDX_Q
OFF_GB = OFF_GA + D_MODEL
N_B = OFF_GB + D_MODEL
LANE_BETA = 0
LANE_DECAY = GDN_V_HEADS
LANE_IW = 2 * GDN_V_HEADS

VMEM_LIMIT = 56 * 1024 * 1024
NEG_INF = float("-inf")
INT_MIN = -2 ** 31


def _cparams(sem):
    return pltpu.CompilerParams(dimension_semantics=sem, vmem_limit_bytes=VMEM_LIMIT)


def _dot(a, b):
    return jnp.dot(a, b, preferred_element_type=F32)


def _dot_nt(a, b):
    return lax.dot_general(a, b, (((1,), (1,)), ((), ())), preferred_element_type=F32)


def _dot_tn(a, b):
    return lax.dot_general(a, b, (((0,), (0,)), ((), ())), preferred_element_type=F32)


def _split2(x):
    hi = x.astype(BF16)
    lo = (x - hi.astype(F32)).astype(BF16)
    return hi, lo


def _split3(x):
    x1 = x.astype(BF16)
    r = x - x1.astype(F32)
    x2 = r.astype(BF16)
    x3 = (r - x2.astype(F32)).astype(BF16)
    return x1, x2, x3


def _dot_3pass(a, b):
    ah, al = _split2(a)
    bh, bl = _split2(b)
    return _dot(ah, bh) + _dot(al, bh) + _dot(ah, bl)


def _silu(x):
    h = 0.5 * x
    return h + h * jnp.tanh(h)


def _mod_kernel(c_ref, w_ref, b_ref, o_ref):
    o_ref[...] = _dot_3pass(_silu(c_ref[...]), w_ref[...]) + b_ref[...]


def _mod_call(c_pad, w_ada, b_ada):
    rows, d = c_pad.shape
    n = w_ada.shape[1]
    tn = 768
    return pl.pallas_call(
        _mod_kernel,
        grid=(n // tn,),
        in_specs=[pl.BlockSpec((rows, d), lambda j: (0, 0)),
                  pl.BlockSpec((d, tn), lambda j: (0, j)),
                  pl.BlockSpec((1, tn), lambda j: (0, j))],
        out_specs=pl.BlockSpec((rows, tn), lambda j: (0, j)),
        out_shape=jax.ShapeDtypeStruct((rows, n), F32),
        compiler_params=_cparams(("arbitrary",)),
        name="adaln_mod",
    )(c_pad, w_ada, b_ada)


def _norm_kernel(x_ref, g_ref, scale_ref, shift_ref, o_ref):
    x = x_ref[...]
    y = x * lax.rsqrt(jnp.mean(x * x, axis=-1, keepdims=True) + NORM_EPS) * g_ref[...]
    o_ref[...] = (y * (1.0 + scale_ref[...]) + shift_ref[...]).astype(o_ref.dtype)


def _norm_call(x2d, g_row, scale3, shift3, tm, tiles_per_group):
    m, d = x2d.shape
    r = scale3.shape[1]
    mod_spec = pl.BlockSpec((None, r, d), lambda i: (i // tiles_per_group, 0, 0))
    return pl.pallas_call(
        _norm_kernel,
        grid=(m // tm,),
        in_specs=[pl.BlockSpec((tm, d), lambda i: (i, 0)),
                  pl.BlockSpec((1, d), lambda i: (0, 0)),
                  mod_spec, mod_spec],
        out_specs=pl.BlockSpec((tm, d), lambda i: (i, 0)),
        out_shape=jax.ShapeDtypeStruct((m, d), BF16),
        compiler_params=_cparams(("parallel",)),
        name="norm_modulate",
    )(x2d, g_row, scale3, shift3)


SEG_OFFS = tuple(sum(SPLIT_SIZES[:i]) for i in range(len(SPLIT_SIZES) + 1))


def _proj_nt_kernel(h_ref, w_ref, o_ref):
    o_ref[...] = _dot_nt(h_ref[...], w_ref[...].astype(BF16)).astype(o_ref.dtype)


def _proj_nt_call(h, wt3, row_of_tile, n_cols, tm, tn, out_dtype, name):
    m, k = h.shape
    return pl.pallas_call(
        _proj_nt_kernel,
        grid=(m // tm, n_cols // tn),
        in_specs=[pl.BlockSpec((tm, k), lambda i, j: (i, 0)),
                  pl.BlockSpec((pl.Element(tn), pl.Element(k)),
                               lambda i, j: (pl.multiple_of(row_of_tile(j), 16), 0))],
        out_specs=pl.BlockSpec((tm, tn), lambda i, j: (i, j)),
        out_shape=jax.ShapeDtypeStruct((m, n_cols), out_dtype),
        compiler_params=_cparams(("parallel", "arbitrary")),
        name=name,
    )(h, wt3)


def _group_b_row(j, tn):
    per = B_Q // tn
    seg = j // per
    start = SEG_OFFS[B_SEGS[-1]]
    for s in range(len(B_SEGS) - 2, -1, -1):
        start = jnp.where(seg == s, SEG_OFFS[B_SEGS[s]], start)
    return start + (j % per) * tn


def _proj_tail_kernel(h_ref, wkv_ref, wik_ref, wg_ref, wiw_ref, k_ref, v_ref, kidx_ref, small_ref):
    h = h_ref[...]
    kv = _dot_nt(h, wkv_ref[...].astype(BF16))
    k_ref[...] = kv[:, 0:B_KV]
    v_ref[...] = kv[:, B_KV:]
    kidx_ref[...] = _dot_nt(h, wik_ref[...].astype(BF16))
    gate = _dot_nt(h, wg_ref[...].astype(BF16))
    iw = _dot_nt(h, wiw_ref[...].astype(BF16))
    lane = lax.broadcasted_iota(I32, gate.shape, 1)
    small_ref[...] = jnp.where(lane < LANE_IW, gate, jnp.where(lane < LANE_IW + IDX_HEADS, iw, 0.0))


def _proj_tail_call(h, wt3, tm):
    m, k = h.shape
    widths = (B_KV, B_KV, IDX_DH, 128)
    beta0, kv0, ik0, iw0 = SEG_OFFS[2], SEG_OFFS[5], SEG_OFFS[9], SEG_OFFS[10]
    assert SEG_OFFS[3] - beta0 == LANE_DECAY and SEG_OFFS[4] - beta0 == LANE_IW and SEG_OFFS[6] - kv0 == B_KV

    def rows(start, size):
        return pl.BlockSpec((pl.Element(size), pl.Element(k)), lambda i: (start, 0))

    return pl.pallas_call(
        _proj_tail_kernel,
        grid=(m // tm,),
        in_specs=[pl.BlockSpec((tm, k), lambda i: (i, 0)),
                  rows(kv0, 2 * B_KV), rows(ik0, IDX_DH), rows(beta0, 128), rows(iw0 - LANE_IW, 128)],
        out_specs=[pl.BlockSpec((tm, w), lambda i: (i, 0)) for w in widths],
        out_shape=[jax.ShapeDtypeStruct((m, w), F32) for w in widths],
        compiler_params=_cparams(("parallel",)),
        name="in_proj_tail",
    )(h, wt3, wt3, wt3, wt3)


def _gates_kernel(sm_ref, alog_ref, dtb_ref, o_ref, *, chunk):
    x = sm_ref[...]
    tm = x.shape[0]
    lane = lax.broadcasted_iota(I32, x.shape, 1)
    beta = jax.nn.sigmoid(x)
    z = x + dtb_ref[...]
    softplus = jnp.maximum(z, 0.0) + jnp.log1p(jnp.exp(-jnp.abs(z)))
    g = -jnp.exp(alog_ref[...]) * softplus
    g = jnp.where((lane >= LANE_DECAY) & (lane < LANE_DECAY + GDN_V_HEADS), g, 0.0)
    ri = lax.broadcasted_iota(I32, (tm, tm), 0)
    ci = lax.broadcasted_iota(I32, (tm, tm), 1)
    shift = int(math.log2(chunk))
    tri = jnp.where((ri >= ci) & ((ri >> shift) == (ci >> shift)), 1.0, 0.0).astype(BF16)
    g1, g2, g3 = _split3(g)
    gc = _dot(tri, g1) + _dot(tri, g2) + _dot(tri, g3)
    o_ref[...] = jnp.where(lane < GDN_V_HEADS, beta, gc)


def _gates_call(small, alog_row, dtb_row, chunk):
    m = small.shape[0]
    tm = 256
    return pl.pallas_call(
        functools.partial(_gates_kernel, chunk=chunk),
        grid=(m // tm,),
        in_specs=[pl.BlockSpec((tm, 128), lambda i: (i, 0)),
                  pl.BlockSpec((1, 128), lambda i: (0, 0)),
                  pl.BlockSpec((1, 128), lambda i: (0, 0))],
        out_specs=pl.BlockSpec((tm, 128), lambda i: (i, 0)),
        out_shape=jax.ShapeDtypeStruct((m, 128), F32),
        compiler_params=_cparams(("parallel",)),
        name="gdn_gates",
    )(small, alog_row, dtb_row)


def _unrolled_loop(n, unroll, fn):
    unroll = min(unroll, n)
    assert n % unroll == 0

    def body(t, carry):
        for u in range(unroll):
            fn(t * unroll + u)
        return carry

    lax.fori_loop(0, n // unroll, body, 0)


def _gdn_kernel(*refs, L, C, HP, NSEG, has_state):
    ins = 13 if has_state else 12
    (q_ref, k_ref, v_ref, z_ref, bq_ref, bk_ref, bv_ref, wq_ref, wk_ref, wv_ref, gates_ref, gn_ref) = refs[:12]
    s0_ref = refs[12] if has_state else None
    (y_ref, sout_ref, xq_s, xk_s, xv_s, beta_s, g_s, st_s, a_s, p_s, rhs_s, kt_s, qkd_s, qd_s,
     m_s, n_s, qe_s, o_s) = refs[ins:]
    nh = 2 * HP
    nc = L // C
    hq0 = pl.program_id(1) * HP

    def first_segment():
        xq_s[0:8, :] = bq_ref[...]
        xk_s[0:8, :] = bk_ref[...]
        xv_s[0:8, :] = bv_ref[...]
        if has_state:
            st_s[...] = s0_ref[...]
        else:
            st_s[...] = jnp.zeros(st_s.shape, F32)

    def next_segment():
        xq_s[0:8, :] = xq_s[L:L + 8, :]
        xk_s[0:8, :] = xk_s[L:L + 8, :]
        xv_s[0:8, :] = xv_s[L:L + 8, :]

    if NSEG == 1:
        first_segment()
    else:
        pl.when(pl.program_id(2) == 0)(first_segment)
        pl.when(pl.program_id(2) > 0)(next_segment)
    xq_s[8:, :] = q_ref[...].astype(F32)
    xk_s[8:, :] = k_ref[...].astype(F32)
    xv_s[8:, :] = v_ref[...].astype(F32)

    gates = gates_ref[...]
    lane = lax.broadcasted_iota(I32, gates.shape, 1)
    for idx in range(nh):
        hv = 2 * hq0 + idx
        bcol = jnp.sum(jnp.where(lane == hv + LANE_BETA, gates, 0.0), axis=-1, keepdims=True)
        gcol = jnp.sum(jnp.where(lane == hv + LANE_DECAY, gates, 0.0), axis=-1, keepdims=True)
        beta_s[idx] = jnp.broadcast_to(bcol, (L, 128))
        g_s[idx] = jnp.broadcast_to(gcol, (L, 128))

    ii = lax.broadcasted_iota(I32, (C, C), 0)
    jj = lax.broadcasted_iota(I32, (C, C), 1)
    gn = gn_ref[...]

    def conv(x_s, r0, cols, w):
        win = x_s[pl.ds(r0, C + 8), cols]
        acc = (win[5:5 + C] * w[0:1] + win[6:6 + C] * w[1:2] + win[7:7 + C] * w[2:3] + win[8:8 + C] * w[3:4])
        return _silu(acc)

    def l2n(x):
        return x * lax.rsqrt(jnp.sum(x * x, axis=-1, keepdims=True) + L2_EPS)

    def lanes(i, width):
        if isinstance(i, int):
            return slice(i * width, (i + 1) * width)
        return pl.ds(pl.multiple_of(i * width, width), width)

    def pair_terms(j):
        c, p = (j, 0) if HP == 1 else (j // HP, j % HP)
        r0 = pl.multiple_of(c * C, C)
        rows = pl.ds(r0, C)
        lq, lv = lanes(p, 128), lanes(p, 256)
        q = l2n(conv(xq_s, r0, lq, wq_ref[:, lq])) * (GDN_DK ** -0.5)
        k = l2n(conv(xk_s, r0, lq, wk_ref[:, lq]))
        vc = conv(xv_s, r0, lv, wv_ref[:, lv])
        k16 = k.astype(BF16)
        kk = _dot_nt(k16, k16)
        qk = _dot_nt(q.astype(BF16), k16)
        for r in range(2):
            idx = 2 * p + r
            e = c * nh + idx
            gb = g_s[idx, rows, :]
            bb = beta_s[idx, rows, :]
            e_g = jnp.exp(gb)
            e_tail = jnp.exp(gb[C - 1:C, :] - gb)
            if C < 128:
                gpad = jnp.concatenate([gb, jnp.zeros((128 - C, 128), F32)], axis=0)
            else:
                gpad = gb
            dexp = gb[:, 0:C] - gpad.T[0:C, 0:C]
            dm = jnp.exp(jnp.where(ii >= jj, dexp, NEG_INF))
            a = jnp.where(ii > jj, kk * dm, 0.0) * bb[:, 0:C]
            a_s[e] = a.astype(a_s.dtype)
            p_s[e] = jnp.where(ii == jj, 1.0, 0.0) - a
            rhs_s[e] = jnp.concatenate([vc[:, r * 128:(r + 1) * 128] * bb, k * bb * e_g], axis=1).astype(rhs_s.dtype)
            kt_s[e] = (k * e_tail).astype(kt_s.dtype)
            qkd_s[e] = (qk * dm).astype(qkd_s.dtype)
            qd_s[e] = q * e_g

    def square(e):
        ak = a_s[e].astype(BF16)
        a_s[e] = _dot(ak, ak).astype(a_s.dtype)

    def inverse_stage(e, last):
        ak = a_s[e].astype(BF16)
        pk = p_s[e]
        p_s[e] = pk + _dot(pk.astype(BF16), ak)
        if not last:
            a_s[e] = _dot(ak, ak).astype(a_s.dtype)

    def solve(e):
        rhs_s[e] = _dot(p_s[e].astype(BF16), rhs_s[e].astype(BF16)).astype(rhs_s.dtype)

    def state_terms(e):
        sol16 = rhs_s[e].astype(BF16)
        mn = _dot_tn(kt_s[e].astype(BF16), sol16)
        qo = _dot(qkd_s[e].astype(BF16), sol16)
        n_s[e] = mn[:, 0:GDN_DV]
        m_s[e] = mn[:, GDN_DV:].astype(m_s.dtype)
        o_s[e] = qo[:, 0:GDN_DV]
        qe_s[e] = (qd_s[e] - qo[:, GDN_DV:]).astype(qe_s.dtype)

    def head_step(c, idx):
        e = c * nh + idx
        r0 = pl.multiple_of(c * C, C)
        rows = pl.ds(r0, C)
        s = st_s[idx]
        s16 = s.astype(BF16)
        o = _dot(qe_s[e].astype(BF16), s16) + o_s[e]
        e_last = jnp.exp(g_s[idx, pl.ds(r0 + C - 1, 1), :])
        st_s[idx] = s * e_last - _dot(m_s[e].astype(BF16), s16) + n_s[e]
        on = o * lax.rsqrt(jnp.mean(o * o, axis=-1, keepdims=True) + NORM_EPS) * gn
        zz = z_ref[rows, lanes(idx, 128)].astype(F32)
        y_ref[rows, lanes(idx, 128)] = (on * _silu(zz)).astype(y_ref.dtype)

    ne = nc * nh
    stages = int(math.log2(C)) - 1
    _unrolled_loop(nc * HP, 2, pair_terms)
    _unrolled_loop(ne, 32, square)
    for st in range(stages):
        _unrolled_loop(ne, 32, functools.partial(inverse_stage, last=(st == stages - 1)))
    _unrolled_loop(ne, 16, solve)
    _unrolled_loop(ne, 16, state_terms)

    def chunk_step(c, carry):
        if nh <= 4:
            for idx in range(nh):
                head_step(c, idx)
        else:
            _unrolled_loop(nh, 8, functools.partial(head_step, c))
        return carry

    lax.fori_loop(0, nc, chunk_step, 0)
    sout_ref[...] = st_s[...]


def _gdn_call(big, convbuf8, conv_w, gates, gn_row, s0, *, B, L, C, HP, NSEG, y_dtype):
    has_state = s0 is not None
    ls = L // NSEG
    ne = (ls // C) * 2 * HP
    half = BF16 if C % 16 == 0 else F32
    nq = GDN_QK_HEADS // HP
    wq, wv = HP * 128, HP * 256
    kq0 = A_QK // wq
    v0 = (2 * A_QK) // wv
    z0 = OFF_AZ // wv
    in_specs = [
        pl.BlockSpec((ls, wq), lambda b, h, s: (b * NSEG + s, h)),
        pl.BlockSpec((ls, wq), lambda b, h, s: (b * NSEG + s, kq0 + h)),
        pl.BlockSpec((ls, wv), lambda b, h, s: (b * NSEG + s, v0 + h)),
        pl.BlockSpec((ls, wv), lambda b, h, s: (b * NSEG + s, z0 + h)),
        pl.BlockSpec((None, 8, wq), lambda b, h, s: (b, 0, h)),
        pl.BlockSpec((None, 8, wq), lambda b, h, s: (b, 0, kq0 + h)),
        pl.BlockSpec((None, 8, wv), lambda b, h, s: (b, 0, v0 + h)),
        pl.BlockSpec((CONV_W, wq), lambda b, h, s: (0, h)),
        pl.BlockSpec((CONV_W, wq), lambda b, h, s: (0, kq0 + h)),
        pl.BlockSpec((CONV_W, wv), lambda b, h, s: (0, v0 + h)),
        pl.BlockSpec((ls, 128), lambda b, h, s: (b * NSEG + s, 0)),
        pl.BlockSpec((1, 128), lambda b, h, s: (0, 0)),
    ]
    args = [big, big, big, big, convbuf8, convbuf8, convbuf8, conv_w, conv_w, conv_w, gates, gn_row]
    if has_state:
        in_specs.append(pl.BlockSpec((None, 2 * HP, GDN_DK, GDN_DV), lambda b, h, s: (b, h, 0, 0)))
        args.append(s0)
    return pl.pallas_call(
        functools.partial(_gdn_kernel, L=ls, C=C, HP=HP, NSEG=NSEG, has_state=has_state),
        grid=(B, nq, NSEG),
        in_specs=in_specs,
        out_specs=[pl.BlockSpec((ls, wv), lambda b, h, s: (b * NSEG + s, h)),
                   pl.BlockSpec((None, 2 * HP, GDN_DK, GDN_DV), lambda b, h, s: (b, h, 0, 0))],
        out_shape=[jax.ShapeDtypeStruct((B * L, A_V), y_dtype),
                   jax.ShapeDtypeStruct((B, GDN_V_HEADS, GDN_DK, GDN_DV), F32)],
        scratch_shapes=[pltpu.VMEM((ls + 8, wq), F32), pltpu.VMEM((ls + 8, wq), F32), pltpu.VMEM((ls + 8, wv), F32),
                        pltpu.VMEM((2 * HP, ls, 128), F32), pltpu.VMEM((2 * HP, ls, 128), F32),
                        pltpu.VMEM((2 * HP, GDN_DK, GDN_DV), F32),
                        pltpu.VMEM((ne, C, C), half), pltpu.VMEM((ne, C, C), F32),
                        pltpu.VMEM((ne, C, GDN_DV + GDN_DK), half), pltpu.VMEM((ne, C, GDN_DK), half),
                        pltpu.VMEM((ne, C, C), half), pltpu.VMEM((ne, C, GDN_DK), F32),
                        pltpu.VMEM((ne, GDN_DK, GDN_DV), BF16), pltpu.VMEM((ne, GDN_DK, GDN_DV), F32),
                        pltpu.VMEM((ne, C, GDN_DK), half), pltpu.VMEM((ne, C, GDN_DV), F32)],
        compiler_params=_cparams(("parallel", "arbitrary", "arbitrary")),
        name="gdn_prompt" if not has_state else "gdn_sample",
    )(*args)


KEY_NEG_INF = -2 ** 31 + 0x7FFFFF


def _key_to_float(key):
    bits = key ^ ((key >> 31) & jnp.int32(0x7FFFFFFF))
    return pltpu.bitcast(bits, F32)


def _count_true(mask):
    ones = jnp.where(mask, 1.0, 0.0)
    parts = [ones[:, i * 128:(i + 1) * 128] for i in range(mask.shape[1] // 128)]
    while len(parts) > 1:
        parts = [a + b for a, b in zip(parts[0::2], parts[1::2])] + ([parts[-1]] if len(parts) % 2 else [])
    return jnp.sum(parts[0], axis=-1, keepdims=True)


def _selection_bias(score_ref, bias_ref, qpos0, rows, width, topk):
    def step(it, t):
        cand = t + jnp.left_shift(jnp.int32(1), 31 - it)
        ge = score_ref[:, 0:width] >= _key_to_float(cand)
        cnt = _count_true(ge)
        return jnp.where((cnt >= float(topk)) | (cand <= KEY_NEG_INF), cand, t)

    thr = _key_to_float(lax.fori_loop(0, 32, step, jnp.full((rows, 1), INT_MIN, I32)))
    score = score_ref[:, 0:width]
    causal = (lax.broadcasted_iota(I32, (rows, width), 1)
              <= qpos0 + lax.broadcasted_iota(I32, (rows, width), 0))
    sel = (score >= thr) & causal
    bias_ref[:, 0:width] = jnp.where(sel, 0.0, NEG_INF)
    n_sel = _count_true(sel)
    has_ties = jnp.max(n_sel) > float(topk)

    @pl.when(has_ties)
    def _():
        need = float(topk) - _count_true(score > thr)
        eq16 = jnp.where(score == thr, 1.0, 0.0).astype(BF16)

        def tile(j, carry):
            c0 = pl.multiple_of(j * 128, 128)
            ri = lax.broadcasted_iota(I32, (width, 128), 0)
            ci = lax.broadcasted_iota(I32, (width, 128), 1) + c0
            upper = jnp.where(ri <= ci, 1.0, 0.0).astype(BF16)
            prefix = _dot(eq16, upper)
            s_t = score_ref[:, pl.ds(c0, 128)]
            keep = (s_t > thr) | ((s_t == thr) & (prefix <= need))
            old = bias_ref[:, pl.ds(c0, 128)]
            bias_ref[:, pl.ds(c0, 128)] = jnp.where(keep, old, NEG_INF)
            return carry

        lax.fori_loop(0, width // 128, tile, 0)


SOFTMAX_LOG2_SCALE = (ATT_DH ** -0.5) * math.log2(math.e)


def _dsa_prompt_kernel(qi_ref, wi_ref, kidx_ref, q_ref, k_ref, v_ref, z_ref, o_ref,
                       score_s, bias_s, kidx16_s, k16_s, vext_s, m_s, acc_s, *, QB, SUB, KT, L, topk):
    i = pl.program_id(1)
    q0 = i * QB
    nt = (q0 + QB - 1) // KT + 1
    groups = ATT_HEADS // ATT_KV_HEADS

    @pl.when(i == 0)
    def _():
        kidx16_s[...] = kidx_ref[...].astype(BF16)
        k16_s[...] = k_ref[...].astype(BF16)
        for n in range(ATT_KV_HEADS):
            vext_s[n, :, 0:ATT_DH] = v_ref[:, n * ATT_DH:(n + 1) * ATT_DH].astype(BF16)
            vext_s[n, :, ATT_DH:] = jnp.ones((L, ATT_DH), BF16)

    rowpos = q0 + lax.broadcasted_iota(I32, (SUB, KT), 0)
    col = lax.broadcasted_iota(I32, (SUB, KT), 1)

    def tile_rows(t):
        return pl.ds(pl.multiple_of(t * KT, KT), KT)

    subs = [slice(r, r + SUB) for r in range(0, QB, SUB)]

    def index_tile(t, carry):
        kid = kidx16_s[tile_rows(t), :]
        for rs in subs:
            wi = wi_ref[rs, :] * ((IDX_DH ** -0.5) * (IDX_HEADS ** -0.5))
            acc = jnp.zeros((SUB, KT), F32)
            for h in range(IDX_HEADS):
                lg = _dot_nt(qi_ref[rs, h * IDX_DH:(h + 1) * IDX_DH], kid)
                acc = acc + jnp.maximum(lg, 0.0) * wi[:, LANE_IW + h:LANE_IW + h + 1]
            causal = col + t * KT <= rowpos + rs.start
            score_s[rs, tile_rows(t)] = jnp.where(causal, acc, NEG_INF)
        return carry

    lax.fori_loop(0, nt, index_tile, 0)
    for w in range(1, L // KT + 1):
        @pl.when(nt == w)
        def _():
            _selection_bias(score_s, bias_s, q0, QB, w * KT, topk)

    m_s[...] = jnp.full(m_s.shape, NEG_INF, F32)
    acc_s[...] = jnp.zeros(acc_s.shape, F32)

    def attend_tile(t, carry):
        for n in range(ATT_KV_HEADS):
            kn = k16_s[tile_rows(t), n * ATT_DH:(n + 1) * ATT_DH]
            vn = vext_s[n, tile_rows(t), :]
            for g in range(groups):
                h = n * groups + g
                for rs in subs:
                    sb = _dot_nt(q_ref[rs, h * ATT_DH:(h + 1) * ATT_DH], kn) + bias_s[rs, tile_rows(t)]
                    m_old = m_s[h, rs, :]
                    m_new = jnp.maximum(m_old, jnp.max(sb, axis=-1, keepdims=True))
                    m_ref = jnp.where(m_new == NEG_INF, 0.0, m_new)
                    p = jnp.exp2((sb - jnp.concatenate([m_ref] * (KT // 128), axis=1)) * SOFTMAX_LOG2_SCALE)
                    alpha = jnp.exp2((m_old - m_ref) * SOFTMAX_LOG2_SCALE)
                    m_s[h, rs, :] = m_new
                    acc_s[h, rs, :] = (acc_s[h, rs, :] * jnp.concatenate([alpha, alpha], axis=1)
                                       + _dot(p.astype(BF16), vn))
        return carry

    lax.fori_loop(0, nt, attend_tile, 0)
    for h in range(ATT_HEADS):
        cols = slice(h * ATT_DH, (h + 1) * ATT_DH)
        zz = z_ref[:, cols].astype(F32)
        acc = acc_s[h]
        o_ref[:, cols] = (acc[:, 0:ATT_DH] / acc[:, ATT_DH:] * _silu(zz)).astype(o_ref.dtype)


def _dsa_prompt_call(big, small, kidx, k, v, *, B, L, QB):
    topk = min(TOPK_MAX, L // 4)
    nqb = L // QB
    iq0 = OFF_IQ // B_Q
    bq0 = OFF_BQ // B_Q
    bz0 = OFF_BZ // B_Q
    return pl.pallas_call(
        functools.partial(_dsa_prompt_kernel, QB=QB, SUB=128, KT=256, L=L, topk=topk),
        grid=(B, nqb),
        in_specs=[pl.BlockSpec((QB, IDX_Q), lambda b, i: (b * nqb + i, iq0)),
                  pl.BlockSpec((QB, 128), lambda b, i: (b * nqb + i, 0)),
                  pl.BlockSpec((L, IDX_DH), lambda b, i: (b, 0)),
                  pl.BlockSpec((QB, B_Q), lambda b, i: (b * nqb + i, bq0)),
                  pl.BlockSpec((L, B_KV), lambda b, i: (b, 0)),
                  pl.BlockSpec((L, B_KV), lambda b, i: (b, 0)),
                  pl.BlockSpec((QB, B_Q), lambda b, i: (b * nqb + i, bz0))],
        out_specs=pl.BlockSpec((QB, B_Q), lambda b, i: (b * nqb + i, 0)),
        out_shape=jax.ShapeDtypeStruct((B * L, B_Q), BF16),
        scratch_shapes=[pltpu.VMEM((QB, L), F32), pltpu.VMEM((QB, L), F32),
                        pltpu.VMEM((L, IDX_DH), BF16), pltpu.VMEM((L, B_KV), BF16),
                        pltpu.VMEM((ATT_KV_HEADS, L, 2 * ATT_DH), BF16),
                        pltpu.VMEM((ATT_HEADS, QB, 128), F32), pltpu.VMEM((ATT_HEADS, QB, 2 * ATT_DH), F32)],
        compiler_params=_cparams(("parallel", "arbitrary")),
        name="dsa_prompt",
    )(big, small, kidx, big, k, v, big)


def _dsa_sample_kernel(pt_ref, qi_ref, wi_ref, kin_ref, q_ref, kn_ref, vn_ref, z_ref, ck_hbm, cv_hbm, ci_hbm,
                       o_ref, kbuf, vbuf, ibuf, sems, score_s, bias_s, *, NP, T, topk):
    b = pl.program_id(0)
    nb = pl.num_programs(0)
    slot = b % 2
    past = NP * PAGE_SIZE
    width = past + 128

    kvp = ATT_KV_HEADS * PAGE_SIZE

    def page_copies(bb, sl, p):
        page = pt_ref[bb * NP + p]
        rows = pl.ds(pl.multiple_of(p * PAGE_SIZE, PAGE_SIZE), PAGE_SIZE)
        src2 = pl.ds(pl.multiple_of(page * kvp, kvp), kvp)
        dst2 = pl.ds(pl.multiple_of(p * kvp, kvp), kvp)
        return (pltpu.make_async_copy(ck_hbm.at[src2], kbuf.at[sl, dst2], sems.at[0, sl]),
                pltpu.make_async_copy(cv_hbm.at[src2], vbuf.at[sl, dst2], sems.at[1, sl]),
                pltpu.make_async_copy(ci_hbm.at[page], ibuf.at[sl, rows], sems.at[2, sl]))

    def start_fetch(bb, sl):
        def body(p, carry):
            for cp in page_copies(bb, sl, p):
                cp.start()
            return carry
        lax.fori_loop(0, NP, body, 0)

    def wait_fetch(bb, sl):
        def body(p, carry):
            for cp in page_copies(bb, sl, p):
                cp.wait()
            return carry
        lax.fori_loop(0, NP, body, 0)

    @pl.when(b == 0)
    def _():
        for sl in range(2):
            kbuf[sl, ATT_KV_HEADS * past:, :] = jnp.zeros((ATT_KV_HEADS * 128, ATT_DH), F32)
            vbuf[sl, ATT_KV_HEADS * past:, :] = jnp.zeros((ATT_KV_HEADS * 128, ATT_DH), F32)
            ibuf[sl, past:, :] = jnp.zeros((128, IDX_DH), F32)
        start_fetch(0, 0)

    @pl.when(b + 1 < nb)
    def _():
        start_fetch(b + 1, 1 - slot)

    for n in range(ATT_KV_HEADS):
        new_rows = pl.ds(ATT_KV_HEADS * past + n, T, stride=ATT_KV_HEADS)
        kbuf[slot, new_rows, :] = kn_ref[:, n * ATT_DH:(n + 1) * ATT_DH]
        vbuf[slot, new_rows, :] = vn_ref[:, n * ATT_DH:(n + 1) * ATT_DH]
    ibuf[slot, past:past + T, :] = kin_ref[...]
    wait_fetch(b, slot)

    qi = qi_ref[...]
    qis = jnp.concatenate([qi[:, h * IDX_DH:(h + 1) * IDX_DH] for h in range(IDX_HEADS)], axis=0).astype(BF16)
    wi = wi_ref[...] * ((IDX_DH ** -0.5) * (IDX_HEADS ** -0.5))
    lg = _dot_nt(qis, ibuf[slot].astype(BF16))
    score = jnp.zeros((T, width), F32)
    for h in range(IDX_HEADS):
        score = score + jnp.maximum(lg[h * T:(h + 1) * T, :], 0.0) * wi[:, LANE_IW + h:LANE_IW + h + 1]
    qpos = past + lax.broadcasted_iota(I32, (T, width), 0)
    kpos = lax.broadcasted_iota(I32, (T, width), 1)
    causal = kpos <= qpos
    score_s[...] = jnp.where(causal, score, NEG_INF)
    _selection_bias(score_s, bias_s, past, T, width, topk)

    groups = ATT_HEADS // ATT_KV_HEADS
    q = q_ref[...]
    bias = jnp.concatenate([bias_s[...]] * groups, axis=0)
    for n in range(ATT_KV_HEADS):
        head_rows = pl.ds(n, width, stride=ATT_KV_HEADS)
        kn = kbuf[slot, head_rows, :].astype(BF16)
        vn = vbuf[slot, head_rows, :].astype(BF16)
        qs = jnp.concatenate([q[:, (n * groups + g) * ATT_DH:(n * groups + g + 1) * ATT_DH] for g in range(groups)],
                             axis=0).astype(BF16)
        s = _dot_nt(qs, kn) * (ATT_DH ** -0.5) + bias
        m = jnp.max(s, axis=-1, keepdims=True)
        p = jnp.exp(s - m)
        denom = jnp.sum(p, axis=-1, keepdims=True)
        o = _dot(p.astype(BF16), vn) / denom
        for g in range(groups):
            h = n * groups + g
            cols = slice(h * ATT_DH, (h + 1) * ATT_DH)
            zz = z_ref[:, cols]
            o_ref[:, cols] = o[g * T:(g + 1) * T, :] * _silu(zz)


def _dsa_sample_call(big, small, kidx, k, v, cache_k, cache_v, cache_kidx, page_table, *, B, T):
    npages = page_table.shape[1]
    past = npages * PAGE_SIZE
    topk = min(TOPK_MAX, (past + T) // 4)
    width = past + 128
    iq0 = OFF_IQ // B_Q
    bq0 = OFF_BQ // B_Q
    bz0 = OFF_BZ // B_Q
    grid_spec = pltpu.PrefetchScalarGridSpec(
        num_scalar_prefetch=1,
        grid=(B,),
        in_specs=[pl.BlockSpec((T, IDX_Q), lambda b, pt: (b, iq0)),
                  pl.BlockSpec((T, 128), lambda b, pt: (b, 0)),
                  pl.BlockSpec((T, IDX_DH), lambda b, pt: (b, 0)),
                  pl.BlockSpec((T, B_Q), lambda b, pt: (b, bq0)),
                  pl.BlockSpec((T, B_KV), lambda b, pt: (b, 0)),
                  pl.BlockSpec((T, B_KV), lambda b, pt: (b, 0)),
                  pl.BlockSpec((T, B_Q), lambda b, pt: (b, bz0)),
                  pl.BlockSpec(memory_space=pl.ANY),
                  pl.BlockSpec(memory_space=pl.ANY),
                  pl.BlockSpec(memory_space=pl.ANY)],
        out_specs=pl.BlockSpec((T, B_Q), lambda b, pt: (b, 0)),
        scratch_shapes=[pltpu.VMEM((2, ATT_KV_HEADS * width, ATT_DH), F32),
                        pltpu.VMEM((2, ATT_KV_HEADS * width, ATT_DH), F32),
                        pltpu.VMEM((2, width, IDX_DH), F32), pltpu.SemaphoreType.DMA((3, 2)),
                        pltpu.VMEM((T, width), F32), pltpu.VMEM((T, width), F32)])
    return pl.pallas_call(
        functools.partial(_dsa_sample_kernel, NP=npages, T=T, topk=topk),
        grid_spec=grid_spec,
        out_shape=jax.ShapeDtypeStruct((B * T, B_Q), F32),
        compiler_params=_cparams(("arbitrary",)),
        name="dsa_sample",
    )(page_table.reshape(-1), big, small, kidx, big, k, v, big, cache_k, cache_v, cache_kidx)


def _merge_kernel(ya_ref, yb_ref, ga_ref, gb_ref, wpa_ref, wpb_ref, o_ref):
    pa = _dot(ya_ref[...].astype(BF16), wpa_ref[...])
    pb = _dot(yb_ref[...].astype(BF16), wpb_ref[...])
    o = jax.nn.sigmoid(ga_ref[...].astype(F32)) * pa + jax.nn.sigmoid(gb_ref[...].astype(F32)) * pb
    o_ref[...] = o.astype(o_ref.dtype)


def _merge_call(ya, yb, big, wpa, wpb, tm, tn):
    m = ya.shape[0]
    n = wpa.shape[1]
    ga0 = OFF_GA // tn
    gb0 = OFF_GB // tn
    return pl.pallas_call(
        _merge_kernel,
        grid=(m // tm, n // tn),
        in_specs=[pl.BlockSpec((tm, A_V), lambda i, j: (i, 0)),
                  pl.BlockSpec((tm, B_Q), lambda i, j: (i, 0)),
                  pl.BlockSpec((tm, tn), lambda i, j: (i, ga0 + j)),
                  pl.BlockSpec((tm, tn), lambda i, j: (i, gb0 + j)),
                  pl.BlockSpec((A_V, tn), lambda i, j: (0, j)),
                  pl.BlockSpec((B_Q, tn), lambda i, j: (0, j))],
        out_specs=pl.BlockSpec((tm, tn), lambda i, j: (i, j)),
        out_shape=jax.ShapeDtypeStruct((m, n), BF16),
        compiler_params=_cparams(("parallel", "arbitrary")),
        name="merge_proj",
    )(ya, yb, big, big, wpa, wpb)


def _final_kernel(m_ref, w_ref, x_ref, gate_ref, g_ref, o_ref):
    t = _dot(m_ref[...], w_ref[...])
    y = t * lax.rsqrt(jnp.mean(t * t, axis=-1, keepdims=True) + NORM_EPS) * g_ref[...]
    o_ref[...] = x_ref[...] + gate_ref[...] * y


def _final_call(merged, w_out, x2d, gate3, g_row, tm, tiles_per_group):
    m, d = x2d.shape
    r = gate3.shape[1]
    return pl.pallas_call(
        _final_kernel,
        grid=(m // tm,),
        in_specs=[pl.BlockSpec((tm, d), lambda i: (i, 0)),
                  pl.BlockSpec((d, d), lambda i: (0, 0)),
                  pl.BlockSpec((tm, d), lambda i: (i, 0)),
                  pl.BlockSpec((None, r, d), lambda i: (i // tiles_per_group, 0, 0)),
                  pl.BlockSpec((1, d), lambda i: (0, 0))],
        out_specs=pl.BlockSpec((tm, d), lambda i: (i, 0)),
        out_shape=jax.ShapeDtypeStruct((m, d), F32),
        compiler_params=_cparams(("parallel",)),
        name="out_proj",
    )(merged, w_out, x2d, gate3, g_row)


def _cast_kernel(x_ref, o_ref):
    o_ref[...] = x_ref[...].astype(o_ref.dtype)


def _cast_bf16_call(w3):
    _, r, n = w3.shape
    tr = 256
    return pl.pallas_call(
        _cast_kernel,
        grid=(r // tr,),
        in_specs=[pl.BlockSpec((None, tr, n), lambda i: (0, i, 0))],
        out_specs=pl.BlockSpec((tr, n), lambda i: (i, 0)),
        out_shape=jax.ShapeDtypeStruct((r, n), BF16),
        compiler_params=_cparams(("parallel",)),
        name="w_cast",
    )(w3)


def _lane_row(vec, lane0):
    return jnp.zeros((1, 128), F32).at[0, lane0:lane0 + vec.shape[0]].set(vec.astype(F32))


def _layer(x, mod, conv_hist, s0, attend, wts, *, gdn_chunk, gdn_hp, gdn_nseg, tm_proj, tn_proj, act_dtype):
    (pre_g, wt3, conv_w, alog_row, dtb_row, gn_row, wpa, wpb, wout, post_g) = wts
    B, L, D = x.shape
    M = B * L
    x2d = x.reshape(M, D)
    shift, scale, gate = mod[:, 0:D], mod[:, D:2 * D], mod[:, 2 * D:3 * D]
    if L >= 256:
        tm_n = 256
        per = L // tm_n
        scale3, shift3, gate3 = scale[:, None, :], shift[:, None, :], gate[:, None, :]
    else:
        tm_n = M
        per = 1
        scale3, shift3, gate3 = (jnp.repeat(t, L, axis=0)[None] for t in (scale, shift, gate))
    h = _norm_call(x2d, pre_g, scale3, shift3, tm_n, per)
    tm = min(tm_proj, M)
    big_a = _proj_nt_call(h, wt3, lambda j: j * tn_proj, N_A, tm, tn_proj, act_dtype, "in_proj_a")
    big_b = _proj_nt_call(h, wt3, functools.partial(_group_b_row, tn=tn_proj), N_B, tm, tn_proj, act_dtype,
                          "in_proj_b")
    k, v, kidx, small = _proj_tail_call(h, wt3, min(tm, 1024))

    gates = _gates_call(small, alog_row, dtb_row, gdn_chunk)
    if conv_hist is None:
        convbuf8 = jnp.zeros((B, 8, A_CONV_CH), F32)
    else:
        convbuf8 = jnp.concatenate([jnp.zeros((B, 5, A_CONV_CH), F32), conv_hist.astype(F32)], axis=1)
    y_a, s_new = _gdn_call(big_a, convbuf8, conv_w, gates, gn_row, s0, B=B, L=L, C=gdn_chunk, HP=gdn_hp,
                           NSEG=gdn_nseg, y_dtype=act_dtype)
    y_b = attend(big_b, small, kidx, k, v)
    tm_m = min(1024, M)
    merged = _merge_call(y_a, y_b, big_b, wpa, wpb, tm_m, 512)
    tm_f = min(512, M)
    per_f = (L // tm_f) if L >= 256 else 1
    y = _final_call(merged, wout, x2d, gate3, post_g, tm_f, per_f)
    return y.reshape(B, L, D), big_a, h, k, v, kidx, s_new


def kernel(x_prompt, x_sample, c_prompt, c_sample, cache_k, cache_v, cache_kidx, state_gdn, state_conv,
           page_table, w_ada, b_ada, pre_norm_g, w_in, conv_w, a_log, dt_bias, gdn_norm_g, w_pa, w_pb, w_out,
           post_norm_g):
    depth = w_in.shape[0]
    assert depth == 1, "single-layer trunk"
    BP, LP, D = x_prompt.shape
    BS, LS, _ = x_sample.shape
    n_pool = cache_k.shape[1]

    wt3 = jnp.swapaxes(w_in, 1, 2)[0]
    wts =(pre_norm_g[0][None, :], wt3, conv_w[0],
           _lane_row(a_log[0], LANE_DECAY), _lane_row(dt_bias[0], LANE_DECAY), gdn_norm_g[0][None, :],
           _cast_bf16_call(w_pa), _cast_bf16_call(w_pb), _cast_bf16_call(w_out), post_norm_g[0][None, :])

    rows = BP + BS
    rows_pad = -(-rows // 16) * 16
    c_all = jnp.concatenate([c_prompt, c_sample, jnp.zeros((rows_pad - rows, D), F32)], axis=0)
    mod = _mod_call(c_all, w_ada[0], b_ada[0][None, :])
    mod_p, mod_s = mod[0:BP], mod[BP:BP + BS]

    attend_p = functools.partial(_dsa_prompt_call, B=BP, L=LP, QB=256)
    y_p, big_p, h_p, k_p, v_p, kidx_p, s_p = _layer(
        x_prompt, mod_p, None, None, attend_p, wts,
        gdn_chunk=min(GDN_CHUNK, LP), gdn_hp=4, gdn_nseg=4, tm_proj=2048, tn_proj=512, act_dtype=BF16)
    h_last = h_p.reshape(BP, LP, D)[:, LP - 8:, :].reshape(BP * 8, D)
    aqkv_last = _proj_nt_call(h_last, wt3, lambda j: j * 512, A_CONV_CH, BP * 8, 512, F32, "in_proj_conv_tail")
    conv_p = aqkv_last.reshape(BP, 8, A_CONV_CH)[:, 8 - (CONV_W - 1):, :]

    ck = cache_k[0].reshape(n_pool * PAGE_SIZE * ATT_KV_HEADS, ATT_DH)
    cv = cache_v[0].reshape(n_pool * PAGE_SIZE * ATT_KV_HEADS, ATT_DH)
    ci = cache_kidx[0]
    attend_s = functools.partial(_dsa_sample_call, cache_k=ck, cache_v=cv, cache_kidx=ci, page_table=page_table,
                                 B=BS, T=LS)
    y_s, big_s, _, k_s, v_s, kidx_s, s_s = _layer(
        x_sample, mod_s, state_conv[0], state_gdn[0], attend_s, wts,
        gdn_chunk=min(GDN_CHUNK, LS), gdn_hp=16, gdn_nseg=1, tm_proj=256, tn_proj=512, act_dtype=F32)
    conv_s = big_s[:, 0:A_CONV_CH].reshape(BS, LS, A_CONV_CH)[:, LS - (CONV_W - 1):, :]

    kv_shape_p = (1, BP, LP, ATT_KV_HEADS, ATT_DH)
    kv_shape_s = (1, BS, LS, ATT_KV_HEADS, ATT_DH)
    return (y_p, y_s,
            k_p.reshape(kv_shape_p), v_p.reshape(kv_shape_p), kidx_p.reshape(1, BP, LP, IDX_DH),
            s_p[None], conv_p[None],
            k_s.reshape(kv_shape_s), v_s.reshape(kv_shape_s), kidx_s.reshape(1, BS, LS, IDX_DH),
            s_s[None], conv_s[None])
```

```python
import functools
import math

import jax
import jax.numpy as jnp
from jax import lax
from jax.experimental import pallas as pl
from jax.experimental.pallas import tpu as pltpu

F32 = jnp.float32
BF16 = jnp.bfloat16
I32 = jnp.int32

D_MODEL = 2048
PAGE_SIZE = 128
GDN_QK_HEADS = 16
GDN_V_HEADS = 32
GDN_DK = 128
GDN_DV = 128
CONV_W = 4
GDN_CHUNK = 64
ATT_HEADS = 16
ATT_KV_HEADS = 2
ATT_DH = 128
IDX_HEADS = 16
IDX_DH = 128
TOPK_MAX = 256
NORM_EPS = 1e-6
L2_EPS = 1e-6

A_QK = GDN_QK_HEADS * GDN_DK
A_V = GDN_V_HEADS * GDN_DV
A_CONV_CH = 2 * A_QK + A_V
B_Q = ATT_HEADS * ATT_DH
B_KV = ATT_KV_HEADS * ATT_DH
IDX_Q = IDX_HEADS * IDX_DH
SPLIT_SIZES = (A_CONV_CH, A_V, GDN_V_HEADS, GDN_V_HEADS, B_Q, B_KV, B_KV, B_Q, IDX_Q, IDX_DH, IDX_HEADS,
               D_MODEL, D_MODEL)
B_SEGS = (4, 7, 8, 11, 12)
OFF_AQKV = 0
OFF_AZ = OFF_AQKV + A_CONV_CH
N_A = OFF_AZ + A_V
OFF_BQ = 0
OFF_BZ = OFF_BQ + B_Q
OFF_IQ = OFF_BZ + B_Q
OFF_GA = OFF_IQ + IDX_Q
OFF_GB = OFF_GA + D_MODEL
N_B = OFF_GB + D_MODEL
LANE_BETA = 0
LANE_DECAY = GDN_V_HEADS
LANE_IW = 2 * GDN_V_HEADS

VMEM_LIMIT = 56 * 1024 * 1024
NEG_INF = float("-inf")
INT_MIN = -2 ** 31


def _cparams(sem):
    return pltpu.CompilerParams(dimension_semantics=sem, vmem_limit_bytes=VMEM_LIMIT)


def _dot(a, b):
    return jnp.dot(a, b, preferred_element_type=F32)


def _dot_nt(a, b):
    return lax.dot_general(a, b, (((1,), (1,)), ((), ())), preferred_element_type=F32)


def _dot_tn(a, b):
    return lax.dot_general(a, b, (((0,), (0,)), ((), ())), preferred_element_type=F32)


def _split2(x):
    hi = x.astype(BF16)
    lo = (x - hi.astype(F32)).astype(BF16)
    return hi, lo


def _split3(x):
    x1 = x.astype(BF16)
    r = x - x1.astype(F32)
    x2 = r.astype(BF16)
    x3 = (r - x2.astype(F32)).astype(BF16)
    return x1, x2, x3


def _dot_3pass(a, b):
    ah, al = _split2(a)
    bh, bl = _split2(b)
    return _dot(ah, bh) + _dot(al, bh) + _dot(ah, bl)


def _silu(x):
    h = 0.5 * x
    return h + h * jnp.tanh(h)


def _mod_kernel(c_ref, w_ref, b_ref, o_ref):
    o_ref[...] = _dot_3pass(_silu(c_ref[...]), w_ref[...]) + b_ref[...]


def _mod_call(c_pad, w_ada, b_ada):
    rows, d = c_pad.shape
    n = w_ada.shape[1]
    tn = 768
    return pl.pallas_call(
        _mod_kernel,
        grid=(n // tn,),
        in_specs=[pl.BlockSpec((rows, d), lambda j: (0, 0)),
                  pl.BlockSpec((d, tn), lambda j: (0, j)),
                  pl.BlockSpec((1, tn), lambda j: (0, j))],
        out_specs=pl.BlockSpec((rows, tn), lambda j: (0, j)),
        out_shape=jax.ShapeDtypeStruct((rows, n), F32),
        compiler_params=_cparams(("arbitrary",)),
        name="adaln_mod",
    )(c_pad, w_ada, b_ada)


def _norm_kernel(x_ref, g_ref, scale_ref, shift_ref, o_ref):
    x = x_ref[...]
    y = x * lax.rsqrt(jnp.mean(x * x, axis=-1, keepdims=True) + NORM_EPS) * g_ref[...]
    o_ref[...] = (y * (1.0 + scale_ref[...]) + shift_ref[...]).astype(o_ref.dtype)


def _norm_call(x2d, g_row, scale3, shift3, tm, tiles_per_group):
    m, d = x2d.shape
    r = scale3.shape[1]
    mod_spec = pl.BlockSpec((None, r, d), lambda i: (i // tiles_per_group, 0, 0))
    return pl.pallas_call(
        _norm_kernel,
        grid=(m // tm,),
        in_specs=[pl.BlockSpec((tm, d), lambda i: (i, 0)),
                  pl.BlockSpec((1, d), lambda i: (0, 0)),
                  mod_spec, mod_spec],
        out_specs=pl.BlockSpec((tm, d), lambda i: (i, 0)),
        out_shape=jax.ShapeDtypeStruct((m, d), BF16),
        compiler_params=_cparams(("parallel",)),
        name="norm_modulate",
    )(x2d, g_row, scale3, shift3)


SEG_OFFS = tuple(sum(SPLIT_SIZES[:i]) for i in range(len(SPLIT_SIZES) + 1))


def _proj_nt_kernel(h_ref, w_ref, o_ref):
    o_ref[...] = _dot_nt(h_ref[...], w_ref[...].astype(BF16)).astype(o_ref.dtype)


def _proj_nt_call(h, wt3, row_of_tile, n_cols, tm, tn, out_dtype, name):
    m, k = h.shape
    return pl.pallas_call(
        _proj_nt_kernel,
        grid=(m // tm, n_cols // tn),
        in_specs=[pl.BlockSpec((tm, k), lambda i, j: (i, 0)),
                  pl.BlockSpec((pl.Element(tn), pl.Element(k)),
                               lambda i, j: (pl.multiple_of(row_of_tile(j), 16), 0))],
        out_specs=pl.BlockSpec((tm, tn), lambda i, j: (i, j)),
        out_shape=jax.ShapeDtypeStruct((m, n_cols), out_dtype),
        compiler_params=_cparams(("parallel", "arbitrary")),
        name=name,
    )(h, wt3)


def _group_b_row(j, tn):
    per = B_Q // tn
    seg = j // per
    start = SEG_OFFS[B_SEGS[-1]]
    for s in range(len(B_SEGS) - 2, -1, -1):
        start = jnp.where(seg == s, SEG_OFFS[B_SEGS[s]], start)
    return start + (j % per) * tn


def _proj_tail_kernel(h_ref, wkv_ref, wik_ref, wg_ref, wiw_ref, k_ref, v_ref, kidx_ref, small_ref):
    h = h_ref[...]
    kv = _dot_nt(h, wkv_ref[...].astype(BF16))
    k_ref[...] = kv[:, 0:B_KV]
    v_ref[...] = kv[:, B_KV:]
    kidx_ref[...] = _dot_nt(h, wik_ref[...].astype(BF16))
    gate = _dot_nt(h, wg_ref[...].astype(BF16))
    iw = _dot_nt(h, wiw_ref[...].astype(BF16))
    lane = lax.broadcasted_iota(I32, gate.shape, 1)
    small_ref[...] = jnp.where(lane < LANE_IW, gate, jnp.where(lane < LANE_IW + IDX_HEADS, iw, 0.0))


def _proj_tail_call(h, wt3, tm):
    m, k = h.shape
    widths = (B_KV, B_KV, IDX_DH, 128)
    beta0, kv0, ik0, iw0 = SEG_OFFS[2], SEG_OFFS[5], SEG_OFFS[9], SEG_OFFS[10]
    assert SEG_OFFS[3] - beta0 == LANE_DECAY and SEG_OFFS[4] - beta0 == LANE_IW and SEG_OFFS[6] - kv0 == B_KV

    def rows(start, size):
        return pl.BlockSpec((pl.Element(size), pl.Element(k)), lambda i: (start, 0))

    return pl.pallas_call(
        _proj_tail_kernel,
        grid=(m // tm,),
        in_specs=[pl.BlockSpec((tm, k), lambda i: (i, 0)),
                  rows(kv0, 2 * B_KV), rows(ik0, IDX_DH), rows(beta0, 128), rows(iw0 - LANE_IW, 128)],
        out_specs=[pl.BlockSpec((tm, w), lambda i: (i, 0)) for w in widths],
        out_shape=[jax.ShapeDtypeStruct((m, w), F32) for w in widths],
        compiler_params=_cparams(("parallel",)),
        name="in_proj_tail",
    )(h, wt3, wt3, wt3, wt3)


def _gates_kernel(sm_ref, alog_ref, dtb_ref, o_ref, *, chunk):
    x = sm_ref[...]
    tm = x.shape[0]
    lane = lax.broadcasted_iota(I32, x.shape, 1)
    beta = jax.nn.sigmoid(x)
    z = x + dtb_ref[...]
    softplus = jnp.maximum(z, 0.0) + jnp.log1p(jnp.exp(-jnp.abs(z)))
    g = -jnp.exp(alog_ref[...]) * softplus
    g = jnp.where((lane >= LANE_DECAY) & (lane < LANE_DECAY + GDN_V_HEADS), g, 0.0)
    ri = lax.broadcasted_iota(I32, (tm, tm), 0)
    ci = lax.broadcasted_iota(I32, (tm, tm), 1)
    shift = int(math.log2(chunk))
    tri = jnp.where((ri >= ci) & ((ri >> shift) == (ci >> shift)), 1.0, 0.0).astype(BF16)
    g1, g2, g3 = _split3(g)
    gc = _dot(tri, g1) + _dot(tri, g2) + _dot(tri, g3)
    o_ref[...] = jnp.where(lane < GDN_V_HEADS, beta, gc)


def _gates_call(small, alog_row, dtb_row, chunk):
    m = small.shape[0]
    tm = 256
    return pl.pallas_call(
        functools.partial(_gates_kernel, chunk=chunk),
        grid=(m // tm,),
        in_specs=[pl.BlockSpec((tm, 128), lambda i: (i, 0)),
                  pl.BlockSpec((1, 128), lambda i: (0, 0)),
                  pl.BlockSpec((1, 128), lambda i: (0, 0))],
        out_specs=pl.BlockSpec((tm, 128), lambda i: (i, 0)),
        out_shape=jax.ShapeDtypeStruct((m, 128), F32),
        compiler_params=_cparams(("parallel",)),
        name="gdn_gates",
    )(small, alog_row, dtb_row)


def _unrolled_loop(n, unroll, fn):
    unroll = min(unroll, n)
    assert n % unroll == 0

    def body(t, carry):
        for u in range(unroll):
            fn(t * unroll + u)
        return carry

    lax.fori_loop(0, n // unroll, body, 0)


def _gdn_kernel(*refs, L, C, HP, NSEG, has_state):
    ins = 13 if has_state else 12
    (q_ref, k_ref, v_ref, z_ref, bq_ref, bk_ref, bv_ref, wq_ref, wk_ref, wv_ref, gates_ref, gn_ref) = refs[:12]
    s0_ref = refs[12] if has_state else None
    (y_ref, sout_ref, xq_s, xk_s, xv_s, beta_s, g_s, st_s, a_s, p_s, rhs_s, kt_s, qkd_s, qd_s,
     m_s, n_s, qe_s, o_s) = refs[ins:]
    nh = 2 * HP
    nc = L // C
    hq0 = pl.program_id(1) * HP

    def first_segment():
        xq_s[0:8, :] = bq_ref[...]
        xk_s[0:8, :] = bk_ref[...]
        xv_s[0:8, :] = bv_ref[...]
        if has_state:
            st_s[...] = s0_ref[...]
        else:
            st_s[...] = jnp.zeros(st_s.shape, F32)

    def next_segment():
        xq_s[0:8, :] = xq_s[L:L + 8, :]
        xk_s[0:8, :] = xk_s[L:L + 8, :]
        xv_s[0:8, :] = xv_s[L:L + 8, :]

    if NSEG == 1:
        first_segment()
    else:
        pl.when(pl.program_id(2) == 0)(first_segment)
        pl.when(pl.program_id(2) > 0)(next_segment)
    xq_s[8:L + 8, :] = q_ref[...].astype(F32)
    xk_s[8:L + 8, :] = k_ref[...].astype(F32)
    xv_s[8:L + 8, :] = v_ref[...].astype(F32)

    gates = gates_ref[...]
    lane = lax.broadcasted_iota(I32, gates.shape, 1)
    for idx in range(nh):
        hv = 2 * hq0 + idx
        bcol = jnp.sum(jnp.where(lane == hv + LANE_BETA, gates, 0.0), axis=-1, keepdims=True)
        gcol = jnp.sum(jnp.where(lane == hv + LANE_DECAY, gates, 0.0), axis=-1, keepdims=True)
        beta_s[idx] = jnp.broadcast_to(bcol, (L, 128))
        g_s[idx] = jnp.broadcast_to(gcol, (L, 128))

    ii = lax.broadcasted_iota(I32, (C, C), 0)
    jj = lax.broadcasted_iota(I32, (C, C), 1)
    gn = gn_ref[...]

    def conv(x_s, r0, cols, w):
        win = x_s[pl.ds(r0, C + 8), cols]
        acc = (win[5:5 + C] * w[0:1] + win[6:6 + C] * w[1:2] + win[7:7 + C] * w[2:3] + win[8:8 + C] * w[3:4])
        return _silu(acc)

    def l2n(x):
        return x * lax.rsqrt(jnp.sum(x * x, axis=-1, keepdims=True) + L2_EPS)

    def lanes(i, width):
        if isinstance(i, int):
            return slice(i * width, (i + 1) * width)
        return pl.ds(pl.multiple_of(i * width, width), width)

    def pair_terms(j):
        c, p = (j, 0) if HP == 1 else (j // HP, j % HP)
        r0 = pl.multiple_of(c * C, C)
        rows = pl.ds(r0, C)
        lq, lv = lanes(p, 128), lanes(p, 256)
        q = l2n(conv(xq_s, r0, lq, wq_ref[:, lq])) * (GDN_DK ** -0.5)
        k = l2n(conv(xk_s, r0, lq, wk_ref[:, lq]))
        vc = conv(xv_s, r0, lv, wv_ref[:, lv])
        k16 = k.astype(BF16)
        kk = _dot_nt(k16, k16)
        qk = _dot_nt(q.astype(BF16), k16)
        for r in range(2):
            idx = 2 * p + r
            e = c * nh + idx
            gb = g_s[idx, rows, :]
            bb = beta_s[idx, rows, :]
            e_g = jnp.exp(gb)
            e_tail = jnp.exp(gb[C - 1:C, :] - gb)
            if C < 128:
                gpad = jnp.concatenate([gb, jnp.zeros((128 - C, 128), F32)], axis=0)
            else:
                gpad = gb
            dexp = gb[:, 0:C] - gpad.T[0:C, 0:C]
            dm = jnp.exp(jnp.where(ii >= jj, dexp, NEG_INF))
            a = jnp.where(ii > jj, kk * dm, 0.0) * bb[:, 0:C]
            a_s[e] = a.astype(a_s.dtype)
            p_s[e] = jnp.where(ii == jj, 1.0, 0.0) - a
            rhs_s[e] = jnp.concatenate([vc[:, r * 128:(r + 1) * 128] * bb, k * bb * e_g], axis=1).astype(rhs_s.dtype)
            kt_s[e] = (k * e_tail).astype(kt_s.dtype)
            qkd_s[e] = (qk * dm).astype(qkd_s.dtype)
            qd_s[e] = q * e_g

    def square(e):
        ak = a_s[e].astype(BF16)
        a_s[e] = _dot(ak, ak).astype(a_s.dtype)

    def inverse_stage(e, last):
        ak = a_s[e].astype(BF16)
        pk = p_s[e]
        p_s[e] = pk + _dot(pk.astype(BF16), ak)
        if not last:
            a_s[e] = _dot(ak, ak).astype(a_s.dtype)

    def solve(e):
        rhs_s[e] = _dot(p_s[e].astype(BF16), rhs_s[e].astype(BF16)).astype(rhs_s.dtype)

    def state_terms(e):
        sol16 = rhs_s[e].astype(BF16)
        mn = _dot_tn(kt_s[e].astype(BF16), sol16)
        qo = _dot(qkd_s[e].astype(BF16), sol16)
        n_s[e] = mn[:, 0:GDN_DV]
        m_s[e] = mn[:, GDN_DV:].astype(m_s.dtype)
        o_s[e] = qo[:, 0:GDN_DV]
        qe_s[e] = (qd_s[e] - qo[:, GDN_DV:]).astype(qe_s.dtype)

    def head_step(c, idx):
        e = c * nh + idx
        r0 = pl.multiple_of(c * C, C)
        rows = pl.ds(r0, C)
        s = st_s[idx]
        s16 = s.astype(BF16)
        o = _dot(qe_s[e].astype(BF16), s16) + o_s[e]
        e_last = jnp.exp(g_s[idx, pl.ds(r0 + C - 1, 1), :])
        st_s[idx] = s * e_last - _dot(m_s[e].astype(BF16), s16) + n_s[e]
        on = o * lax.rsqrt(jnp.mean(o * o, axis=-1, keepdims=True) + NORM_EPS) * gn
        zz = z_ref[rows, lanes(idx, 128)].astype(F32)
        y_ref[rows, lanes(idx, 128)] = (on * _silu(zz)).astype(y_ref.dtype)

    ne = nc * nh
    stages = int(math.log2(C)) - 1
    _unrolled_loop(nc * HP, 2 if C >= 64 else 4, pair_terms)
    _unrolled_loop(ne, 32, square)
    for st in range(stages):
        _unrolled_loop(ne, 32, functools.partial(inverse_stage, last=(st == stages - 1)))
    _unrolled_loop(ne, 16, solve)
    _unrolled_loop(ne, 16, state_terms)

    def chunk_step(c, carry):
        if nh <= 4:
            for idx in range(nh):
                head_step(c, idx)
        else:
            _unrolled_loop(nh, 16, functools.partial(head_step, c))
        return carry

    lax.fori_loop(0, nc, chunk_step, 0)
    sout_ref[...] = st_s[...]


def _gdn_call(big, convbuf8, conv_w, gates, gn_row, s0, *, B, L, C, HP, NSEG, y_dtype):
    has_state = s0 is not None
    ls = L // NSEG
    ne = (ls // C) * 2 * HP
    half = BF16 if C % 16 == 0 else F32
    nq = GDN_QK_HEADS // HP
    wq, wv = HP * 128, HP * 256
    kq0 = A_QK // wq
    v0 = (2 * A_QK) // wv
    z0 = OFF_AZ // wv
    in_specs = [
        pl.BlockSpec((ls, wq), lambda b, h, s: (b * NSEG + s, h)),
        pl.BlockSpec((ls, wq), lambda b, h, s: (b * NSEG + s, kq0 + h)),
        pl.BlockSpec((ls, wv), lambda b, h, s: (b * NSEG + s, v0 + h)),
        pl.BlockSpec((ls, wv), lambda b, h, s: (b * NSEG + s, z0 + h)),
        pl.BlockSpec((None, 8, wq), lambda b, h, s: (b, 0, h)),
        pl.BlockSpec((None, 8, wq), lambda b, h, s: (b, 0, kq0 + h)),
        pl.BlockSpec((None, 8, wv), lambda b, h, s: (b, 0, v0 + h)),
        pl.BlockSpec((CONV_W, wq), lambda b, h, s: (0, h)),
        pl.BlockSpec((CONV_W, wq), lambda b, h, s: (0, kq0 + h)),
        pl.BlockSpec((CONV_W, wv), lambda b, h, s: (0, v0 + h)),
        pl.BlockSpec((ls, 128), lambda b, h, s: (b * NSEG + s, 0)),
        pl.BlockSpec((1, 128), lambda b, h, s: (0, 0)),
    ]
    args = [big, big, big, big, convbuf8, convbuf8, convbuf8, conv_w, conv_w, conv_w, gates, gn_row]
    if has_state:
        in_specs.append(pl.BlockSpec((None, 2 * HP, GDN_DK, GDN_DV), lambda b, h, s: (b, h, 0, 0)))
        args.append(s0)
    return pl.pallas_call(
        functools.partial(_gdn_kernel, L=ls, C=C, HP=HP, NSEG=NSEG, has_state=has_state),
        grid=(B, nq, NSEG),
        in_specs=in_specs,
        out_specs=[pl.BlockSpec((ls, wv), lambda b, h, s: (b * NSEG + s, h)),
                   pl.BlockSpec((None, 2 * HP, GDN_DK, GDN_DV), lambda b, h, s: (b, h, 0, 0))],
        out_shape=[jax.ShapeDtypeStruct((B * L, A_V), y_dtype),
                   jax.ShapeDtypeStruct((B, GDN_V_HEADS, GDN_DK, GDN_DV), F32)],
        scratch_shapes=[pltpu.VMEM((ls + 8, wq), F32), pltpu.VMEM((ls + 8, wq), F32), pltpu.VMEM((ls + 8, wv), F32),
                        pltpu.VMEM((2 * HP, ls, 128), F32), pltpu.VMEM((2 * HP, ls, 128), F32),
                        pltpu.VMEM((2 * HP, GDN_DK, GDN_DV), F32),
                        pltpu.VMEM((ne, C, C), half), pltpu.VMEM((ne, C, C), F32),
                        pltpu.VMEM((ne, C, GDN_DV + GDN_DK), half), pltpu.VMEM((ne, C, GDN_DK), half),
                        pltpu.VMEM((ne, C, C), half), pltpu.VMEM((ne, C, GDN_DK), F32),
                        pltpu.VMEM((ne, GDN_DK, GDN_DV), BF16), pltpu.VMEM((ne, GDN_DK, GDN_DV), F32),
                        pltpu.VMEM((ne, C, GDN_DK), half), pltpu.VMEM((ne, C, GDN_DV), F32)],
        compiler_params=_cparams(("parallel", "arbitrary", "arbitrary")),
        name="gdn_prompt" if not has_state else "gdn_sample",
    )(*args)


KEY_NEG_INF = -2 ** 31 + 0x7FFFFF


def _key_to_float(key):
    bits = key ^ ((key >> 31) & jnp.int32(0x7FFFFFFF))
    return pltpu.bitcast(bits, F32)


def _count_true(mask):
    ones = jnp.where(mask, 1.0, 0.0)
    parts = [ones[:, i * 128:(i + 1) * 128] for i in range(mask.shape[1] // 128)]
    while len(parts) > 1:
        parts = [a + b for a, b in zip(parts[0::2], parts[1::2])] + ([parts[-1]] if len(parts) % 2 else [])
    return jnp.sum(parts[0], axis=-1, keepdims=True)


def _selection_bias(score_ref, bias_ref, qpos0, rows, width, topk, bits=1):
    def step(it, t):
        best = t
        for digit in range(1, 2 ** bits):
            cand = t + jnp.left_shift(jnp.int32(digit), 32 - bits * (it + 1))
            cnt = _count_true(score_ref[:, 0:width] >= _key_to_float(cand))
            best = jnp.where((cnt >= float(topk)) | (cand <= KEY_NEG_INF), cand, best)
        return best

    thr = _key_to_float(lax.fori_loop(0, 32 // bits, step, jnp.full((rows, 1), INT_MIN, I32)))
    score = score_ref[:, 0:width]
    causal = (lax.broadcasted_iota(I32, (rows, width), 1)
              <= qpos0 + lax.broadcasted_iota(I32, (rows, width), 0))
    sel = (score >= thr) & causal
    bias_ref[:, 0:width] = jnp.where(sel, 0.0, NEG_INF)
    n_sel = _count_true(sel)
    has_ties = jnp.max(n_sel) > float(topk)

    @pl.when(has_ties)
    def _():
        need = float(topk) - _count_true(score > thr)
        eq16 = jnp.where(score == thr, 1.0, 0.0).astype(BF16)

        def tile(j, carry):
            c0 = pl.multiple_of(j * 128, 128)
            ri = lax.broadcasted_iota(I32, (width, 128), 0)
            ci = lax.broadcasted_iota(I32, (width, 128), 1) + c0
            upper = jnp.where(ri <= ci, 1.0, 0.0).astype(BF16)
            prefix = _dot(eq16, upper)
            s_t = score_ref[:, pl.ds(c0, 128)]
            keep = (s_t > thr) | ((s_t == thr) & (prefix <= need))
            old = bias_ref[:, pl.ds(c0, 128)]
            bias_ref[:, pl.ds(c0, 128)] = jnp.where(keep, old, NEG_INF)
            return carry

        lax.fori_loop(0, width // 128, tile, 0)


SOFTMAX_LOG2_SCALE = (ATT_DH ** -0.5) * math.log2(math.e)


def _dsa_prompt_kernel(qi_ref, wi_ref, kidx_ref, q_ref, k_ref, v_ref, z_ref, o_ref,
                       score_s, bias_s, kidx16_s, k16_s, vext_s, m_s, acc_s, *, QB, SUB, KT, L, topk):
    i = pl.program_id(1)
    q0 = i * QB
    nt = (q0 + QB - 1) // KT + 1
    groups = ATT_HEADS // ATT_KV_HEADS

    @pl.when(i == 0)
    def _():
        kidx16_s[...] = kidx_ref[...].astype(BF16)
        k16_s[...] = k_ref[...].astype(BF16)
        for n in range(ATT_KV_HEADS):
            vext_s[n, :, 0:ATT_DH] = v_ref[:, n * ATT_DH:(n + 1) * ATT_DH].astype(BF16)
            vext_s[n, :, ATT_DH:] = jnp.ones((L, ATT_DH), BF16)

    rowpos = q0 + lax.broadcasted_iota(I32, (SUB, KT), 0)
    col = lax.broadcasted_iota(I32, (SUB, KT), 1)

    def tile_rows(t):
        return pl.ds(pl.multiple_of(t * KT, KT), KT)

    subs = [slice(r, r + SUB) for r in range(0, QB, SUB)]

    def index_tile(t, carry):
        kid = kidx16_s[tile_rows(t), :]
        for rs in subs:
            wi = wi_ref[rs, :] * ((IDX_DH ** -0.5) * (IDX_HEADS ** -0.5))
            acc = jnp.zeros((SUB, KT), F32)
            for h in range(IDX_HEADS):
                lg = _dot_nt(qi_ref[rs, h * IDX_DH:(h + 1) * IDX_DH], kid)
                acc = acc + jnp.maximum(lg, 0.0) * wi[:, LANE_IW + h:LANE_IW + h + 1]
            causal = col + t * KT <= rowpos + rs.start
            score_s[rs, tile_rows(t)] = jnp.where(causal, acc, NEG_INF)
        return carry

    lax.fori_loop(0, nt, index_tile, 0)
    for w in range(1, L // KT + 1):
        @pl.when(nt == w)
        def _():
            _selection_bias(score_s, bias_s, q0, QB, w * KT, topk)

    m_s[...] = jnp.full(m_s.shape, float(jnp.finfo(F32).min), F32)
    acc_s[...] = jnp.zeros(acc_s.shape, F32)

    def attend_tile(t, carry):
        for n in range(ATT_KV_HEADS):
            kn = k16_s[tile_rows(t), n * ATT_DH:(n + 1) * ATT_DH]
            vn = vext_s[n, tile_rows(t), :]
            for g in range(groups):
                h = n * groups + g
                for rs in subs:
                    sb = _dot_nt(q_ref[rs, h * ATT_DH:(h + 1) * ATT_DH], kn) + bias_s[rs, tile_rows(t)]
                    m_old = m_s[h, rs, :]
                    m_new = jnp.maximum(m_old, jnp.max(sb, axis=-1, keepdims=True))
                    p = jnp.exp2((sb - jnp.concatenate([m_new] * (KT // 128), axis=1)) * SOFTMAX_LOG2_SCALE)
                    alpha = jnp.exp2((m_old - m_new) * SOFTMAX_LOG2_SCALE)
                    m_s[h, rs, :] = m_new
                    acc_s[h, rs, :] = (acc_s[h, rs, :] * jnp.concatenate([alpha, alpha], axis=1)
                                       + _dot(p.astype(BF16), vn))
        return carry

    lax.fori_loop(0, nt, attend_tile, 0)
    for h in range(ATT_HEADS):
        cols = slice(h * ATT_DH, (h + 1) * ATT_DH)
        zz = z_ref[:, cols].astype(F32)
        acc = acc_s[h]
        o_ref[:, cols] = (acc[:, 0:ATT_DH] / acc[:, ATT_DH:] * _silu(zz)).astype(o_ref.dtype)


def _dsa_prompt_call(big, small, kidx, k, v, *, B, L, QB):
    topk = min(TOPK_MAX, L // 4)
    nqb = L // QB
    iq0 = OFF_IQ // B_Q
    bq0 = OFF_BQ // B_Q
    bz0 = OFF_BZ // B_Q
    return pl.pallas_call(
        functools.partial(_dsa_prompt_kernel, QB=QB, SUB=128, KT=256, L=L, topk=topk),
        grid=(B, nqb),
        in_specs=[pl.BlockSpec((QB, IDX_Q), lambda b, i: (b * nqb + i, iq0)),
                  pl.BlockSpec((QB, 128), lambda b, i: (b * nqb + i, 0)),
                  pl.BlockSpec((L, IDX_DH), lambda b, i: (b, 0)),
                  pl.BlockSpec((QB, B_Q), lambda b, i: (b * nqb + i, bq0)),
                  pl.BlockSpec((L, B_KV), lambda b, i: (b, 0)),
                  pl.BlockSpec((L, B_KV), lambda b, i: (b, 0)),
                  pl.BlockSpec((QB, B_Q), lambda b, i: (b * nqb + i, bz0))],
        out_specs=pl.BlockSpec((QB, B_Q), lambda b, i: (b * nqb + i, 0)),
        out_shape=jax.ShapeDtypeStruct((B * L, B_Q), BF16),
        scratch_shapes=[pltpu.VMEM((QB, L), F32), pltpu.VMEM((QB, L), F32),
                        pltpu.VMEM((L, IDX_DH), BF16), pltpu.VMEM((L, B_KV), BF16),
                        pltpu.VMEM((ATT_KV_HEADS, L, 2 * ATT_DH), BF16),
                        pltpu.VMEM((ATT_HEADS, QB, 128), F32), pltpu.VMEM((ATT_HEADS, QB, 2 * ATT_DH), F32)],
        compiler_params=_cparams(("parallel", "arbitrary")),
        name="dsa_prompt",
    )(big, small, kidx, big, k, v, big)


def _dsa_sample_kernel(pt_ref, qi_ref, wi_ref, kin_ref, q_ref, kn_ref, vn_ref, z_ref, ck_hbm, cv_hbm, ci_hbm,
                       o_ref, kbuf, vbuf, ibuf, sems, score_s, bias_s, *, NP, T, topk):
    b = pl.program_id(0)
    nb = pl.num_programs(0)
    slot = b % 2
    past = NP * PAGE_SIZE
    width = past + 128

    kvp = ATT_KV_HEADS * PAGE_SIZE

    def page_copies(bb, sl, p):
        page = pt_ref[bb * NP + p]
        rows = pl.ds(pl.multiple_of(p * PAGE_SIZE, PAGE_SIZE), PAGE_SIZE)
        src2 = pl.ds(pl.multiple_of(page * kvp, kvp), kvp)
        dst2 = pl.ds(pl.multiple_of(p * kvp, kvp), kvp)
        return (pltpu.make_async_copy(ck_hbm.at[src2], kbuf.at[sl, dst2], sems.at[0, sl]),
                pltpu.make_async_copy(cv_hbm.at[src2], vbuf.at[sl, dst2], sems.at[1, sl]),
                pltpu.make_async_copy(ci_hbm.at[page], ibuf.at[sl, rows], sems.at[2, sl]))

    def start_fetch(bb, sl):
        def body(p, carry):
            for cp in page_copies(bb, sl, p):
                cp.start()
            return carry
        lax.fori_loop(0, NP, body, 0)

    def wait_fetch(bb, sl):
        def body(p, carry):
            for cp in page_copies(bb, sl, p):
                cp.wait()
            return carry
        lax.fori_loop(0, NP, body, 0)

    @pl.when(b == 0)
    def _():
        for sl in range(2):
            kbuf[sl, ATT_KV_HEADS * past:, :] = jnp.zeros((ATT_KV_HEADS * 128, ATT_DH), F32)
            vbuf[sl, ATT_KV_HEADS * past:, :] = jnp.zeros((ATT_KV_HEADS * 128, ATT_DH), F32)
            ibuf[sl, past:, :] = jnp.zeros((128, IDX_DH), F32)
        start_fetch(0, 0)

    @pl.when(b + 1 < nb)
    def _():
        start_fetch(b + 1, 1 - slot)

    for n in range(ATT_KV_HEADS):
        new_rows = pl.ds(ATT_KV_HEADS * past + n, T, stride=ATT_KV_HEADS)
        kbuf[slot, new_rows, :] = kn_ref[:, n * ATT_DH:(n + 1) * ATT_DH]
        vbuf[slot, new_rows, :] = vn_ref[:, n * ATT_DH:(n + 1) * ATT_DH]
    ibuf[slot, past:past + T, :] = kin_ref[...]
    wait_fetch(b, slot)

    qi = qi_ref[...]
    qis = jnp.concatenate([qi[:, h * IDX_DH:(h + 1) * IDX_DH] for h in range(IDX_HEADS)], axis=0).astype(BF16)
    wi = wi_ref[...] * ((IDX_DH ** -0.5) * (IDX_HEADS ** -0.5))
    lg = _dot_nt(qis, ibuf[slot].astype(BF16))
    score = jnp.zeros((T, width), F32)
    for h in range(IDX_HEADS):
        score = score + jnp.maximum(lg[h * T:(h + 1) * T, :], 0.0) * wi[:, LANE_IW + h:LANE_IW + h + 1]
    qpos = past + lax.broadcasted_iota(I32, (T, width), 0)
    kpos = lax.broadcasted_iota(I32, (T, width), 1)
    causal = kpos <= qpos
    score_s[...] = jnp.where(causal, score, NEG_INF)
    _selection_bias(score_s, bias_s, past, T, width, topk, bits=2)

    groups = ATT_HEADS // ATT_KV_HEADS
    q = q_ref[...]
    bias = jnp.concatenate([bias_s[...]] * groups, axis=0)
    for n in range(ATT_KV_HEADS):
        head_rows = pl.ds(n, width, stride=ATT_KV_HEADS)
        kn = kbuf[slot, head_rows, :].astype(BF16)
        vn = vbuf[slot, head_rows, :].astype(BF16)
        qs = jnp.concatenate([q[:, (n * groups + g) * ATT_DH:(n * groups + g + 1) * ATT_DH] for g in range(groups)],
                             axis=0).astype(BF16)
        s = _dot_nt(qs, kn) * (ATT_DH ** -0.5) + bias
        m = jnp.max(s, axis=-1, keepdims=True)
        p = jnp.exp(s - m)
        denom = jnp.sum(p, axis=-1, keepdims=True)
        o = _dot(p.astype(BF16), vn) / denom
        for g in range(groups):
            h = n * groups + g
            cols = slice(h * ATT_DH, (h + 1) * ATT_DH)
            zz = z_ref[:, cols]
            o_ref[:, cols] = o[g * T:(g + 1) * T, :] * _silu(zz)


def _dsa_sample_call(big, small, kidx, k, v, cache_k, cache_v, cache_kidx, page_table, *, B, T):
    npages = page_table.shape[1]
    past = npages * PAGE_SIZE
    topk = min(TOPK_MAX, (past + T) // 4)
    width = past + 128
    iq0 = OFF_IQ // B_Q
    bq0 = OFF_BQ // B_Q
    bz0 = OFF_BZ // B_Q
    grid_spec = pltpu.PrefetchScalarGridSpec(
        num_scalar_prefetch=1,
        grid=(B,),
        in_specs=[pl.BlockSpec((T, IDX_Q), lambda b, pt: (b, iq0)),
                  pl.BlockSpec((T, 128), lambda b, pt: (b, 0)),
                  pl.BlockSpec((T, IDX_DH), lambda b, pt: (b, 0)),
                  pl.BlockSpec((T, B_Q), lambda b, pt: (b, bq0)),
                  pl.BlockSpec((T, B_KV), lambda b, pt: (b, 0)),
                  pl.BlockSpec((T, B_KV), lambda b, pt: (b, 0)),
                  pl.BlockSpec((T, B_Q), lambda b, pt: (b, bz0)),
                  pl.BlockSpec(memory_space=pl.ANY),
                  pl.BlockSpec(memory_space=pl.ANY),
                  pl.BlockSpec(memory_space=pl.ANY)],
        out_specs=pl.BlockSpec((T, B_Q), lambda b, pt: (b, 0)),
        scratch_shapes=[pltpu.VMEM((2, ATT_KV_HEADS * width, ATT_DH), F32),
                        pltpu.VMEM((2, ATT_KV_HEADS * width, ATT_DH), F32),
                        pltpu.VMEM((2, width, IDX_DH), F32), pltpu.SemaphoreType.DMA((3, 2)),
                        pltpu.VMEM((T, width), F32), pltpu.VMEM((T, width), F32)])
    return pl.pallas_call(
        functools.partial(_dsa_sample_kernel, NP=npages, T=T, topk=topk),
        grid_spec=grid_spec,
        out_shape=jax.ShapeDtypeStruct((B * T, B_Q), F32),
        compiler_params=_cparams(("arbitrary",)),
        name="dsa_sample",
    )(page_table.reshape(-1), big, small, kidx, big, k, v, big, cache_k, cache_v, cache_kidx)


def _merge_kernel(ya_ref, yb_ref, ga_ref, gb_ref, wpa_ref, wpb_ref, o_ref):
    pa = _dot(ya_ref[...].astype(BF16), wpa_ref[...])
    pb = _dot(yb_ref[...].astype(BF16), wpb_ref[...])
    o = jax.nn.sigmoid(ga_ref[...].astype(F32)) * pa + jax.nn.sigmoid(gb_ref[...].astype(F32)) * pb
    o_ref[...] = o.astype(o_ref.dtype)


def _merge_call(ya, yb, big, wpa, wpb, tm, tn):
    m = ya.shape[0]
    n = wpa.shape[1]
    ga0 = OFF_GA // tn
    gb0 = OFF_GB // tn
    return pl.pallas_call(
        _merge_kernel,
        grid=(m // tm, n // tn),
        in_specs=[pl.BlockSpec((tm, A_V), lambda i, j: (i, 0)),
                  pl.BlockSpec((tm, B_Q), lambda i, j: (i, 0)),
                  pl.BlockSpec((tm, tn), lambda i, j: (i, ga0 + j)),
                  pl.BlockSpec((tm, tn), lambda i, j: (i, gb0 + j)),
                  pl.BlockSpec((A_V, tn), lambda i, j: (0, j)),
                  pl.BlockSpec((B_Q, tn), lambda i, j: (0, j))],
        out_specs=pl.BlockSpec((tm, tn), lambda i, j: (i, j)),
        out_shape=jax.ShapeDtypeStruct((m, n), BF16),
        compiler_params=_cparams(("parallel", "arbitrary")),
        name="merge_proj",
    )(ya, yb, big, big, wpa, wpb)


def _final_kernel(m_ref, w_ref, x_ref, gate_ref, g_ref, o_ref):
    t = _dot(m_ref[...], w_ref[...])
    y = t * lax.rsqrt(jnp.mean(t * t, axis=-1, keepdims=True) + NORM_EPS) * g_ref[...]
    o_ref[...] = x_ref[...] + gate_ref[...] * y


def _final_call(merged, w_out, x2d, gate3, g_row, tm, tiles_per_group):
    m, d = x2d.shape
    r = gate3.shape[1]
    return pl.pallas_call(
        _final_kernel,
        grid=(m // tm,),
        in_specs=[pl.BlockSpec((tm, d), lambda i: (i, 0)),
                  pl.BlockSpec((d, d), lambda i: (0, 0)),
                  pl.BlockSpec((tm, d), lambda i: (i, 0)),
                  pl.BlockSpec((None, r, d), lambda i: (i // tiles_per_group, 0, 0)),
                  pl.BlockSpec((1, d), lambda i: (0, 0))],
        out_specs=pl.BlockSpec((tm, d), lambda i: (i, 0)),
        out_shape=jax.ShapeDtypeStruct((m, d), F32),
        compiler_params=_cparams(("parallel",)),
        name="out_proj",
    )(merged, w_out, x2d, gate3, g_row)


def _cast_kernel(x_ref, o_ref):
    o_ref[...] = x_ref[...].astype(o_ref.dtype)


def _cast_bf16_call(w3):
    _, r, n = w3.shape
    tr = 256
    return pl.pallas_call(
        _cast_kernel,
        grid=(r // tr,),
        in_specs=[pl.BlockSpec((None, tr, n), lambda i: (0, i, 0))],
        out_specs=pl.BlockSpec((tr, n), lambda i: (i, 0)),
        out_shape=jax.ShapeDtypeStruct((r, n), BF16),
        compiler_params=_cparams(("parallel",)),
        name="w_cast",
    )(w3)


def _lane_row(vec, lane0):
    return jnp.zeros((1, 128), F32).at[0, lane0:lane0 + vec.shape[0]].set(vec.astype(F32))


def _layer(x, mod, conv_hist, s0, attend, wts, *, gdn_chunk, gdn_hp, gdn_nseg, tm_proj, tn_proj, act_dtype):
    (pre_g, wt3, conv_w, alog_row, dtb_row, gn_row, wpa, wpb, wout, post_g) = wts
    B, L, D = x.shape
    M = B * L
    x2d = x.reshape(M, D)
    shift, scale, gate = mod[:, 0:D], mod[:, D:2 * D], mod[:, 2 * D:3 * D]
    if L >= 256:
        tm_n = 256
        per = L // tm_n
        scale3, shift3, gate3 = scale[:, None, :], shift[:, None, :], gate[:, None, :]
    else:
        tm_n = M
        per = 1
        scale3, shift3, gate3 = (jnp.repeat(t, L, axis=0)[None] for t in (scale, shift, gate))
    h = _norm_call(x2d, pre_g, scale3, shift3, tm_n, per)
    tm = min(tm_proj, M)
    big_a = _proj_nt_call(h, wt3, lambda j: j * tn_proj, N_A, tm, tn_proj, act_dtype, "in_proj_a")
    big_b = _proj_nt_call(h, wt3, functools.partial(_group_b_row, tn=tn_proj), N_B, tm, tn_proj, act_dtype,
                          "in_proj_b")
    k, v, kidx, small = _proj_tail_call(h, wt3, min(tm, 1024))

    gates = _gates_call(small, alog_row, dtb_row, gdn_chunk)
    if conv_hist is None:
        convbuf8 = jnp.zeros((B, 8, A_CONV_CH), F32)
    else:
        convbuf8 = jnp.concatenate([jnp.zeros((B, 5, A_CONV_CH), F32), conv_hist.astype(F32)], axis=1)
    y_a, s_new = _gdn_call(big_a, convbuf8, conv_w, gates, gn_row, s0, B=B, L=L, C=gdn_chunk, HP=gdn_hp,
                           NSEG=gdn_nseg, y_dtype=act_dtype)
    y_b = attend(big_b, small, kidx, k, v)
    tm_m = min(1024, M)
    merged = _merge_call(y_a, y_b, big_b, wpa, wpb, tm_m, 512)
    tm_f = min(512, M)
    per_f = (L // tm_f) if L >= 256 else 1
    y = _final_call(merged, wout, x2d, gate3, post_g, tm_f, per_f)
    return y.reshape(B, L, D), big_a, h, k, v, kidx, s_new


def kernel(x_prompt, x_sample, c_prompt, c_sample, cache_k, cache_v, cache_kidx, state_gdn, state_conv,
           page_table, w_ada, b_ada, pre_norm_g, w_in, conv_w, a_log, dt_bias, gdn_norm_g, w_pa, w_pb, w_out,
           post_norm_g):
    depth = w_in.shape[0]
    assert depth == 1, "single-layer trunk"
    BP, LP, D = x_prompt.shape
    BS, LS, _ = x_sample.shape
    n_pool = cache_k.shape[1]

    wt3 = jnp.swapaxes(w_in, 1, 2)[0]
    wts =(pre_norm_g[0][None, :], wt3, conv_w[0],
           _lane_row(a_log[0], LANE_DECAY), _lane_row(dt_bias[0], LANE_DECAY), gdn_norm_g[0][None, :],
           _cast_bf16_call(w_pa), _cast_bf16_call(w_pb), _cast_bf16_call(w_out), post_norm_g[0][None, :])

    rows = BP + BS
    rows_pad = -(-rows // 16) * 16
    c_all = jnp.concatenate([c_prompt, c_sample, jnp.zeros((rows_pad - rows, D), F32)], axis=0)
    mod = _mod_call(c_all, w_ada[0], b_ada[0][None, :])
    mod_p, mod_s = mod[0:BP], mod[BP:BP + BS]

    attend_p = functools.partial(_dsa_prompt_call, B=BP, L=LP, QB=256)
    y_p, big_p, h_p, k_p, v_p, kidx_p, s_p = _layer(
        x_prompt, mod_p, None, None, attend_p, wts,
        gdn_chunk=min(GDN_CHUNK, LP), gdn_hp=8, gdn_nseg=8, tm_proj=2048, tn_proj=512, act_dtype=BF16)
    h_last = h_p.reshape(BP, LP, D)[:, LP - 8:, :].reshape(BP * 8, D)
    aqkv_last = _proj_nt_call(h_last, wt3, lambda j: j * 512, A_CONV_CH, BP * 8, 512, F32, "in_proj_conv_tail")
    conv_p = aqkv_last.reshape(BP, 8, A_CONV_CH)[:, 8 - (CONV_W - 1):, :]

    ck = cache_k[0].reshape(n_pool * PAGE_SIZE * ATT_KV_HEADS, ATT_DH)
    cv = cache_v[0].reshape(n_pool * PAGE_SIZE * ATT_KV_HEADS, ATT_DH)
    ci = cache_kidx[0]
    attend_s = functools.partial(_dsa_sample_call, cache_k=ck, cache_v=cv, cache_kidx=ci, page_table=page_table,
                                 B=BS, T=LS)
    y_s, big_s, _, k_s, v_s, kidx_s, s_s = _layer(
        x_sample, mod_s, state_conv[0], state_gdn[0], attend_s, wts,
        gdn_chunk=min(GDN_CHUNK, LS), gdn_hp=16, gdn_nseg=1, tm_proj=256, tn_proj=512, act_dtype=F32)
    conv_s = big_s[:, 0:A_CONV_CH].reshape(BS, LS, A_CONV_CH)[:, LS - (CONV_W - 1):, :]

    kv_shape_p = (1, BP, LP, ATT_KV_HEADS, ATT_DH)
    kv_shape_s = (1, BS, LS, ATT_KV_HEADS, ATT_DH)
    return (y_p, y_s,
            k_p.reshape(kv_shape_p), v_p.reshape(kv_shape_p), kidx_p.reshape(1, BP, LP, IDX_DH),
            s_p[None], conv_p[None],
            k_s.reshape(kv_shape_s), v_s.reshape(kv_shape_s), kidx_s.reshape(1, BS, LS, IDX_DH),
            s_s[None], conv_s[None])
```

```python
import functools
import math

import jax
import jax.numpy as jnp
from jax import lax
from jax.experimental import pallas as pl
from jax.experimental.pallas import tpu as pltpu

F32 = jnp.float32
BF16 = jnp.bfloat16
I32 = jnp.int32

D_MODEL = 2048
PAGE_SIZE = 128
GDN_QK_HEADS = 16
GDN_V_HEADS = 32
GDN_DK = 128
GDN_DV = 128
CONV_W = 4
GDN_CHUNK = 64
ATT_HEADS = 16
ATT_KV_HEADS = 2
ATT_DH = 128
IDX_HEADS = 16
IDX_DH = 128
TOPK_MAX = 256
NORM_EPS = 1e-6
L2_EPS = 1e-6

A_QK = GDN_QK_HEADS * GDN_DK
A_V = GDN_V_HEADS * GDN_DV
A_CONV_CH = 2 * A_QK + A_V
B_Q = ATT_HEADS * ATT_DH
B_KV = ATT_KV_HEADS * ATT_DH
IDX_Q = IDX_HEADS * IDX_DH
SPLIT_SIZES = (A_CONV_CH, A_V, GDN_V_HEADS, GDN_V_HEADS, B_Q, B_KV, B_KV, B_Q, IDX_Q, IDX_DH, IDX_HEADS,
               D_MODEL, D_MODEL)
B_SEGS = (4, 7, 8, 11, 12)
OFF_AQKV = 0
OFF_AZ = OFF_AQKV + A_CONV_CH
N_A = OFF_AZ + A_V
OFF_BQ = 0
OFF_BZ = OFF_BQ + B_Q
OFF_IQ = OFF_BZ + B_Q
OFF_GA = OFF_IQ + IDX_Q
OFF_GB = OFF_GA + D_MODEL
N_B = OFF_GB + D_MODEL
LANE_BETA = 0
LANE_DECAY = GDN_V_HEADS
LANE_IW = 2 * GDN_V_HEADS

VMEM_LIMIT = 56 * 1024 * 1024
NEG_INF = float("-inf")
INT_MIN = -2 ** 31


def _cparams(sem):
    return pltpu.CompilerParams(dimension_semantics=sem, vmem_limit_bytes=VMEM_LIMIT)


def _dot(a, b):
    return jnp.dot(a, b, preferred_element_type=F32)


def _dot_nt(a, b):
    return lax.dot_general(a, b, (((1,), (1,)), ((), ())), preferred_element_type=F32)


def _dot_tn(a, b):
    return lax.dot_general(a, b, (((0,), (0,)), ((), ())), preferred_element_type=F32)


def _split2(x):
    hi = x.astype(BF16)
    lo = (x - hi.astype(F32)).astype(BF16)
    return hi, lo


def _split3(x):
    x1 = x.astype(BF16)
    r = x - x1.astype(F32)
    x2 = r.astype(BF16)
    x3 = (r - x2.astype(F32)).astype(BF16)
    return x1, x2, x3


def _dot_3pass(a, b):
    ah, al = _split2(a)
    bh, bl = _split2(b)
    return _dot(ah, bh) + _dot(al, bh) + _dot(ah, bl)


def _silu(x):
    h = 0.5 * x
    return h + h * jnp.tanh(h)


def _mod_kernel(c_ref, w_ref, b_ref, o_ref):
    o_ref[...] = _dot_3pass(_silu(c_ref[...]), w_ref[...]) + b_ref[...]


def _mod_call(c_pad, w_ada, b_ada):
    rows, d = c_pad.shape
    n = w_ada.shape[1]
    tn = 768
    return pl.pallas_call(
        _mod_kernel,
        grid=(n // tn,),
        in_specs=[pl.BlockSpec((rows, d), lambda j: (0, 0)),
                  pl.BlockSpec((d, tn), lambda j: (0, j)),
                  pl.BlockSpec((1, tn), lambda j: (0, j))],
        out_specs=pl.BlockSpec((rows, tn), lambda j: (0, j)),
        out_shape=jax.ShapeDtypeStruct((rows, n), F32),
        compiler_params=_cparams(("arbitrary",)),
        name="adaln_mod",
    )(c_pad, w_ada, b_ada)


def _norm_kernel(x_ref, g_ref, scale_ref, shift_ref, o_ref):
    x = x_ref[...]
    y = x * lax.rsqrt(jnp.mean(x * x, axis=-1, keepdims=True) + NORM_EPS) * g_ref[...]
    o_ref[...] = (y * (1.0 + scale_ref[...]) + shift_ref[...]).astype(o_ref.dtype)


def _norm_call(x2d, g_row, scale3, shift3, tm, tiles_per_group):
    m, d = x2d.shape
    r = scale3.shape[1]
    mod_spec = pl.BlockSpec((None, r, d), lambda i: (i // tiles_per_group, 0, 0))
    return pl.pallas_call(
        _norm_kernel,
        grid=(m // tm,),
        in_specs=[pl.BlockSpec((tm, d), lambda i: (i, 0)),
                  pl.BlockSpec((1, d), lambda i: (0, 0)),
                  mod_spec, mod_spec],
        out_specs=pl.BlockSpec((tm, d), lambda i: (i, 0)),
        out_shape=jax.ShapeDtypeStruct((m, d), BF16),
        compiler_params=_cparams(("parallel",)),
        name="norm_modulate",
    )(x2d, g_row, scale3, shift3)


SEG_OFFS = tuple(sum(SPLIT_SIZES[:i]) for i in range(len(SPLIT_SIZES) + 1))


def _proj_nt_kernel(h_ref, w_ref, h2_ref, o_ref, o2_ref, *tail_ref, tail_tiles):
    w16 = w_ref[...].astype(BF16)
    acc = _dot_nt(h_ref[...], w16)
    o_ref[...] = acc.astype(o_ref.dtype)
    if tail_tiles:
        wanted = pl.program_id(1) < tail_tiles
        tail_ref[0][...] = jnp.where(wanted, acc[acc.shape[0] - 8:, :], 0.0)

    is_last = pl.program_id(0) == pl.num_programs(0) - 1

    @pl.when(is_last)
    def _():
        o2_ref[...] = _dot_nt(h2_ref[...], w16).astype(o2_ref.dtype)

    @pl.when(jnp.logical_not(is_last))
    def _():
        o2_ref[...] = jnp.zeros(o2_ref.shape, o2_ref.dtype)


def _proj_nt_call(h, h2, wt3, row_of_tile, n_cols, tm, tn, out_dtype, out2_dtype, name, tail_cols=0):
    m, k = h.shape
    m2 = h2.shape[0]
    n_i, n_j = m // tm, n_cols // tn
    tail_tiles = tail_cols // tn
    out_specs = [pl.BlockSpec((tm, tn), lambda i, j: (i, j)),
                 pl.BlockSpec((m2, tn), lambda i, j: (0, jnp.where(i == n_i - 1, j, n_j)))]
    out_shape = [jax.ShapeDtypeStruct((m, n_cols), out_dtype),
                 jax.ShapeDtypeStruct((m2, n_cols + tn), out2_dtype)]
    if tail_tiles:
        out_specs.append(pl.BlockSpec((None, 8, tn), lambda i, j: (i, 0, jnp.minimum(j, tail_tiles))))
        out_shape.append(jax.ShapeDtypeStruct((n_i, 8, tail_cols + tn), F32))
    return pl.pallas_call(
        functools.partial(_proj_nt_kernel, tail_tiles=tail_tiles),
        grid=(n_i, n_j),
        in_specs=[pl.BlockSpec((tm, k), lambda i, j: (i, 0)),
                  pl.BlockSpec((pl.Element(tn), pl.Element(k)),
                               lambda i, j: (pl.multiple_of(row_of_tile(j), 16), 0)),
                  pl.BlockSpec((m2, k), lambda i, j: (0, 0))],
        out_specs=out_specs,
        out_shape=out_shape,
        compiler_params=_cparams(("arbitrary", "arbitrary")),
        name=name,
    )(h, wt3, h2)


def _group_b_row(j, tn):
    per = B_Q // tn
    seg = j // per
    start = SEG_OFFS[B_SEGS[-1]]
    for s in range(len(B_SEGS) - 2, -1, -1):
        start = jnp.where(seg == s, SEG_OFFS[B_SEGS[s]], start)
    return start + (j % per) * tn


def _proj_tail_kernel(h_ref, wkv_ref, wik_ref, wg_ref, wiw_ref, k_ref, v_ref, kidx_ref, small_ref):
    h = h_ref[...]
    kv = _dot_nt(h, wkv_ref[...].astype(BF16))
    k_ref[...] = kv[:, 0:B_KV]
    v_ref[...] = kv[:, B_KV:]
    kidx_ref[...] = _dot_nt(h, wik_ref[...].astype(BF16))
    gate = _dot_nt(h, wg_ref[...].astype(BF16))
    iw = _dot_nt(h, wiw_ref[...].astype(BF16))
    lane = lax.broadcasted_iota(I32, gate.shape, 1)
    small_ref[...] = jnp.where(lane < LANE_IW, gate, jnp.where(lane < LANE_IW + IDX_HEADS, iw, 0.0))


def _proj_tail_call(h, wt3, tm):
    m, k = h.shape
    widths = (B_KV, B_KV, IDX_DH, 128)
    beta0, kv0, ik0, iw0 = SEG_OFFS[2], SEG_OFFS[5], SEG_OFFS[9], SEG_OFFS[10]
    assert SEG_OFFS[3] - beta0 == LANE_DECAY and SEG_OFFS[4] - beta0 == LANE_IW and SEG_OFFS[6] - kv0 == B_KV

    def rows(start, size):
        return pl.BlockSpec((pl.Element(size), pl.Element(k)), lambda i: (start, 0))

    return pl.pallas_call(
        _proj_tail_kernel,
        grid=(m // tm,),
        in_specs=[pl.BlockSpec((tm, k), lambda i: (i, 0)),
                  rows(kv0, 2 * B_KV), rows(ik0, IDX_DH), rows(beta0, 128), rows(iw0 - LANE_IW, 128)],
        out_specs=[pl.BlockSpec((tm, w), lambda i: (i, 0)) for w in widths],
        out_shape=[jax.ShapeDtypeStruct((m, w), F32) for w in widths],
        compiler_params=_cparams(("parallel",)),
        name="in_proj_tail",
    )(h, wt3, wt3, wt3, wt3)


def _gates_kernel(sm_ref, alog_ref, dtb_ref, o_ref, *, chunk):
    x = sm_ref[...]
    tm = x.shape[0]
    lane = lax.broadcasted_iota(I32, x.shape, 1)
    beta = jax.nn.sigmoid(x)
    z = x + dtb_ref[...]
    softplus = jnp.maximum(z, 0.0) + jnp.log1p(jnp.exp(-jnp.abs(z)))
    g = -jnp.exp(alog_ref[...]) * softplus
    g = jnp.where((lane >= LANE_DECAY) & (lane < LANE_DECAY + GDN_V_HEADS), g, 0.0)
    ri = lax.broadcasted_iota(I32, (tm, tm), 0)
    ci = lax.broadcasted_iota(I32, (tm, tm), 1)
    shift = int(math.log2(chunk))
    tri = jnp.where((ri >= ci) & ((ri >> shift) == (ci >> shift)), 1.0, 0.0).astype(BF16)
    g1, g2, g3 = _split3(g)
    gc = _dot(tri, g1) + _dot(tri, g2) + _dot(tri, g3)
    o_ref[...] = jnp.where(lane < GDN_V_HEADS, beta, gc)


def _gates_call(small, alog_row, dtb_row, chunk):
    m = small.shape[0]
    tm = 256
    return pl.pallas_call(
        functools.partial(_gates_kernel, chunk=chunk),
        grid=(m // tm,),
        in_specs=[pl.BlockSpec((tm, 128), lambda i: (i, 0)),
                  pl.BlockSpec((1, 128), lambda i: (0, 0)),
                  pl.BlockSpec((1, 128), lambda i: (0, 0))],
        out_specs=pl.BlockSpec((tm, 128), lambda i: (i, 0)),
        out_shape=jax.ShapeDtypeStruct((m, 128), F32),
        compiler_params=_cparams(("parallel",)),
        name="gdn_gates",
    )(small, alog_row, dtb_row)


def _unrolled_loop(n, unroll, fn):
    unroll = min(unroll, n)
    assert n % unroll == 0

    def body(t, carry):
        for u in range(unroll):
            fn(t * unroll + u)
        return carry

    lax.fori_loop(0, n // unroll, body, 0)


def _gdn_kernel(*refs, L, C, HP, NSEG, has_state):
    ins = 13 if has_state else 12
    (q_ref, k_ref, v_ref, z_ref, bq_ref, bk_ref, bv_ref, wq_ref, wk_ref, wv_ref, gates_ref, gn_ref) = refs[:12]
    s0_ref = refs[12] if has_state else None
    (y_ref, sout_ref, xq_s, xk_s, xv_s, beta_s, g_s, st_s, a_s, p_s, rhs_s, kt_s, qkd_s, qd_s,
     m_s, n_s, qe_s, o_s) = refs[ins:]
    nh = 2 * HP
    nc = L // C
    hq0 = pl.program_id(1) * HP

    def first_segment():
        xq_s[0:8, :] = bq_ref[...]
        xk_s[0:8, :] = bk_ref[...]
        xv_s[0:8, :] = bv_ref[...]
        if has_state:
            st_s[...] = s0_ref[...]
        else:
            st_s[...] = jnp.zeros(st_s.shape, F32)

    def next_segment():
        xq_s[0:8, :] = xq_s[L:L + 8, :]
        xk_s[0:8, :] = xk_s[L:L + 8, :]
        xv_s[0:8, :] = xv_s[L:L + 8, :]

    if NSEG == 1:
        first_segment()
    else:
        pl.when(pl.program_id(2) == 0)(first_segment)
        pl.when(pl.program_id(2) > 0)(next_segment)
    xq_s[8:L + 8, :] = q_ref[...].astype(F32)
    xk_s[8:L + 8, :] = k_ref[...].astype(F32)
    xv_s[8:L + 8, :] = v_ref[...].astype(F32)

    gates = gates_ref[...]
    lane = lax.broadcasted_iota(I32, gates.shape, 1)
    for idx in range(nh):
        hv = 2 * hq0 + idx
        bcol = jnp.sum(jnp.where(lane == hv + LANE_BETA, gates, 0.0), axis=-1, keepdims=True)
        gcol = jnp.sum(jnp.where(lane == hv + LANE_DECAY, gates, 0.0), axis=-1, keepdims=True)
        beta_s[idx] = jnp.broadcast_to(bcol, (L, 128))
        g_s[idx] = jnp.broadcast_to(gcol, (L, 128))

    ii = lax.broadcasted_iota(I32, (C, C), 0)
    jj = lax.broadcasted_iota(I32, (C, C), 1)
    gn = gn_ref[...]

    def conv(x_s, r0, cols, w):
        win = x_s[pl.ds(r0, C + 8), cols]
        acc = (win[5:5 + C] * w[0:1] + win[6:6 + C] * w[1:2] + win[7:7 + C] * w[2:3] + win[8:8 + C] * w[3:4])
        return _silu(acc)

    def l2n(x):
        return x * lax.rsqrt(jnp.sum(x * x, axis=-1, keepdims=True) + L2_EPS)

    def lanes(i, width):
        if isinstance(i, int):
            return slice(i * width, (i + 1) * width)
        return pl.ds(pl.multiple_of(i * width, width), width)

    def pair_terms(j):
        c, p = (j, 0) if HP == 1 else (j // HP, j % HP)
        r0 = pl.multiple_of(c * C, C)
        rows = pl.ds(r0, C)
        lq, lv = lanes(p, 128), lanes(p, 256)
        q = l2n(conv(xq_s, r0, lq, wq_ref[:, lq])) * (GDN_DK ** -0.5)
        k = l2n(conv(xk_s, r0, lq, wk_ref[:, lq]))
        vc = conv(xv_s, r0, lv, wv_ref[:, lv])
        k16 = k.astype(BF16)
        kk = _dot_nt(k16, k16)
        qk = _dot_nt(q.astype(BF16), k16)
        for r in range(2):
            idx = 2 * p + r
            e = c * nh + idx
            gb = g_s[idx, rows, :]
            bb = beta_s[idx, rows, :]
            e_g = jnp.exp(gb)
            e_tail = jnp.exp(gb[C - 1:C, :] - gb)
            if C < 128:
                gpad = jnp.concatenate([gb, jnp.zeros((128 - C, 128), F32)], axis=0)
            else:
                gpad = gb
            dexp = gb[:, 0:C] - gpad.T[0:C, 0:C]
            dm = jnp.exp(jnp.where(ii >= jj, dexp, NEG_INF))
            a = jnp.where(ii > jj, kk * dm, 0.0) * bb[:, 0:C]
            a_s[e] = a.astype(a_s.dtype)
            p_s[e] = jnp.where(ii == jj, 1.0, 0.0) - a
            rhs_s[e] = jnp.concatenate([vc[:, r * 128:(r + 1) * 128] * bb, k * bb * e_g], axis=1).astype(rhs_s.dtype)
            kt_s[e] = (k * e_tail).astype(kt_s.dtype)
            qkd_s[e] = (qk * dm).astype(qkd_s.dtype)
            qd_s[e] = q * e_g

    def square(e):
        ak = a_s[e].astype(BF16)
        a_s[e] = _dot(ak, ak).astype(a_s.dtype)

    def inverse_stage(e, last):
        ak = a_s[e].astype(BF16)
        pk = p_s[e]
        p_s[e] = pk + _dot(pk.astype(BF16), ak)
        if not last:
            a_s[e] = _dot(ak, ak).astype(a_s.dtype)

    def solve(e):
        rhs_s[e] = _dot(p_s[e].astype(BF16), rhs_s[e].astype(BF16)).astype(rhs_s.dtype)

    def state_terms(e):
        sol16 = rhs_s[e].astype(BF16)
        mn = _dot_tn(kt_s[e].astype(BF16), sol16)
        qo = _dot(qkd_s[e].astype(BF16), sol16)
        n_s[e] = mn[:, 0:GDN_DV]
        m_s[e] = mn[:, GDN_DV:].astype(m_s.dtype)
        o_s[e] = qo[:, 0:GDN_DV]
        qe_s[e] = (qd_s[e] - qo[:, GDN_DV:]).astype(qe_s.dtype)

    def head_step(c, idx):
        e = c * nh + idx
        r0 = pl.multiple_of(c * C, C)
        rows = pl.ds(r0, C)
        s = st_s[idx]
        s16 = s.astype(BF16)
        o = _dot(qe_s[e].astype(BF16), s16) + o_s[e]
        e_last = jnp.exp(g_s[idx, pl.ds(r0 + C - 1, 1), :])
        st_s[idx] = s * e_last - _dot(m_s[e].astype(BF16), s16) + n_s[e]
        on = o * lax.rsqrt(jnp.mean(o * o, axis=-1, keepdims=True) + NORM_EPS) * gn
        zz = z_ref[rows, lanes(idx, 128)].astype(F32)
        y_ref[rows, lanes(idx, 128)] = (on * _silu(zz)).astype(y_ref.dtype)

    ne = nc * nh
    stages = int(math.log2(C)) - 1
    _unrolled_loop(nc * HP, 2 if C >= 64 else 4, pair_terms)
    _unrolled_loop(ne, 32, square)
    for st in range(stages):
        _unrolled_loop(ne, 32, functools.partial(inverse_stage, last=(st == stages - 1)))
    _unrolled_loop(ne, 16, solve)
    _unrolled_loop(ne, 16, state_terms)

    def chunk_step(c, carry):
        if nh <= 4:
            for idx in range(nh):
                head_step(c, idx)
        else:
            _unrolled_loop(nh, 16, functools.partial(head_step, c))
        return carry

    lax.fori_loop(0, nc, chunk_step, 0)
    sout_ref[...] = st_s[...]


def _gdn_call(big, convbuf8, conv_w, gates, gn_row, s0, *, B, L, C, HP, NSEG, y_dtype):
    has_state = s0 is not None
    ls = L // NSEG
    ne = (ls // C) * 2 * HP
    half = BF16 if C % 16 == 0 else F32
    nq = GDN_QK_HEADS // HP
    wq, wv = HP * 128, HP * 256
    kq0 = A_QK // wq
    v0 = (2 * A_QK) // wv
    z0 = OFF_AZ // wv
    in_specs = [
        pl.BlockSpec((ls, wq), lambda b, h, s: (b * NSEG + s, h)),
        pl.BlockSpec((ls, wq), lambda b, h, s: (b * NSEG + s, kq0 + h)),
        pl.BlockSpec((ls, wv), lambda b, h, s: (b * NSEG + s, v0 + h)),
        pl.BlockSpec((ls, wv), lambda b, h, s: (b * NSEG + s, z0 + h)),
        pl.BlockSpec((None, 8, wq), lambda b, h, s: (b, 0, h)),
        pl.BlockSpec((None, 8, wq), lambda b, h, s: (b, 0, kq0 + h)),
        pl.BlockSpec((None, 8, wv), lambda b, h, s: (b, 0, v0 + h)),
        pl.BlockSpec((CONV_W, wq), lambda b, h, s: (0, h)),
        pl.BlockSpec((CONV_W, wq), lambda b, h, s: (0, kq0 + h)),
        pl.BlockSpec((CONV_W, wv), lambda b, h, s: (0, v0 + h)),
        pl.BlockSpec((ls, 128), lambda b, h, s: (b * NSEG + s, 0)),
        pl.BlockSpec((1, 128), lambda b, h, s: (0, 0)),
    ]
    args = [big, big, big, big, convbuf8, convbuf8, convbuf8, conv_w, conv_w, conv_w, gates, gn_row]
    if has_state:
        in_specs.append(pl.BlockSpec((None, 2 * HP, GDN_DK, GDN_DV), lambda b, h, s: (b, h, 0, 0)))
        args.append(s0)
    return pl.pallas_call(
        functools.partial(_gdn_kernel, L=ls, C=C, HP=HP, NSEG=NSEG, has_state=has_state),
        grid=(B, nq, NSEG),
        in_specs=in_specs,
        out_specs=[pl.BlockSpec((ls, wv), lambda b, h, s: (b * NSEG + s, h)),
                   pl.BlockSpec((None, 2 * HP, GDN_DK, GDN_DV), lambda b, h, s: (b, h, 0, 0))],
        out_shape=[jax.ShapeDtypeStruct((B * L, A_V), y_dtype),
                   jax.ShapeDtypeStruct((B, GDN_V_HEADS, GDN_DK, GDN_DV), F32)],
        scratch_shapes=[pltpu.VMEM((ls + 8, wq), F32), pltpu.VMEM((ls + 8, wq), F32), pltpu.VMEM((ls + 8, wv), F32),
                        pltpu.VMEM((2 * HP, ls, 128), F32), pltpu.VMEM((2 * HP, ls, 128), F32),
                        pltpu.VMEM((2 * HP, GDN_DK, GDN_DV), F32),
                        pltpu.VMEM((ne, C, C), half), pltpu.VMEM((ne, C, C), F32),
                        pltpu.VMEM((ne, C, GDN_DV + GDN_DK), half), pltpu.VMEM((ne, C, GDN_DK), half),
                        pltpu.VMEM((ne, C, C), half), pltpu.VMEM((ne, C, GDN_DK), F32),
                        pltpu.VMEM((ne, GDN_DK, GDN_DV), BF16), pltpu.VMEM((ne, GDN_DK, GDN_DV), F32),
                        pltpu.VMEM((ne, C, GDN_DK), half), pltpu.VMEM((ne, C, GDN_DV), F32)],
        compiler_params=_cparams(("parallel", "arbitrary", "arbitrary")),
        name="gdn_prompt" if not has_state else "gdn_sample",
    )(*args)


KEY_NEG_INF = -2 ** 31 + 0x7FFFFF


def _key_to_float(key):
    bits = key ^ ((key >> 31) & jnp.int32(0x7FFFFFFF))
    return pltpu.bitcast(bits, F32)


def _count_true(mask):
    ones = jnp.where(mask, 1.0, 0.0)
    parts = [ones[:, i * 128:(i + 1) * 128] for i in range(mask.shape[1] // 128)]
    while len(parts) > 1:
        parts = [a + b for a, b in zip(parts[0::2], parts[1::2])] + ([parts[-1]] if len(parts) % 2 else [])
    return jnp.sum(parts[0], axis=-1, keepdims=True)


def _selection_bias(score_ref, bias_ref, qpos0, rows, width, topk, bits=1):
    def step(it, t):
        best = t
        for digit in range(1, 2 ** bits):
            cand = t + jnp.left_shift(jnp.int32(digit), 32 - bits * (it + 1))
            cnt = _count_true(score_ref[:, 0:width] >= _key_to_float(cand))
            best = jnp.where((cnt >= float(topk)) | (cand <= KEY_NEG_INF), cand, best)
        return best

    thr = _key_to_float(lax.fori_loop(0, 32 // bits, step, jnp.full((rows, 1), INT_MIN, I32)))
    score = score_ref[:, 0:width]
    causal = (lax.broadcasted_iota(I32, (rows, width), 1)
              <= qpos0 + lax.broadcasted_iota(I32, (rows, width), 0))
    sel = (score >= thr) & causal
    bias_ref[:, 0:width] = jnp.where(sel, 0.0, NEG_INF)
    n_sel = _count_true(sel)
    has_ties = jnp.max(n_sel) > float(topk)

    @pl.when(has_ties)
    def _():
        need = float(topk) - _count_true(score > thr)
        eq16 = jnp.where(score == thr, 1.0, 0.0).astype(BF16)

        def tile(j, carry):
            c0 = pl.multiple_of(j * 128, 128)
            ri = lax.broadcasted_iota(I32, (width, 128), 0)
            ci = lax.broadcasted_iota(I32, (width, 128), 1) + c0
            upper = jnp.where(ri <= ci, 1.0, 0.0).astype(BF16)
            prefix = _dot(eq16, upper)
            s_t = score_ref[:, pl.ds(c0, 128)]
            keep = (s_t > thr) | ((s_t == thr) & (prefix <= need))
            old = bias_ref[:, pl.ds(c0, 128)]
            bias_ref[:, pl.ds(c0, 128)] = jnp.where(keep, old, NEG_INF)
            return carry

        lax.fori_loop(0, width // 128, tile, 0)


SOFTMAX_LOG2_SCALE = (ATT_DH ** -0.5) * math.log2(math.e)


def _dsa_prompt_kernel(qi_ref, wi_ref, kidx_ref, q_ref, k_ref, v_ref, z_ref, o_ref,
                       score_s, bias_s, kidx16_s, k16_s, vext_s, m_s, acc_s, *, QB, SUB, KT, L, topk):
    i = pl.program_id(1)
    q0 = i * QB
    nt = (q0 + QB - 1) // KT + 1
    groups = ATT_HEADS // ATT_KV_HEADS

    @pl.when(i == 0)
    def _():
        kidx16_s[...] = kidx_ref[...].astype(BF16)
        k16_s[...] = k_ref[...].astype(BF16)
        for n in range(ATT_KV_HEADS):
            vext_s[n, :, 0:ATT_DH] = v_ref[:, n * ATT_DH:(n + 1) * ATT_DH].astype(BF16)
            vext_s[n, :, ATT_DH:] = jnp.ones((L, ATT_DH), BF16)

    rowpos = q0 + lax.broadcasted_iota(I32, (SUB, KT), 0)
    col = lax.broadcasted_iota(I32, (SUB, KT), 1)

    def tile_rows(t):
        return pl.ds(pl.multiple_of(t * KT, KT), KT)

    subs = [slice(r, r + SUB) for r in range(0, QB, SUB)]

    def index_tile(t, carry):
        kid = kidx16_s[tile_rows(t), :]
        for rs in subs:
            wi = wi_ref[rs, :] * ((IDX_DH ** -0.5) * (IDX_HEADS ** -0.5))
            acc = jnp.zeros((SUB, KT), F32)
            for h in range(IDX_HEADS):
                lg = _dot_nt(qi_ref[rs, h * IDX_DH:(h + 1) * IDX_DH], kid)
                acc = acc + jnp.maximum(lg, 0.0) * wi[:, LANE_IW + h:LANE_IW + h + 1]
            causal = col + t * KT <= rowpos + rs.start
            score_s[rs, tile_rows(t)] = jnp.where(causal, acc, NEG_INF)
        return carry

    lax.fori_loop(0, nt, index_tile, 0)
    for w in range(1, L // KT + 1):
        @pl.when(nt == w)
        def _():
            _selection_bias(score_s, bias_s, q0, QB, w * KT, topk)

    m_s[...] = jnp.full(m_s.shape, float(jnp.finfo(F32).min), F32)
    acc_s[...] = jnp.zeros(acc_s.shape, F32)

    def attend_tile(t, carry):
        for n in range(ATT_KV_HEADS):
            kn = k16_s[tile_rows(t), n * ATT_DH:(n + 1) * ATT_DH]
            vn = vext_s[n, tile_rows(t), :]
            for g in range(groups):
                h = n * groups + g
                for rs in subs:
                    sb = _dot_nt(q_ref[rs, h * ATT_DH:(h + 1) * ATT_DH], kn) + bias_s[rs, tile_rows(t)]
                    m_old = m_s[h, rs, :]
                    m_new = jnp.maximum(m_old, jnp.max(sb, axis=-1, keepdims=True))
                    p = jnp.exp2((sb - jnp.concatenate([m_new] * (KT // 128), axis=1)) * SOFTMAX_LOG2_SCALE)
                    alpha = jnp.exp2((m_old - m_new) * SOFTMAX_LOG2_SCALE)
                    m_s[h, rs, :] = m_new
                    acc_s[h, rs, :] = (acc_s[h, rs, :] * jnp.concatenate([alpha, alpha], axis=1)
                                       + _dot(p.astype(BF16), vn))
        return carry

    lax.fori_loop(0, nt, attend_tile, 0)
    for h in range(ATT_HEADS):
        cols = slice(h * ATT_DH, (h + 1) * ATT_DH)
        zz = z_ref[:, cols].astype(F32)
        acc = acc_s[h]
        o_ref[:, cols] = (acc[:, 0:ATT_DH] / acc[:, ATT_DH:] * _silu(zz)).astype(o_ref.dtype)


def _dsa_prompt_call(big, small, kidx, k, v, *, B, L, QB):
    topk = min(TOPK_MAX, L // 4)
    nqb = L // QB
    iq0 = OFF_IQ // B_Q
    bq0 = OFF_BQ // B_Q
    bz0 = OFF_BZ // B_Q
    return pl.pallas_call(
        functools.partial(_dsa_prompt_kernel, QB=QB, SUB=128, KT=256, L=L, topk=topk),
        grid=(B, nqb),
        in_specs=[pl.BlockSpec((QB, IDX_Q), lambda b, i: (b * nqb + i, iq0)),
                  pl.BlockSpec((QB, 128), lambda b, i: (b * nqb + i, 0)),
                  pl.BlockSpec((L, IDX_DH), lambda b, i: (b, 0)),
                  pl.BlockSpec((QB, B_Q), lambda b, i: (b * nqb + i, bq0)),
                  pl.BlockSpec((L, B_KV), lambda b, i: (b, 0)),
                  pl.BlockSpec((L, B_KV), lambda b, i: (b, 0)),
                  pl.BlockSpec((QB, B_Q), lambda b, i: (b * nqb + i, bz0))],
        out_specs=pl.BlockSpec((QB, B_Q), lambda b, i: (b * nqb + i, 0)),
        out_shape=jax.ShapeDtypeStruct((B * L, B_Q), BF16),
        scratch_shapes=[pltpu.VMEM((QB, L), F32), pltpu.VMEM((QB, L), F32),
                        pltpu.VMEM((L, IDX_DH), BF16), pltpu.VMEM((L, B_KV), BF16),
                        pltpu.VMEM((ATT_KV_HEADS, L, 2 * ATT_DH), BF16),
                        pltpu.VMEM((ATT_HEADS, QB, 128), F32), pltpu.VMEM((ATT_HEADS, QB, 2 * ATT_DH), F32)],
        compiler_params=_cparams(("parallel", "arbitrary")),
        name="dsa_prompt",
    )(big, small, kidx, big, k, v, big)


def _dsa_sample_kernel(pt_ref, qi_ref, wi_ref, kin_ref, q_ref, kn_ref, vn_ref, z_ref, ck_hbm, cv_hbm, ci_hbm,
                       o_ref, kbuf, vbuf, ibuf, sems, score_s, bias_s, *, NP, T, topk):
    b = pl.program_id(0)
    nb = pl.num_programs(0)
    slot = b % 2
    past = NP * PAGE_SIZE
    width = past + 128

    kvp = ATT_KV_HEADS * PAGE_SIZE

    def page_copies(bb, sl, p):
        page = pt_ref[bb * NP + p]
        rows = pl.ds(pl.multiple_of(p * PAGE_SIZE, PAGE_SIZE), PAGE_SIZE)
        src2 = pl.ds(pl.multiple_of(page * kvp, kvp), kvp)
        dst2 = pl.ds(pl.multiple_of(p * kvp, kvp), kvp)
        return (pltpu.make_async_copy(ck_hbm.at[src2], kbuf.at[sl, dst2], sems.at[0, sl]),
                pltpu.make_async_copy(cv_hbm.at[src2], vbuf.at[sl, dst2], sems.at[1, sl]),
                pltpu.make_async_copy(ci_hbm.at[page], ibuf.at[sl, rows], sems.at[2, sl]))

    def start_fetch(bb, sl):
        def body(p, carry):
            for cp in page_copies(bb, sl, p):
                cp.start()
            return carry
        lax.fori_loop(0, NP, body, 0)

    def wait_fetch(bb, sl):
        def body(p, carry):
            for cp in page_copies(bb, sl, p):
                cp.wait()
            return carry
        lax.fori_loop(0, NP, body, 0)

    @pl.when(b == 0)
    def _():
        for sl in range(2):
            kbuf[sl, ATT_KV_HEADS * past:, :] = jnp.zeros((ATT_KV_HEADS * 128, ATT_DH), F32)
            vbuf[sl, ATT_KV_HEADS * past:, :] = jnp.zeros((ATT_KV_HEADS * 128, ATT_DH), F32)
            ibuf[sl, past:, :] = jnp.zeros((128, IDX_DH), F32)
        start_fetch(0, 0)

    @pl.when(b + 1 < nb)
    def _():
        start_fetch(b + 1, 1 - slot)

    for n in range(ATT_KV_HEADS):
        new_rows = pl.ds(ATT_KV_HEADS * past + n, T, stride=ATT_KV_HEADS)
        kbuf[slot, new_rows, :] = kn_ref[:, n * ATT_DH:(n + 1) * ATT_DH]
        vbuf[slot, new_rows, :] = vn_ref[:, n * ATT_DH:(n + 1) * ATT_DH]
    ibuf[slot, past:past + T, :] = kin_ref[...]
    wait_fetch(b, slot)

    qi = qi_ref[...]
    qis = jnp.concatenate([qi[:, h * IDX_DH:(h + 1) * IDX_DH] for h in range(IDX_HEADS)], axis=0).astype(BF16)
    wi = wi_ref[...] * ((IDX_DH ** -0.5) * (IDX_HEADS ** -0.5))
    lg = _dot_nt(qis, ibuf[slot].astype(BF16))
    score = jnp.zeros((T, width), F32)
    for h in range(IDX_HEADS):
        score = score + jnp.maximum(lg[h * T:(h + 1) * T, :], 0.0) * wi[:, LANE_IW + h:LANE_IW + h + 1]
    qpos = past + lax.broadcasted_iota(I32, (T, width), 0)
    kpos = lax.broadcasted_iota(I32, (T, width), 1)
    causal = kpos <= qpos
    score_s[...] = jnp.where(causal, score, NEG_INF)
    _selection_bias(score_s, bias_s, past, T, width, topk, bits=2)

    groups = ATT_HEADS // ATT_KV_HEADS
    q = q_ref[...]
    bias = jnp.concatenate([bias_s[...]] * groups, axis=0)
    for n in range(ATT_KV_HEADS):
        head_rows = pl.ds(n, width, stride=ATT_KV_HEADS)
        kn = kbuf[slot, head_rows, :].astype(BF16)
        vn = vbuf[slot, head_rows, :].astype(BF16)
        qs = jnp.concatenate([q[:, (n * groups + g) * ATT_DH:(n * groups + g + 1) * ATT_DH] for g in range(groups)],
                             axis=0).astype(BF16)
        s = _dot_nt(qs, kn) * (ATT_DH ** -0.5) + bias
        m = jnp.max(s, axis=-1, keepdims=True)
        p = jnp.exp(s - m)
        denom = jnp.sum(p, axis=-1, keepdims=True)
        o = _dot(p.astype(BF16), vn) / denom
        for g in range(groups):
            h = n * groups + g
            cols = slice(h * ATT_DH, (h + 1) * ATT_DH)
            zz = z_ref[:, cols]
            o_ref[:, cols] = o[g * T:(g + 1) * T, :] * _silu(zz)


def _dsa_sample_call(big, small, kidx, k, v, cache_k, cache_v, cache_kidx, page_table, *, B, T):
    npages = page_table.shape[1]
    past = npages * PAGE_SIZE
    topk = min(TOPK_MAX, (past + T) // 4)
    width = past + 128
    iq0 = OFF_IQ // B_Q
    bq0 = OFF_BQ // B_Q
    bz0 = OFF_BZ // B_Q
    grid_spec = pltpu.PrefetchScalarGridSpec(
        num_scalar_prefetch=1,
        grid=(B,),
        in_specs=[pl.BlockSpec((T, IDX_Q), lambda b, pt: (b, iq0)),
                  pl.BlockSpec((T, 128), lambda b, pt: (b, 0)),
                  pl.BlockSpec((T, IDX_DH), lambda b, pt: (b, 0)),
                  pl.BlockSpec((T, B_Q), lambda b, pt: (b, bq0)),
                  pl.BlockSpec((T, B_KV), lambda b, pt: (b, 0)),
                  pl.BlockSpec((T, B_KV), lambda b, pt: (b, 0)),
                  pl.BlockSpec((T, B_Q), lambda b, pt: (b, bz0)),
                  pl.BlockSpec(memory_space=pl.ANY),
                  pl.BlockSpec(memory_space=pl.ANY),
                  pl.BlockSpec(memory_space=pl.ANY)],
        out_specs=pl.BlockSpec((T, B_Q), lambda b, pt: (b, 0)),
        scratch_shapes=[pltpu.VMEM((2, ATT_KV_HEADS * width, ATT_DH), F32),
                        pltpu.VMEM((2, ATT_KV_HEADS * width, ATT_DH), F32),
                        pltpu.VMEM((2, width, IDX_DH), F32), pltpu.SemaphoreType.DMA((3, 2)),
                        pltpu.VMEM((T, width), F32), pltpu.VMEM((T, width), F32)])
    return pl.pallas_call(
        functools.partial(_dsa_sample_kernel, NP=npages, T=T, topk=topk),
        grid_spec=grid_spec,
        out_shape=jax.ShapeDtypeStruct((B * T, B_Q), F32),
        compiler_params=_cparams(("arbitrary",)),
        name="dsa_sample",
    )(page_table.reshape(-1), big, small, kidx, big, k, v, big, cache_k, cache_v, cache_kidx)


def _merge_kernel(ya_ref, yb_ref, ga_ref, gb_ref, wpa_ref, wpb_ref, o_ref):
    pa = _dot(ya_ref[...].astype(BF16), wpa_ref[...])
    pb = _dot(yb_ref[...].astype(BF16), wpb_ref[...])
    o = jax.nn.sigmoid(ga_ref[...].astype(F32)) * pa + jax.nn.sigmoid(gb_ref[...].astype(F32)) * pb
    o_ref[...] = o.astype(o_ref.dtype)


def _merge_call(ya, yb, big, wpa, wpb, tm, tn):
    m = ya.shape[0]
    n = wpa.shape[1]
    ga0 = OFF_GA // tn
    gb0 = OFF_GB // tn
    return pl.pallas_call(
        _merge_kernel,
        grid=(m // tm, n // tn),
        in_specs=[pl.BlockSpec((tm, A_V), lambda i, j: (i, 0)),
                  pl.BlockSpec((tm, B_Q), lambda i, j: (i, 0)),
                  pl.BlockSpec((tm, tn), lambda i, j: (i, ga0 + j)),
                  pl.BlockSpec((tm, tn), lambda i, j: (i, gb0 + j)),
                  pl.BlockSpec((A_V, tn), lambda i, j: (0, j)),
                  pl.BlockSpec((B_Q, tn), lambda i, j: (0, j))],
        out_specs=pl.BlockSpec((tm, tn), lambda i, j: (i, j)),
        out_shape=jax.ShapeDtypeStruct((m, n), BF16),
        compiler_params=_cparams(("parallel", "arbitrary")),
        name="merge_proj",
    )(ya, yb, big, big, wpa, wpb)


def _final_kernel(m_ref, w_ref, x_ref, gate_ref, g_ref, o_ref):
    t = _dot(m_ref[...], w_ref[...])
    y = t * lax.rsqrt(jnp.mean(t * t, axis=-1, keepdims=True) + NORM_EPS) * g_ref[...]
    o_ref[...] = x_ref[...] + gate_ref[...] * y


def _final_call(merged, w_out, x2d, gate3, g_row, tm, tiles_per_group):
    m, d = x2d.shape
    r = gate3.shape[1]
    return pl.pallas_call(
        _final_kernel,
        grid=(m // tm,),
        in_specs=[pl.BlockSpec((tm, d), lambda i: (i, 0)),
                  pl.BlockSpec((d, d), lambda i: (0, 0)),
                  pl.BlockSpec((tm, d), lambda i: (i, 0)),
                  pl.BlockSpec((None, r, d), lambda i: (i // tiles_per_group, 0, 0)),
                  pl.BlockSpec((1, d), lambda i: (0, 0))],
        out_specs=pl.BlockSpec((tm, d), lambda i: (i, 0)),
        out_shape=jax.ShapeDtypeStruct((m, d), F32),
        compiler_params=_cparams(("parallel",)),
        name="out_proj",
    )(merged, w_out, x2d, gate3, g_row)


def _cast_kernel(x_ref, o_ref):
    o_ref[...] = x_ref[...].astype(o_ref.dtype)


def _cast_bf16_call(w3):
    _, r, n = w3.shape
    tr = 256
    return pl.pallas_call(
        _cast_kernel,
        grid=(r // tr,),
        in_specs=[pl.BlockSpec((None, tr, n), lambda i: (0, i, 0))],
        out_specs=pl.BlockSpec((tr, n), lambda i: (i, 0)),
        out_shape=jax.ShapeDtypeStruct((r, n), BF16),
        compiler_params=_cparams(("parallel",)),
        name="w_cast",
    )(w3)


def _lane_row(vec, lane0):
    return jnp.zeros((1, 128), F32).at[0, lane0:lane0 + vec.shape[0]].set(vec.astype(F32))


def _modulated_input(x, mod, pre_g):
    B, L, D = x.shape
    M = B * L
    x2d = x.reshape(M, D)
    shift, scale, gate = mod[:, 0:D], mod[:, D:2 * D], mod[:, 2 * D:3 * D]
    if L >= 256:
        tm_n = 256
        per = L // tm_n
        scale3, shift3, gate3 = scale[:, None, :], shift[:, None, :], gate[:, None, :]
    else:
        tm_n = M
        per = 1
        scale3, shift3, gate3 = (jnp.repeat(t, L, axis=0)[None] for t in (scale, shift, gate))
    return _norm_call(x2d, pre_g, scale3, shift3, tm_n, per), x2d, gate3


def _mixers(x2d, gate3, big_a, big_b, h, conv_hist, s0, attend, wts, *, B, L, gdn_chunk, gdn_hp, gdn_nseg,
            act_dtype):
    (wt3, conv_w, alog_row, dtb_row, gn_row, wpa, wpb, wout, post_g) = wts
    M = B * L
    k, v, kidx, small = _proj_tail_call(h, wt3, min(M, 1024))

    gates = _gates_call(small, alog_row, dtb_row, gdn_chunk)
    if conv_hist is None:
        convbuf8 = jnp.zeros((B, 8, A_CONV_CH), F32)
    else:
        convbuf8 = jnp.concatenate([jnp.zeros((B, 5, A_CONV_CH), F32), conv_hist.astype(F32)], axis=1)
    y_a, s_new = _gdn_call(big_a, convbuf8, conv_w, gates, gn_row, s0, B=B, L=L, C=gdn_chunk, HP=gdn_hp,
                           NSEG=gdn_nseg, y_dtype=act_dtype)
    y_b = attend(big_b, small, kidx, k, v)
    tm_m = min(1024, M)
    merged = _merge_call(y_a, y_b, big_b, wpa, wpb, tm_m, 512)
    tm_f = min(512, M)
    per_f = (L // tm_f) if L >= 256 else 1
    y = _final_call(merged, wout, x2d, gate3, post_g, tm_f, per_f)
    return y.reshape(B, L, x2d.shape[1]), k, v, kidx, s_new


def kernel(x_prompt, x_sample, c_prompt, c_sample, cache_k, cache_v, cache_kidx, state_gdn, state_conv,
           page_table, w_ada, b_ada, pre_norm_g, w_in, conv_w, a_log, dt_bias, gdn_norm_g, w_pa, w_pb, w_out,
           post_norm_g):
    depth = w_in.shape[0]
    assert depth == 1, "single-layer trunk"
    BP, LP, D = x_prompt.shape
    BS, LS, _ = x_sample.shape
    n_pool = cache_k.shape[1]

    wt3 = jnp.swapaxes(w_in, 1, 2)[0]
    pre_g = pre_norm_g[0][None, :]
    wts = (wt3, conv_w[0],
           _lane_row(a_log[0], LANE_DECAY), _lane_row(dt_bias[0], LANE_DECAY), gdn_norm_g[0][None, :],
           _cast_bf16_call(w_pa), _cast_bf16_call(w_pb), _cast_bf16_call(w_out), post_norm_g[0][None, :])

    rows = BP + BS
    rows_pad = -(-rows // 16) * 16
    c_all = jnp.concatenate([c_prompt, c_sample, jnp.zeros((rows_pad - rows, D), F32)], axis=0)
    mod = _mod_call(c_all, w_ada[0], b_ada[0][None, :])

    h_p, x2d_p, gate3_p = _modulated_input(x_prompt, mod[0:BP], pre_g)
    h_s, x2d_s, gate3_s = _modulated_input(x_sample, mod[BP:BP + BS], pre_g)
    tn = 512
    big_a_p, big_a_s, aqkv_tail = _proj_nt_call(h_p, h_s, wt3, lambda j: j * tn, N_A, LP, tn, BF16, F32,
                                                "in_proj_a", tail_cols=A_CONV_CH)
    big_b_p, big_b_s = _proj_nt_call(h_p, h_s, wt3, functools.partial(_group_b_row, tn=tn), N_B, LP, tn, BF16,
                                     F32, "in_proj_b")
    conv_p = aqkv_tail[:, 8 - (CONV_W - 1):, 0:A_CONV_CH]

    attend_p = functools.partial(_dsa_prompt_call, B=BP, L=LP, QB=256)
    y_p, k_p, v_p, kidx_p, s_p = _mixers(
        x2d_p, gate3_p, big_a_p, big_b_p, h_p, None, None, attend_p, wts, B=BP, L=LP,
        gdn_chunk=min(GDN_CHUNK, LP), gdn_hp=8, gdn_nseg=8, act_dtype=BF16)

    ck = cache_k[0].reshape(n_pool * PAGE_SIZE * ATT_KV_HEADS, ATT_DH)
    cv = cache_v[0].reshape(n_pool * PAGE_SIZE * ATT_KV_HEADS, ATT_DH)
    ci = cache_kidx[0]
    attend_s = functools.partial(_dsa_sample_call, cache_k=ck, cache_v=cv, cache_kidx=ci, page_table=page_table,
                                 B=BS, T=LS)
    y_s, k_s, v_s, kidx_s, s_s = _mixers(
        x2d_s, gate3_s, big_a_s, big_b_s, h_s, state_conv[0], state_gdn[0], attend_s, wts, B=BS, L=LS,
        gdn_chunk=min(GDN_CHUNK, LS), gdn_hp=16, gdn_nseg=1, act_dtype=F32)
    conv_s = big_a_s[:, 0:A_CONV_CH].reshape(BS, LS, A_CONV_CH)[:, LS - (CONV_W - 1):, :]

    kv_shape_p = (1, BP, LP, ATT_KV_HEADS, ATT_DH)
    kv_shape_s = (1, BS, LS, ATT_KV_HEADS, ATT_DH)
    return (y_p, y_s,
            k_p.reshape(kv_shape_p), v_p.reshape(kv_shape_p), kidx_p.reshape(1, BP, LP, IDX_DH),
            s_p[None], conv_p[None],
            k_s.reshape(kv_shape_s), v_s.reshape(kv_shape_s), kidx_s.reshape(1, BS, LS, IDX_DH),
            s_s[None], conv_s[None])
```

```python
import functools
import math

import jax
import jax.numpy as jnp
from jax import lax
from jax.experimental import pallas as pl
from jax.experimental.pallas import tpu as pltpu

F32 = jnp.float32
BF16 = jnp.bfloat16
I32 = jnp.int32

D_MODEL = 2048
PAGE_SIZE = 128
GDN_QK_HEADS = 16
GDN_V_HEADS = 32
GDN_DK = 128
GDN_DV = 128
CONV_W = 4
GDN_CHUNK = 64
ATT_HEADS = 16
ATT_KV_HEADS = 2
ATT_DH = 128
IDX_HEADS = 16
IDX_DH = 128
TOPK_MAX = 256
NORM_EPS = 1e-6
L2_EPS = 1e-6

A_QK = GDN_QK_HEADS * GDN_DK
A_V = GDN_V_HEADS * GDN_DV
A_CONV_CH = 2 * A_QK + A_V
B_Q = ATT_HEADS * ATT_DH
B_KV = ATT_KV_HEADS * ATT_DH
IDX_Q = IDX_HEADS * IDX_DH
SPLIT_SIZES = (A_CONV_CH, A_V, GDN_V_HEADS, GDN_V_HEADS, B_Q, B_KV, B_KV, B_Q, IDX_Q, IDX_DH, IDX_HEADS,
               D_MODEL, D_MODEL)
B_SEGS = (4, 7, 8, 11, 12)
OFF_AQKV = 0
OFF_AZ = OFF_AQKV + A_CONV_CH
N_A = OFF_AZ + A_V
OFF_BQ = 0
OFF_BZ = OFF_BQ + B_Q
OFF_IQ = OFF_BZ + B_Q
OFF_GA = OFF_IQ + IDX_Q
OFF_GB = OFF_GA + D_MODEL
N_B = OFF_GB + D_MODEL
LANE_BETA = 0
LANE_DECAY = GDN_V_HEADS
LANE_IW = 2 * GDN_V_HEADS

VMEM_LIMIT = 56 * 1024 * 1024
NEG_INF = float("-inf")
INT_MIN = -2 ** 31


def _cparams(sem):
    return pltpu.CompilerParams(dimension_semantics=sem, vmem_limit_bytes=VMEM_LIMIT)


def _dot(a, b):
    return jnp.dot(a, b, preferred_element_type=F32)


def _dot_nt(a, b):
    return lax.dot_general(a, b, (((1,), (1,)), ((), ())), preferred_element_type=F32)


def _dot_tn(a, b):
    return lax.dot_general(a, b, (((0,), (0,)), ((), ())), preferred_element_type=F32)


def _split2(x):
    hi = x.astype(BF16)
    lo = (x - hi.astype(F32)).astype(BF16)
    return hi, lo


def _split3(x):
    x1 = x.astype(BF16)
    r = x - x1.astype(F32)
    x2 = r.astype(BF16)
    x3 = (r - x2.astype(F32)).astype(BF16)
    return x1, x2, x3


def _dot_3pass(a, b):
    ah, al = _split2(a)
    bh, bl = _split2(b)
    return _dot(ah, bh) + _dot(al, bh) + _dot(ah, bl)


def _silu(x):
    h = 0.5 * x
    return h + h * jnp.tanh(h)


def _mod_kernel(c_ref, w_ref, b_ref, o_ref):
    o_ref[...] = _dot_3pass(_silu(c_ref[...]), w_ref[...]) + b_ref[...]


def _mod_call(c_pad, w_ada, b_ada):
    rows, d = c_pad.shape
    n = w_ada.shape[1]
    tn = 768
    return pl.pallas_call(
        _mod_kernel,
        grid=(n // tn,),
        in_specs=[pl.BlockSpec((rows, d), lambda j: (0, 0)),
                  pl.BlockSpec((d, tn), lambda j: (0, j)),
                  pl.BlockSpec((1, tn), lambda j: (0, j))],
        out_specs=pl.BlockSpec((rows, tn), lambda j: (0, j)),
        out_shape=jax.ShapeDtypeStruct((rows, n), F32),
        compiler_params=_cparams(("arbitrary",)),
        name="adaln_mod",
    )(c_pad, w_ada, b_ada)


def _norm_kernel(x_ref, g_ref, scale_ref, shift_ref, o_ref):
    x = x_ref[...]
    y = x * lax.rsqrt(jnp.mean(x * x, axis=-1, keepdims=True) + NORM_EPS) * g_ref[...]
    o_ref[...] = (y * (1.0 + scale_ref[...]) + shift_ref[...]).astype(o_ref.dtype)


def _norm_call(x2d, g_row, scale3, shift3, tm, tiles_per_group):
    m, d = x2d.shape
    r = scale3.shape[1]
    mod_spec = pl.BlockSpec((None, r, d), lambda i: (i // tiles_per_group, 0, 0))
    return pl.pallas_call(
        _norm_kernel,
        grid=(m // tm,),
        in_specs=[pl.BlockSpec((tm, d), lambda i: (i, 0)),
                  pl.BlockSpec((1, d), lambda i: (0, 0)),
                  mod_spec, mod_spec],
        out_specs=pl.BlockSpec((tm, d), lambda i: (i, 0)),
        out_shape=jax.ShapeDtypeStruct((m, d), BF16),
        compiler_params=_cparams(("parallel",)),
        name="norm_modulate",
    )(x2d, g_row, scale3, shift3)


SEG_OFFS = tuple(sum(SPLIT_SIZES[:i]) for i in range(len(SPLIT_SIZES) + 1))


def _proj_nt_kernel(h_ref, w_ref, h2_ref, o_ref, o2_ref, *tail_ref, tail_tiles):
    w16 = w_ref[...].astype(BF16)
    acc = _dot_nt(h_ref[...], w16)
    o_ref[...] = acc.astype(o_ref.dtype)
    if tail_tiles:
        wanted = pl.program_id(1) < tail_tiles
        tail_ref[0][...] = jnp.where(wanted, acc[acc.shape[0] - 8:, :], 0.0)

    is_last = pl.program_id(0) == pl.num_programs(0) - 1

    @pl.when(is_last)
    def _():
        o2_ref[...] = _dot_nt(h2_ref[...], w16).astype(o2_ref.dtype)

    @pl.when(jnp.logical_not(is_last))
    def _():
        o2_ref[...] = jnp.zeros(o2_ref.shape, o2_ref.dtype)


def _proj_nt_call(h, h2, wt3, row_of_tile, n_cols, tm, tn, out_dtype, out2_dtype, name, tail_cols=0):
    m, k = h.shape
    m2 = h2.shape[0]
    n_i, n_j = m // tm, n_cols // tn
    tail_tiles = tail_cols // tn
    out_specs = [pl.BlockSpec((tm, tn), lambda i, j: (i, j)),
                 pl.BlockSpec((m2, tn), lambda i, j: (0, jnp.where(i == n_i - 1, j, n_j)))]
    out_shape = [jax.ShapeDtypeStruct((m, n_cols), out_dtype),
                 jax.ShapeDtypeStruct((m2, n_cols + tn), out2_dtype)]
    if tail_tiles:
        out_specs.append(pl.BlockSpec((None, 8, tn), lambda i, j: (i, 0, jnp.minimum(j, tail_tiles))))
        out_shape.append(jax.ShapeDtypeStruct((n_i, 8, tail_cols + tn), F32))
    return pl.pallas_call(
        functools.partial(_proj_nt_kernel, tail_tiles=tail_tiles),
        grid=(n_i, n_j),
        in_specs=[pl.BlockSpec((tm, k), lambda i, j: (i, 0)),
                  pl.BlockSpec((pl.Element(tn), pl.Element(k)),
                               lambda i, j: (pl.multiple_of(row_of_tile(j), 16), 0)),
                  pl.BlockSpec((m2, k), lambda i, j: (0, 0))],
        out_specs=out_specs,
        out_shape=out_shape,
        compiler_params=_cparams(("arbitrary", "arbitrary")),
        name=name,
    )(h, wt3, h2)


def _group_b_row(j, tn):
    per = B_Q // tn
    seg = j // per
    start = SEG_OFFS[B_SEGS[-1]]
    for s in range(len(B_SEGS) - 2, -1, -1):
        start = jnp.where(seg == s, SEG_OFFS[B_SEGS[s]], start)
    return start + (j % per) * tn


def _proj_tail_kernel(h_ref, wkv_ref, wik_ref, wg_ref, wiw_ref, k_ref, v_ref, kidx_ref, small_ref):
    h = h_ref[...]
    kv = _dot_nt(h, wkv_ref[...].astype(BF16))
    rows = h.shape[0]
    for n in range(ATT_KV_HEADS):
        head_rows = pl.ds(n, rows, stride=ATT_KV_HEADS)
        k_ref[head_rows, :] = kv[:, n * ATT_DH:(n + 1) * ATT_DH]
        v_ref[head_rows, :] = kv[:, B_KV + n * ATT_DH:B_KV + (n + 1) * ATT_DH]
    kidx_ref[...] = _dot_nt(h, wik_ref[...].astype(BF16))
    gate = _dot_nt(h, wg_ref[...].astype(BF16))
    iw = _dot_nt(h, wiw_ref[...].astype(BF16))
    lane = lax.broadcasted_iota(I32, gate.shape, 1)
    small_ref[...] = jnp.where(lane < LANE_IW, gate, jnp.where(lane < LANE_IW + IDX_HEADS, iw, 0.0))


def _proj_tail_call(h, wt3, tm):
    m, k = h.shape
    kvh = ATT_KV_HEADS
    out_blocks = ((kvh * tm, ATT_DH), (kvh * tm, ATT_DH), (tm, IDX_DH), (tm, 128))
    out_arrays = ((kvh * m, ATT_DH), (kvh * m, ATT_DH), (m, IDX_DH), (m, 128))
    beta0, kv0, ik0, iw0 = SEG_OFFS[2], SEG_OFFS[5], SEG_OFFS[9], SEG_OFFS[10]
    assert SEG_OFFS[3] - beta0 == LANE_DECAY and SEG_OFFS[4] - beta0 == LANE_IW and SEG_OFFS[6] - kv0 == B_KV

    def rows(start, size):
        return pl.BlockSpec((pl.Element(size), pl.Element(k)), lambda i: (start, 0))

    return pl.pallas_call(
        _proj_tail_kernel,
        grid=(m // tm,),
        in_specs=[pl.BlockSpec((tm, k), lambda i: (i, 0)),
                  rows(kv0, 2 * B_KV), rows(ik0, IDX_DH), rows(beta0, 128), rows(iw0 - LANE_IW, 128)],
        out_specs=[pl.BlockSpec(blk, lambda i: (i, 0)) for blk in out_blocks],
        out_shape=[jax.ShapeDtypeStruct(shp, F32) for shp in out_arrays],
        compiler_params=_cparams(("parallel",)),
        name="in_proj_tail",
    )(h, wt3, wt3, wt3, wt3)


def _gates_kernel(sm_ref, alog_ref, dtb_ref, o_ref, *, chunk):
    x = sm_ref[...]
    tm = x.shape[0]
    lane = lax.broadcasted_iota(I32, x.shape, 1)
    beta = jax.nn.sigmoid(x)
    z = x + dtb_ref[...]
    softplus = jnp.maximum(z, 0.0) + jnp.log1p(jnp.exp(-jnp.abs(z)))
    g = -jnp.exp(alog_ref[...]) * softplus
    g = jnp.where((lane >= LANE_DECAY) & (lane < LANE_DECAY + GDN_V_HEADS), g, 0.0)
    ri = lax.broadcasted_iota(I32, (tm, tm), 0)
    ci = lax.broadcasted_iota(I32, (tm, tm), 1)
    shift = int(math.log2(chunk))
    tri = jnp.where((ri >= ci) & ((ri >> shift) == (ci >> shift)), 1.0, 0.0).astype(BF16)
    g1, g2, g3 = _split3(g)
    gc = _dot(tri, g1) + _dot(tri, g2) + _dot(tri, g3)
    o_ref[...] = jnp.where(lane < GDN_V_HEADS, beta, gc)


def _gates_call(small, alog_row, dtb_row, chunk):
    m = small.shape[0]
    tm = 256
    return pl.pallas_call(
        functools.partial(_gates_kernel, chunk=chunk),
        grid=(m // tm,),
        in_specs=[pl.BlockSpec((tm, 128), lambda i: (i, 0)),
                  pl.BlockSpec((1, 128), lambda i: (0, 0)),
                  pl.BlockSpec((1, 128), lambda i: (0, 0))],
        out_specs=pl.BlockSpec((tm, 128), lambda i: (i, 0)),
        out_shape=jax.ShapeDtypeStruct((m, 128), F32),
        compiler_params=_cparams(("parallel",)),
        name="gdn_gates",
    )(small, alog_row, dtb_row)


def _unrolled_loop(n, unroll, fn):
    unroll = min(unroll, n)
    assert n % unroll == 0

    def body(t, carry):
        for u in range(unroll):
            fn(t * unroll + u)
        return carry

    lax.fori_loop(0, n // unroll, body, 0)


def _gdn_kernel(*refs, L, C, HP, NSEG, has_state):
    ins = 13 if has_state else 12
    (q_ref, k_ref, v_ref, z_ref, bq_ref, bk_ref, bv_ref, wq_ref, wk_ref, wv_ref, gates_ref, gn_ref) = refs[:12]
    s0_ref = refs[12] if has_state else None
    (y_ref, sout_ref, xq_s, xk_s, xv_s, beta_s, g_s, st_s, a_s, p_s, rhs_s, kt_s, qkd_s, qd_s,
     m_s, n_s, qe_s, o_s) = refs[ins:]
    nh = 2 * HP
    nc = L // C
    hq0 = pl.program_id(1) * HP

    def first_segment():
        xq_s[0:8, :] = bq_ref[...]
        xk_s[0:8, :] = bk_ref[...]
        xv_s[0:8, :] = bv_ref[...]
        if has_state:
            st_s[...] = s0_ref[...]
        else:
            st_s[...] = jnp.zeros(st_s.shape, F32)

    def next_segment():
        xq_s[0:8, :] = xq_s[L:L + 8, :]
        xk_s[0:8, :] = xk_s[L:L + 8, :]
        xv_s[0:8, :] = xv_s[L:L + 8, :]

    if NSEG == 1:
        first_segment()
    else:
        pl.when(pl.program_id(2) == 0)(first_segment)
        pl.when(pl.program_id(2) > 0)(next_segment)
    xq_s[8:L + 8, :] = q_ref[...].astype(F32)
    xk_s[8:L + 8, :] = k_ref[...].astype(F32)
    xv_s[8:L + 8, :] = v_ref[...].astype(F32)

    gates = gates_ref[...]
    lane = lax.broadcasted_iota(I32, gates.shape, 1)
    for idx in range(nh):
        hv = 2 * hq0 + idx
        bcol = jnp.sum(jnp.where(lane == hv + LANE_BETA, gates, 0.0), axis=-1, keepdims=True)
        gcol = jnp.sum(jnp.where(lane == hv + LANE_DECAY, gates, 0.0), axis=-1, keepdims=True)
        beta_s[idx] = jnp.broadcast_to(bcol, (L, 128))
        g_s[idx] = jnp.broadcast_to(gcol, (L, 128))

    ii = lax.broadcasted_iota(I32, (C, C), 0)
    jj = lax.broadcasted_iota(I32, (C, C), 1)
    gn = gn_ref[...]

    def conv(x_s, r0, cols, w):
        win = x_s[pl.ds(r0, C + 8), cols]
        acc = (win[5:5 + C] * w[0:1] + win[6:6 + C] * w[1:2] + win[7:7 + C] * w[2:3] + win[8:8 + C] * w[3:4])
        return _silu(acc)

    def l2n(x):
        return x * lax.rsqrt(jnp.sum(x * x, axis=-1, keepdims=True) + L2_EPS)

    def lanes(i, width):
        if isinstance(i, int):
            return slice(i * width, (i + 1) * width)
        return pl.ds(pl.multiple_of(i * width, width), width)

    def pair_terms(j):
        c, p = (j, 0) if HP == 1 else (j // HP, j % HP)
        r0 = pl.multiple_of(c * C, C)
        rows = pl.ds(r0, C)
        lq, lv = lanes(p, 128), lanes(p, 256)
        q = l2n(conv(xq_s, r0, lq, wq_ref[:, lq])) * (GDN_DK ** -0.5)
        k = l2n(conv(xk_s, r0, lq, wk_ref[:, lq]))
        vc = conv(xv_s, r0, lv, wv_ref[:, lv])
        k16 = k.astype(BF16)
        kk = _dot_nt(k16, k16)
        qk = _dot_nt(q.astype(BF16), k16)
        for r in range(2):
            idx = 2 * p + r
            e = c * nh + idx
            gb = g_s[idx, rows, :]
            bb = beta_s[idx, rows, :]
            e_g = jnp.exp(gb)
            e_tail = jnp.exp(gb[C - 1:C, :] - gb)
            if C < 128:
                gpad = jnp.concatenate([gb, jnp.zeros((128 - C, 128), F32)], axis=0)
            else:
                gpad = gb
            dexp = gb[:, 0:C] - gpad.T[0:C, 0:C]
            dm = jnp.exp(jnp.where(ii >= jj, dexp, NEG_INF))
            a = jnp.where(ii > jj, kk * dm, 0.0) * bb[:, 0:C]
            a_s[e] = a.astype(a_s.dtype)
            p_s[e] = jnp.where(ii == jj, 1.0, 0.0) - a
            rhs_s[e] = jnp.concatenate([vc[:, r * 128:(r + 1) * 128] * bb, k * bb * e_g], axis=1).astype(rhs_s.dtype)
            kt_s[e] = (k * e_tail).astype(kt_s.dtype)
            qkd_s[e] = (qk * dm).astype(qkd_s.dtype)
            qd_s[e] = q * e_g

    def square(e):
        ak = a_s[e].astype(BF16)
        a_s[e] = _dot(ak, ak).astype(a_s.dtype)

    def inverse_stage(e, last):
        ak = a_s[e].astype(BF16)
        pk = p_s[e]
        p_s[e] = pk + _dot(pk.astype(BF16), ak)
        if not last:
            a_s[e] = _dot(ak, ak).astype(a_s.dtype)

    def solve(e):
        rhs_s[e] = _dot(p_s[e].astype(BF16), rhs_s[e].astype(BF16)).astype(rhs_s.dtype)

    def state_terms(e):
        sol16 = rhs_s[e].astype(BF16)
        mn = _dot_tn(kt_s[e].astype(BF16), sol16)
        qo = _dot(qkd_s[e].astype(BF16), sol16)
        n_s[e] = mn[:, 0:GDN_DV]
        m_s[e] = mn[:, GDN_DV:].astype(m_s.dtype)
        o_s[e] = qo[:, 0:GDN_DV]
        qe_s[e] = (qd_s[e] - qo[:, GDN_DV:]).astype(qe_s.dtype)

    def head_step(c, idx):
        e = c * nh + idx
        r0 = pl.multiple_of(c * C, C)
        rows = pl.ds(r0, C)
        s = st_s[idx]
        s16 = s.astype(BF16)
        o = _dot(qe_s[e].astype(BF16), s16) + o_s[e]
        e_last = jnp.exp(g_s[idx, pl.ds(r0 + C - 1, 1), :])
        st_s[idx] = s * e_last - _dot(m_s[e].astype(BF16), s16) + n_s[e]
        on = o * lax.rsqrt(jnp.mean(o * o, axis=-1, keepdims=True) + NORM_EPS) * gn
        zz = z_ref[rows, lanes(idx, 128)].astype(F32)
        y_ref[rows, lanes(idx, 128)] = (on * _silu(zz)).astype(y_ref.dtype)

    ne = nc * nh
    stages = int(math.log2(C)) - 1
    _unrolled_loop(nc * HP, 2 if C >= 64 else 4, pair_terms)
    _unrolled_loop(ne, 32, square)
    for st in range(stages):
        _unrolled_loop(ne, 32, functools.partial(inverse_stage, last=(st == stages - 1)))
    _unrolled_loop(ne, 16, solve)
    _unrolled_loop(ne, 16, state_terms)

    def chunk_step(c, carry):
        if nh <= 4:
            for idx in range(nh):
                head_step(c, idx)
        else:
            _unrolled_loop(nh, 16, functools.partial(head_step, c))
        return carry

    lax.fori_loop(0, nc, chunk_step, 0)
    sout_ref[...] = st_s[...]


def _gdn_call(big, convbuf8, conv_w, gates, gn_row, s0, *, B, L, C, HP, NSEG, y_dtype):
    has_state = s0 is not None
    ls = L // NSEG
    ne = (ls // C) * 2 * HP
    half = BF16 if C % 16 == 0 else F32
    nq = GDN_QK_HEADS // HP
    wq, wv = HP * 128, HP * 256
    kq0 = A_QK // wq
    v0 = (2 * A_QK) // wv
    z0 = OFF_AZ // wv
    in_specs = [
        pl.BlockSpec((ls, wq), lambda b, h, s: (b * NSEG + s, h)),
        pl.BlockSpec((ls, wq), lambda b, h, s: (b * NSEG + s, kq0 + h)),
        pl.BlockSpec((ls, wv), lambda b, h, s: (b * NSEG + s, v0 + h)),
        pl.BlockSpec((ls, wv), lambda b, h, s: (b * NSEG + s, z0 + h)),
        pl.BlockSpec((None, 8, wq), lambda b, h, s: (b, 0, h)),
        pl.BlockSpec((None, 8, wq), lambda b, h, s: (b, 0, kq0 + h)),
        pl.BlockSpec((None, 8, wv), lambda b, h, s: (b, 0, v0 + h)),
        pl.BlockSpec((CONV_W, wq), lambda b, h, s: (0, h)),
        pl.BlockSpec((CONV_W, wq), lambda b, h, s: (0, kq0 + h)),
        pl.BlockSpec((CONV_W, wv), lambda b, h, s: (0, v0 + h)),
        pl.BlockSpec((ls, 128), lambda b, h, s: (b * NSEG + s, 0)),
        pl.BlockSpec((1, 128), lambda b, h, s: (0, 0)),
    ]
    args = [big, big, big, big, convbuf8, convbuf8, convbuf8, conv_w, conv_w, conv_w, gates, gn_row]
    if has_state:
        in_specs.append(pl.BlockSpec((None, 2 * HP, GDN_DK, GDN_DV), lambda b, h, s: (b, h, 0, 0)))
        args.append(s0)
    return pl.pallas_call(
        functools.partial(_gdn_kernel, L=ls, C=C, HP=HP, NSEG=NSEG, has_state=has_state),
        grid=(B, nq, NSEG),
        in_specs=in_specs,
        out_specs=[pl.BlockSpec((ls, wv), lambda b, h, s: (b * NSEG + s, h)),
                   pl.BlockSpec((None, 2 * HP, GDN_DK, GDN_DV), lambda b, h, s: (b, h, 0, 0))],
        out_shape=[jax.ShapeDtypeStruct((B * L, A_V), y_dtype),
                   jax.ShapeDtypeStruct((B, GDN_V_HEADS, GDN_DK, GDN_DV), F32)],
        scratch_shapes=[pltpu.VMEM((ls + 8, wq), F32), pltpu.VMEM((ls + 8, wq), F32), pltpu.VMEM((ls + 8, wv), F32),
                        pltpu.VMEM((2 * HP, ls, 128), F32), pltpu.VMEM((2 * HP, ls, 128), F32),
                        pltpu.VMEM((2 * HP, GDN_DK, GDN_DV), F32),
                        pltpu.VMEM((ne, C, C), half), pltpu.VMEM((ne, C, C), F32),
                        pltpu.VMEM((ne, C, GDN_DV + GDN_DK), half), pltpu.VMEM((ne, C, GDN_DK), half),
                        pltpu.VMEM((ne, C, C), half), pltpu.VMEM((ne, C, GDN_DK), F32),
                        pltpu.VMEM((ne, GDN_DK, GDN_DV), BF16), pltpu.VMEM((ne, GDN_DK, GDN_DV), F32),
                        pltpu.VMEM((ne, C, GDN_DK), half), pltpu.VMEM((ne, C, GDN_DV), F32)],
        compiler_params=_cparams(("parallel", "arbitrary", "arbitrary")),
        name="gdn_prompt" if not has_state else "gdn_sample",
    )(*args)


KEY_NEG_INF = -2 ** 31 + 0x7FFFFF


def _key_to_float(key):
    bits = key ^ ((key >> 31) & jnp.int32(0x7FFFFFFF))
    return pltpu.bitcast(bits, F32)


def _count_true(mask):
    ones = jnp.where(mask, 1.0, 0.0)
    parts = [ones[:, i * 128:(i + 1) * 128] for i in range(mask.shape[1] // 128)]
    while len(parts) > 1:
        parts = [a + b for a, b in zip(parts[0::2], parts[1::2])] + ([parts[-1]] if len(parts) % 2 else [])
    return jnp.sum(parts[0], axis=-1, keepdims=True)


def _selection_bias(score_ref, bias_ref, qpos0, rows, width, topk, bits=1):
    def step(it, t):
        best = t
        for digit in range(1, 2 ** bits):
            cand = t + jnp.left_shift(jnp.int32(digit), 32 - bits * (it + 1))
            cnt = _count_true(score_ref[:, 0:width] >= _key_to_float(cand))
            best = jnp.where((cnt >= float(topk)) | (cand <= KEY_NEG_INF), cand, best)
        return best

    thr = _key_to_float(lax.fori_loop(0, 32 // bits, step, jnp.full((rows, 1), INT_MIN, I32)))
    score = score_ref[:, 0:width]
    causal = (lax.broadcasted_iota(I32, (rows, width), 1)
              <= qpos0 + lax.broadcasted_iota(I32, (rows, width), 0))
    sel = (score >= thr) & causal
    bias_ref[:, 0:width] = jnp.where(sel, 0.0, NEG_INF)
    n_sel = _count_true(sel)
    has_ties = jnp.max(n_sel) > float(topk)

    @pl.when(has_ties)
    def _():
        need = float(topk) - _count_true(score > thr)
        eq16 = jnp.where(score == thr, 1.0, 0.0).astype(BF16)

        def tile(j, carry):
            c0 = pl.multiple_of(j * 128, 128)
            ri = lax.broadcasted_iota(I32, (width, 128), 0)
            ci = lax.broadcasted_iota(I32, (width, 128), 1) + c0
            upper = jnp.where(ri <= ci, 1.0, 0.0).astype(BF16)
            prefix = _dot(eq16, upper)
            s_t = score_ref[:, pl.ds(c0, 128)]
            keep = (s_t > thr) | ((s_t == thr) & (prefix <= need))
            old = bias_ref[:, pl.ds(c0, 128)]
            bias_ref[:, pl.ds(c0, 128)] = jnp.where(keep, old, NEG_INF)
            return carry

        lax.fori_loop(0, width // 128, tile, 0)


SOFTMAX_LOG2_SCALE = (ATT_DH ** -0.5) * math.log2(math.e)


def _dsa_prompt_kernel(qi_ref, wi_ref, kidx_ref, q_ref, k_ref, v_ref, z_ref, o_ref,
                       score_s, bias_s, kidx16_s, k16_s, vext_s, m_s, acc_s, *, QB, SUB, KT, L, topk):
    i = pl.program_id(1)
    q0 = i * QB
    nt = (q0 + QB - 1) // KT + 1
    groups = ATT_HEADS // ATT_KV_HEADS

    @pl.when(i == 0)
    def _():
        kidx16_s[...] = kidx_ref[...].astype(BF16)
        for n in range(ATT_KV_HEADS):
            head_rows = pl.ds(n, L, stride=ATT_KV_HEADS)
            k16_s[:, n * ATT_DH:(n + 1) * ATT_DH] = k_ref[head_rows, :].astype(BF16)
            vext_s[n, :, 0:ATT_DH] = v_ref[head_rows, :].astype(BF16)
            vext_s[n, :, ATT_DH:] = jnp.ones((L, ATT_DH), BF16)

    rowpos = q0 + lax.broadcasted_iota(I32, (SUB, KT), 0)
    col = lax.broadcasted_iota(I32, (SUB, KT), 1)

    def tile_rows(t):
        return pl.ds(pl.multiple_of(t * KT, KT), KT)

    subs = [slice(r, r + SUB) for r in range(0, QB, SUB)]

    def index_tile(t, carry):
        kid = kidx16_s[tile_rows(t), :]
        for rs in subs:
            wi = wi_ref[rs, :] * ((IDX_DH ** -0.5) * (IDX_HEADS ** -0.5))
            acc = jnp.zeros((SUB, KT), F32)
            for h in range(IDX_HEADS):
                lg = _dot_nt(qi_ref[rs, h * IDX_DH:(h + 1) * IDX_DH], kid)
                acc = acc + jnp.maximum(lg, 0.0) * wi[:, LANE_IW + h:LANE_IW + h + 1]
            causal = col + t * KT <= rowpos + rs.start
            score_s[rs, tile_rows(t)] = jnp.where(causal, acc, NEG_INF)
        return carry

    lax.fori_loop(0, nt, index_tile, 0)
    for w in range(1, L // KT + 1):
        @pl.when(nt == w)
        def _():
            _selection_bias(score_s, bias_s, q0, QB, w * KT, topk)

    m_s[...] = jnp.full(m_s.shape, float(jnp.finfo(F32).min), F32)
    acc_s[...] = jnp.zeros(acc_s.shape, F32)

    def attend_tile(t, carry):
        for n in range(ATT_KV_HEADS):
            kn = k16_s[tile_rows(t), n * ATT_DH:(n + 1) * ATT_DH]
            vn = vext_s[n, tile_rows(t), :]
            for g in range(groups):
                h = n * groups + g
                for rs in subs:
                    sb = _dot_nt(q_ref[rs, h * ATT_DH:(h + 1) * ATT_DH], kn) + bias_s[rs, tile_rows(t)]
                    m_old = m_s[h, rs, :]
                    m_new = jnp.maximum(m_old, jnp.max(sb, axis=-1, keepdims=True))
                    p = jnp.exp2((sb - jnp.concatenate([m_new] * (KT // 128), axis=1)) * SOFTMAX_LOG2_SCALE)
                    alpha = jnp.exp2((m_old - m_new) * SOFTMAX_LOG2_SCALE)
                    m_s[h, rs, :] = m_new
                    acc_s[h, rs, :] = (acc_s[h, rs, :] * jnp.concatenate([alpha, alpha], axis=1)
                                       + _dot(p.astype(BF16), vn))
        return carry

    lax.fori_loop(0, nt, attend_tile, 0)
    for h in range(ATT_HEADS):
        cols = slice(h * ATT_DH, (h + 1) * ATT_DH)
        zz = z_ref[:, cols].astype(F32)
        acc = acc_s[h]
        o_ref[:, cols] = (acc[:, 0:ATT_DH] / acc[:, ATT_DH:] * _silu(zz)).astype(o_ref.dtype)


def _dsa_prompt_call(big, small, kidx, k, v, *, B, L, QB):
    topk = min(TOPK_MAX, L // 4)
    nqb = L // QB
    iq0 = OFF_IQ // B_Q
    bq0 = OFF_BQ // B_Q
    bz0 = OFF_BZ // B_Q
    return pl.pallas_call(
        functools.partial(_dsa_prompt_kernel, QB=QB, SUB=128, KT=256, L=L, topk=topk),
        grid=(B, nqb),
        in_specs=[pl.BlockSpec((QB, IDX_Q), lambda b, i: (b * nqb + i, iq0)),
                  pl.BlockSpec((QB, 128), lambda b, i: (b * nqb + i, 0)),
                  pl.BlockSpec((L, IDX_DH), lambda b, i: (b, 0)),
                  pl.BlockSpec((QB, B_Q), lambda b, i: (b * nqb + i, bq0)),
                  pl.BlockSpec((ATT_KV_HEADS * L, ATT_DH), lambda b, i: (b, 0)),
                  pl.BlockSpec((ATT_KV_HEADS * L, ATT_DH), lambda b, i: (b, 0)),
                  pl.BlockSpec((QB, B_Q), lambda b, i: (b * nqb + i, bz0))],
        out_specs=pl.BlockSpec((QB, B_Q), lambda b, i: (b * nqb + i, 0)),
        out_shape=jax.ShapeDtypeStruct((B * L, B_Q), BF16),
        scratch_shapes=[pltpu.VMEM((QB, L), F32), pltpu.VMEM((QB, L), F32),
                        pltpu.VMEM((L, IDX_DH), BF16), pltpu.VMEM((L, B_KV), BF16),
                        pltpu.VMEM((ATT_KV_HEADS, L, 2 * ATT_DH), BF16),
                        pltpu.VMEM((ATT_HEADS, QB, 128), F32), pltpu.VMEM((ATT_HEADS, QB, 2 * ATT_DH), F32)],
        compiler_params=_cparams(("parallel", "arbitrary")),
        name="dsa_prompt",
    )(big, small, kidx, big, k, v, big)


def _dsa_sample_kernel(pt_ref, qi_ref, wi_ref, kin_ref, q_ref, kn_ref, vn_ref, z_ref, ck_hbm, cv_hbm, ci_hbm,
                       o_ref, kbuf, vbuf, ibuf, sems, score_s, bias_s, *, NP, T, topk):
    b = pl.program_id(0)
    nb = pl.num_programs(0)
    slot = b % 2
    past = NP * PAGE_SIZE
    width = past + 128

    kvp = ATT_KV_HEADS * PAGE_SIZE

    def page_copies(bb, sl, p):
        page = pt_ref[bb * NP + p]
        rows = pl.ds(pl.multiple_of(p * PAGE_SIZE, PAGE_SIZE), PAGE_SIZE)
        src2 = pl.ds(pl.multiple_of(page * kvp, kvp), kvp)
        dst2 = pl.ds(pl.multiple_of(p * kvp, kvp), kvp)
        return (pltpu.make_async_copy(ck_hbm.at[src2], kbuf.at[sl, dst2], sems.at[0, sl]),
                pltpu.make_async_copy(cv_hbm.at[src2], vbuf.at[sl, dst2], sems.at[1, sl]),
                pltpu.make_async_copy(ci_hbm.at[page], ibuf.at[sl, rows], sems.at[2, sl]))

    def start_fetch(bb, sl):
        def body(p, carry):
            for cp in page_copies(bb, sl, p):
                cp.start()
            return carry
        lax.fori_loop(0, NP, body, 0, unroll=8)

    def wait_fetch(bb, sl):
        def body(p, carry):
            for cp in page_copies(bb, sl, p):
                cp.wait()
            return carry
        lax.fori_loop(0, NP, body, 0, unroll=8)

    @pl.when(b == 0)
    def _():
        for sl in range(2):
            kbuf[sl, ATT_KV_HEADS * past:, :] = jnp.zeros((ATT_KV_HEADS * 128, ATT_DH), F32)
            vbuf[sl, ATT_KV_HEADS * past:, :] = jnp.zeros((ATT_KV_HEADS * 128, ATT_DH), F32)
            ibuf[sl, past:, :] = jnp.zeros((128, IDX_DH), F32)
        start_fetch(0, 0)

    @pl.when(b + 1 < nb)
    def _():
        start_fetch(b + 1, 1 - slot)

    new_rows = pl.ds(ATT_KV_HEADS * past, ATT_KV_HEADS * T)
    kbuf[slot, new_rows, :] = kn_ref[...]
    vbuf[slot, new_rows, :] = vn_ref[...]
    ibuf[slot, past:past + T, :] = kin_ref[...]
    wait_fetch(b, slot)

    qi = qi_ref[...]
    qis = jnp.concatenate([qi[:, h * IDX_DH:(h + 1) * IDX_DH] for h in range(IDX_HEADS)], axis=0).astype(BF16)
    wi = wi_ref[...] * ((IDX_DH ** -0.5) * (IDX_HEADS ** -0.5))
    lg = _dot_nt(qis, ibuf[slot].astype(BF16))
    score = jnp.zeros((T, width), F32)
    for h in range(IDX_HEADS):
        score = score + jnp.maximum(lg[h * T:(h + 1) * T, :], 0.0) * wi[:, LANE_IW + h:LANE_IW + h + 1]
    qpos = past + lax.broadcasted_iota(I32, (T, width), 0)
    kpos = lax.broadcasted_iota(I32, (T, width), 1)
    causal = kpos <= qpos
    score_s[...] = jnp.where(causal, score, NEG_INF)
    _selection_bias(score_s, bias_s, past, T, width, topk, bits=2)

    groups = ATT_HEADS // ATT_KV_HEADS
    q = q_ref[...]
    bias = jnp.concatenate([bias_s[...]] * groups, axis=0)
    for n in range(ATT_KV_HEADS):
        head_rows = pl.ds(n, width, stride=ATT_KV_HEADS)
        kn = kbuf[slot, head_rows, :].astype(BF16)
        vn = vbuf[slot, head_rows, :].astype(BF16)
        qs = jnp.concatenate([q[:, (n * groups + g) * ATT_DH:(n * groups + g + 1) * ATT_DH] for g in range(groups)],
                             axis=0).astype(BF16)
        s = _dot_nt(qs, kn) * (ATT_DH ** -0.5) + bias
        m = jnp.max(s, axis=-1, keepdims=True)
        p = jnp.exp(s - m)
        denom = jnp.sum(p, axis=-1, keepdims=True)
        o = _dot(p.astype(BF16), vn) / denom
        for g in range(groups):
            h = n * groups + g
            cols = slice(h * ATT_DH, (h + 1) * ATT_DH)
            zz = z_ref[:, cols]
            o_ref[:, cols] = o[g * T:(g + 1) * T, :] * _silu(zz)


def _dsa_sample_call(big, small, kidx, k, v, cache_k, cache_v, cache_kidx, page_table, *, B, T):
    npages = page_table.shape[1]
    past = npages * PAGE_SIZE
    topk = min(TOPK_MAX, (past + T) // 4)
    width = past + 128
    iq0 = OFF_IQ // B_Q
    bq0 = OFF_BQ // B_Q
    bz0 = OFF_BZ // B_Q
    grid_spec = pltpu.PrefetchScalarGridSpec(
        num_scalar_prefetch=1,
        grid=(B,),
        in_specs=[pl.BlockSpec((T, IDX_Q), lambda b, pt: (b, iq0)),
                  pl.BlockSpec((T, 128), lambda b, pt: (b, 0)),
                  pl.BlockSpec((T, IDX_DH), lambda b, pt: (b, 0)),
                  pl.BlockSpec((T, B_Q), lambda b, pt: (b, bq0)),
                  pl.BlockSpec((ATT_KV_HEADS * T, ATT_DH), lambda b, pt: (b, 0)),
                  pl.BlockSpec((ATT_KV_HEADS * T, ATT_DH), lambda b, pt: (b, 0)),
                  pl.BlockSpec((T, B_Q), lambda b, pt: (b, bz0)),
                  pl.BlockSpec(memory_space=pl.ANY),
                  pl.BlockSpec(memory_space=pl.ANY),
                  pl.BlockSpec(memory_space=pl.ANY)],
        out_specs=pl.BlockSpec((T, B_Q), lambda b, pt: (b, 0)),
        scratch_shapes=[pltpu.VMEM((2, ATT_KV_HEADS * width, ATT_DH), F32),
                        pltpu.VMEM((2, ATT_KV_HEADS * width, ATT_DH), F32),
                        pltpu.VMEM((2, width, IDX_DH), F32), pltpu.SemaphoreType.DMA((3, 2)),
                        pltpu.VMEM((T, width), F32), pltpu.VMEM((T, width), F32)])
    return pl.pallas_call(
        functools.partial(_dsa_sample_kernel, NP=npages, T=T, topk=topk),
        grid_spec=grid_spec,
        out_shape=jax.ShapeDtypeStruct((B * T, B_Q), F32),
        compiler_params=_cparams(("arbitrary",)),
        name="dsa_sample",
    )(page_table.reshape(-1), big, small, kidx, big, k, v, big, cache_k, cache_v, cache_kidx)


def _merge_kernel(ya_ref, yb_ref, ga_ref, gb_ref, wpa_ref, wpb_ref, o_ref):
    pa = _dot(ya_ref[...].astype(BF16), wpa_ref[...])
    pb = _dot(yb_ref[...].astype(BF16), wpb_ref[...])
    o = jax.nn.sigmoid(ga_ref[...].astype(F32)) * pa + jax.nn.sigmoid(gb_ref[...].astype(F32)) * pb
    o_ref[...] = o.astype(o_ref.dtype)


def _merge_call(ya, yb, big, wpa, wpb, tm, tn):
    m = ya.shape[0]
    n = wpa.shape[1]
    ga0 = OFF_GA // tn
    gb0 = OFF_GB // tn
    return pl.pallas_call(
        _merge_kernel,
        grid=(m // tm, n // tn),
        in_specs=[pl.BlockSpec((tm, A_V), lambda i, j: (i, 0)),
                  pl.BlockSpec((tm, B_Q), lambda i, j: (i, 0)),
                  pl.BlockSpec((tm, tn), lambda i, j: (i, ga0 + j)),
                  pl.BlockSpec((tm, tn), lambda i, j: (i, gb0 + j)),
                  pl.BlockSpec((A_V, tn), lambda i, j: (0, j)),
                  pl.BlockSpec((B_Q, tn), lambda i, j: (0, j))],
        out_specs=pl.BlockSpec((tm, tn), lambda i, j: (i, j)),
        out_shape=jax.ShapeDtypeStruct((m, n), BF16),
        compiler_params=_cparams(("parallel", "arbitrary")),
        name="merge_proj",
    )(ya, yb, big, big, wpa, wpb)


def _final_kernel(m_ref, w_ref, x_ref, gate_ref, g_ref, o_ref):
    t = _dot(m_ref[...], w_ref[...])
    y = t * lax.rsqrt(jnp.mean(t * t, axis=-1, keepdims=True) + NORM_EPS) * g_ref[...]
    o_ref[...] = x_ref[...] + gate_ref[...] * y


def _final_call(merged, w_out, x2d, gate3, g_row, tm, tiles_per_group):
    m, d = x2d.shape
    r = gate3.shape[1]
    return pl.pallas_call(
        _final_kernel,
        grid=(m // tm,),
        in_specs=[pl.BlockSpec((tm, d), lambda i: (i, 0)),
                  pl.BlockSpec((d, d), lambda i: (0, 0)),
                  pl.BlockSpec((tm, d), lambda i: (i, 0)),
                  pl.BlockSpec((None, r, d), lambda i: (i // tiles_per_group, 0, 0)),
                  pl.BlockSpec((1, d), lambda i: (0, 0))],
        out_specs=pl.BlockSpec((tm, d), lambda i: (i, 0)),
        out_shape=jax.ShapeDtypeStruct((m, d), F32),
        compiler_params=_cparams(("parallel",)),
        name="out_proj",
    )(merged, w_out, x2d, gate3, g_row)


def _cast_kernel(x_ref, o_ref):
    o_ref[...] = x_ref[...].astype(o_ref.dtype)


def _cast_bf16_call(w3):
    _, r, n = w3.shape
    tr = 256
    return pl.pallas_call(
        _cast_kernel,
        grid=(r // tr,),
        in_specs=[pl.BlockSpec((None, tr, n), lambda i: (0, i, 0))],
        out_specs=pl.BlockSpec((tr, n), lambda i: (i, 0)),
        out_shape=jax.ShapeDtypeStruct((r, n), BF16),
        compiler_params=_cparams(("parallel",)),
        name="w_cast",
    )(w3)


def _lane_row(vec, lane0):
    return jnp.zeros((1, 128), F32).at[0, lane0:lane0 + vec.shape[0]].set(vec.astype(F32))


def _modulated_input(x, mod, pre_g):
    B, L, D = x.shape
    M = B * L
    x2d = x.reshape(M, D)
    shift, scale, gate = mod[:, 0:D], mod[:, D:2 * D], mod[:, 2 * D:3 * D]
    if L >= 256:
        tm_n = 256
        per = L // tm_n
        scale3, shift3, gate3 = scale[:, None, :], shift[:, None, :], gate[:, None, :]
    else:
        tm_n = M
        per = 1
        scale3, shift3, gate3 = (jnp.repeat(t, L, axis=0)[None] for t in (scale, shift, gate))
    return _norm_call(x2d, pre_g, scale3, shift3, tm_n, per), x2d, gate3


def _mixers(x2d, gate3, big_a, big_b, h, conv_hist, s0, attend, wts, *, B, L, gdn_chunk, gdn_hp, gdn_nseg,
            act_dtype):
    (wt3, conv_w, alog_row, dtb_row, gn_row, wpa, wpb, wout, post_g) = wts
    M = B * L
    k, v, kidx, small = _proj_tail_call(h, wt3, min(M, 1024))

    gates = _gates_call(small, alog_row, dtb_row, gdn_chunk)
    if conv_hist is None:
        convbuf8 = jnp.zeros((B, 8, A_CONV_CH), F32)
    else:
        convbuf8 = jnp.concatenate([jnp.zeros((B, 5, A_CONV_CH), F32), conv_hist.astype(F32)], axis=1)
    y_a, s_new = _gdn_call(big_a, convbuf8, conv_w, gates, gn_row, s0, B=B, L=L, C=gdn_chunk, HP=gdn_hp,
                           NSEG=gdn_nseg, y_dtype=act_dtype)
    y_b = attend(big_b, small, kidx, k, v)
    tm_m = min(1024, M)
    merged = _merge_call(y_a, y_b, big_b, wpa, wpb, tm_m, 512)
    tm_f = min(512, M)
    per_f = (L // tm_f) if L >= 256 else 1
    y = _final_call(merged, wout, x2d, gate3, post_g, tm_f, per_f)
    return y.reshape(B, L, x2d.shape[1]), k, v, kidx, s_new


def kernel(x_prompt, x_sample, c_prompt, c_sample, cache_k, cache_v, cache_kidx, state_gdn, state_conv,
           page_table, w_ada, b_ada, pre_norm_g, w_in, conv_w, a_log, dt_bias, gdn_norm_g, w_pa, w_pb, w_out,
           post_norm_g):
    depth = w_in.shape[0]
    assert depth == 1, "single-layer trunk"
    BP, LP, D = x_prompt.shape
    BS, LS, _ = x_sample.shape
    n_pool = cache_k.shape[1]

    wt3 = jnp.swapaxes(w_in, 1, 2)[0]
    pre_g = pre_norm_g[0][None, :]
    wts = (wt3, conv_w[0],
           _lane_row(a_log[0], LANE_DECAY), _lane_row(dt_bias[0], LANE_DECAY), gdn_norm_g[0][None, :],
           _cast_bf16_call(w_pa), _cast_bf16_call(w_pb), _cast_bf16_call(w_out), post_norm_g[0][None, :])

    rows = BP + BS
    rows_pad = -(-rows // 16) * 16
    c_all = jnp.concatenate([c_prompt, c_sample, jnp.zeros((rows_pad - rows, D), F32)], axis=0)
    mod = _mod_call(c_all, w_ada[0], b_ada[0][None, :])

    h_p, x2d_p, gate3_p = _modulated_input(x_prompt, mod[0:BP], pre_g)
    h_s, x2d_s, gate3_s = _modulated_input(x_sample, mod[BP:BP + BS], pre_g)
    tn = 512
    big_a_p, big_a_s, aqkv_tail = _proj_nt_call(h_p, h_s, wt3, lambda j: j * tn, N_A, LP, tn, BF16, F32,
                                                "in_proj_a", tail_cols=A_CONV_CH)
    big_b_p, big_b_s = _proj_nt_call(h_p, h_s, wt3, functools.partial(_group_b_row, tn=tn), N_B, LP, tn, BF16,
                                     F32, "in_proj_b")
    conv_p = aqkv_tail[:, 8 - (CONV_W - 1):, 0:A_CONV_CH]

    attend_p = functools.partial(_dsa_prompt_call, B=BP, L=LP, QB=256)
    y_p, k_p, v_p, kidx_p, s_p = _mixers(
        x2d_p, gate3_p, big_a_p, big_b_p, h_p, None, None, attend_p, wts, B=BP, L=LP,
        gdn_chunk=min(GDN_CHUNK, LP), gdn_hp=8, gdn_nseg=8, act_dtype=BF16)

    ck = cache_k[0].reshape(n_pool * PAGE_SIZE * ATT_KV_HEADS, ATT_DH)
    cv = cache_v[0].reshape(n_pool * PAGE_SIZE * ATT_KV_HEADS, ATT_DH)
    ci = cache_kidx[0]
    attend_s = functools.partial(_dsa_sample_call, cache_k=ck, cache_v=cv, cache_kidx=ci, page_table=page_table,
                                 B=BS, T=LS)
    y_s, k_s, v_s, kidx_s, s_s = _mixers(
        x2d_s, gate3_s, big_a_s, big_b_s, h_s, state_conv[0], state_gdn[0], attend_s, wts, B=BS, L=LS,
        gdn_chunk=min(GDN_CHUNK, LS), gdn_hp=16, gdn_nseg=1, act_dtype=F32)
    conv_s = big_a_s[:, 0:A_CONV_CH].reshape(BS, LS, A_CONV_CH)[:, LS - (CONV_W - 1):, :]

    kv_shape_p = (1, BP, LP, ATT_KV_HEADS, ATT_DH)
    kv_shape_s = (1, BS, LS, ATT_KV_HEADS, ATT_DH)
    return (y_p, y_s,
            k_p.reshape(kv_shape_p), v_p.reshape(kv_shape_p), kidx_p.reshape(1, BP, LP, IDX_DH),
            s_p[None], conv_p[None],
            k_s.reshape(kv_shape_s), v_s.reshape(kv_shape_s), kidx_s.reshape(1, BS, LS, IDX_DH),
            s_s[None], conv_s[None])
```

```python
import functools
import math

import jax
import jax.numpy as jnp
from jax import lax
from jax.experimental import pallas as pl
from jax.experimental.pallas import tpu as pltpu

F32 = jnp.float32
BF16 = jnp.bfloat16
I32 = jnp.int32

D_MODEL = 2048
PAGE_SIZE = 128
GDN_QK_HEADS = 16
GDN_V_HEADS = 32
GDN_DK = 128
GDN_DV = 128
CONV_W = 4
GDN_CHUNK = 64
ATT_HEADS = 16
ATT_KV_HEADS = 2
ATT_DH = 128
IDX_HEADS = 16
IDX_DH = 128
TOPK_MAX = 256
NORM_EPS = 1e-6
L2_EPS = 1e-6

A_QK = GDN_QK_HEADS * GDN_DK
A_V = GDN_V_HEADS * GDN_DV
A_CONV_CH = 2 * A_QK + A_V
B_Q = ATT_HEADS * ATT_DH
B_KV = ATT_KV_HEADS * ATT_DH
IDX_Q = IDX_HEADS * IDX_DH
SPLIT_SIZES = (A_CONV_CH, A_V, GDN_V_HEADS, GDN_V_HEADS, B_Q, B_KV, B_KV, B_Q, IDX_Q, IDX_DH, IDX_HEADS,
               D_MODEL, D_MODEL)
B_SEGS = (4, 7, 8, 11, 12)
OFF_AQKV = 0
OFF_AZ = OFF_AQKV + A_CONV_CH
N_A = OFF_AZ + A_V
OFF_BQ = 0
OFF_BZ = OFF_BQ + B_Q
OFF_IQ = OFF_BZ + B_Q
OFF_GA = OFF_IQ + IDX_Q
OFF_GB = OFF_GA + D_MODEL
N_B = OFF_GB + D_MODEL
LANE_BETA = 0
LANE_DECAY = GDN_V_HEADS
LANE_IW = 2 * GDN_V_HEADS

VMEM_LIMIT = 56 * 1024 * 1024
NEG_INF = float("-inf")
INT_MIN = -2 ** 31


def _cparams(sem):
    return pltpu.CompilerParams(dimension_semantics=sem, vmem_limit_bytes=VMEM_LIMIT)


def _dot(a, b):
    return jnp.dot(a, b, preferred_element_type=F32)


def _dot_nt(a, b):
    return lax.dot_general(a, b, (((1,), (1,)), ((), ())), preferred_element_type=F32)


def _dot_tn(a, b):
    return lax.dot_general(a, b, (((0,), (0,)), ((), ())), preferred_element_type=F32)


def _split2(x):
    hi = x.astype(BF16)
    lo = (x - hi.astype(F32)).astype(BF16)
    return hi, lo


def _split3(x):
    x1 = x.astype(BF16)
    r = x - x1.astype(F32)
    x2 = r.astype(BF16)
    x3 = (r - x2.astype(F32)).astype(BF16)
    return x1, x2, x3


def _dot_3pass(a, b):
    ah, al = _split2(a)
    bh, bl = _split2(b)
    return _dot(ah, bh) + _dot(al, bh) + _dot(ah, bl)


def _silu(x):
    h = 0.5 * x
    return h + h * jnp.tanh(h)


def _mod_kernel(c_ref, w_ref, b_ref, o_ref):
    o_ref[...] = _dot_3pass(_silu(c_ref[...]), w_ref[...]) + b_ref[...]


def _mod_call(c_pad, w_ada, b_ada):
    rows, d = c_pad.shape
    n = w_ada.shape[1]
    tn = 768
    return pl.pallas_call(
        _mod_kernel,
        grid=(n // tn,),
        in_specs=[pl.BlockSpec((rows, d), lambda j: (0, 0)),
                  pl.BlockSpec((d, tn), lambda j: (0, j)),
                  pl.BlockSpec((1, tn), lambda j: (0, j))],
        out_specs=pl.BlockSpec((rows, tn), lambda j: (0, j)),
        out_shape=jax.ShapeDtypeStruct((rows, n), F32),
        compiler_params=_cparams(("arbitrary",)),
        name="adaln_mod",
    )(c_pad, w_ada, b_ada)


def _norm_kernel(x_ref, g_ref, scale_ref, shift_ref, o_ref):
    x = x_ref[...]
    y = x * lax.rsqrt(jnp.mean(x * x, axis=-1, keepdims=True) + NORM_EPS) * g_ref[...]
    o_ref[...] = (y * (1.0 + scale_ref[...]) + shift_ref[...]).astype(o_ref.dtype)


def _norm_call(x2d, g_row, scale3, shift3, tm, tiles_per_group):
    m, d = x2d.shape
    r = scale3.shape[1]
    mod_spec = pl.BlockSpec((None, r, d), lambda i: (i // tiles_per_group, 0, 0))
    return pl.pallas_call(
        _norm_kernel,
        grid=(m // tm,),
        in_specs=[pl.BlockSpec((tm, d), lambda i: (i, 0)),
                  pl.BlockSpec((1, d), lambda i: (0, 0)),
                  mod_spec, mod_spec],
        out_specs=pl.BlockSpec((tm, d), lambda i: (i, 0)),
        out_shape=jax.ShapeDtypeStruct((m, d), BF16),
        compiler_params=_cparams(("parallel",)),
        name="norm_modulate",
    )(x2d, g_row, scale3, shift3)


SEG_OFFS = tuple(sum(SPLIT_SIZES[:i]) for i in range(len(SPLIT_SIZES) + 1))


def _proj_nt_kernel(h_ref, w_ref, h2_ref, o_ref, o2_ref, *tail_ref, tail_tiles):
    w16 = w_ref[...].astype(BF16)
    acc = _dot_nt(h_ref[...], w16)
    o_ref[...] = acc.astype(o_ref.dtype)
    if tail_tiles:
        wanted = pl.program_id(1) < tail_tiles
        tail_ref[0][...] = jnp.where(wanted, acc[acc.shape[0] - 8:, :], 0.0)

    is_last = pl.program_id(0) == pl.num_programs(0) - 1

    @pl.when(is_last)
    def _():
        o2_ref[...] = _dot_nt(h2_ref[...], w16).astype(o2_ref.dtype)

    @pl.when(jnp.logical_not(is_last))
    def _():
        o2_ref[...] = jnp.zeros(o2_ref.shape, o2_ref.dtype)


def _proj_nt_call(h, h2, wt3, row_of_tile, n_cols, tm, tn, out_dtype, out2_dtype, name, tail_cols=0):
    m, k = h.shape
    m2 = h2.shape[0]
    n_i, n_j = m // tm, n_cols // tn
    tail_tiles = tail_cols // tn
    out_specs = [pl.BlockSpec((tm, tn), lambda i, j: (i, j)),
                 pl.BlockSpec((m2, tn), lambda i, j: (0, jnp.where(i == n_i - 1, j, n_j)))]
    out_shape = [jax.ShapeDtypeStruct((m, n_cols), out_dtype),
                 jax.ShapeDtypeStruct((m2, n_cols + tn), out2_dtype)]
    if tail_tiles:
        out_specs.append(pl.BlockSpec((None, 8, tn), lambda i, j: (i, 0, jnp.minimum(j, tail_tiles))))
        out_shape.append(jax.ShapeDtypeStruct((n_i, 8, tail_cols + tn), F32))
    return pl.pallas_call(
        functools.partial(_proj_nt_kernel, tail_tiles=tail_tiles),
        grid=(n_i, n_j),
        in_specs=[pl.BlockSpec((tm, k), lambda i, j: (i, 0)),
                  pl.BlockSpec((pl.Element(tn), pl.Element(k)),
                               lambda i, j: (pl.multiple_of(row_of_tile(j), 16), 0)),
                  pl.BlockSpec((m2, k), lambda i, j: (0, 0))],
        out_specs=out_specs,
        out_shape=out_shape,
        compiler_params=_cparams(("arbitrary", "arbitrary")),
        name=name,
    )(h, wt3, h2)


def _group_b_row(j, tn):
    per = B_Q // tn
    seg = j // per
    start = SEG_OFFS[B_SEGS[-1]]
    for s in range(len(B_SEGS) - 2, -1, -1):
        start = jnp.where(seg == s, SEG_OFFS[B_SEGS[s]], start)
    return start + (j % per) * tn


def _proj_tail_kernel(h_ref, wkv_ref, wik_ref, wg_ref, wiw_ref, k_ref, v_ref, kidx_ref, small_ref):
    h = h_ref[...]
    kv = _dot_nt(h, wkv_ref[...].astype(BF16))
    rows = h.shape[0]
    for n in range(ATT_KV_HEADS):
        head_rows = pl.ds(n, rows, stride=ATT_KV_HEADS)
        k_ref[head_rows, :] = kv[:, n * ATT_DH:(n + 1) * ATT_DH]
        v_ref[head_rows, :] = kv[:, B_KV + n * ATT_DH:B_KV + (n + 1) * ATT_DH]
    kidx_ref[...] = _dot_nt(h, wik_ref[...].astype(BF16))
    gate = _dot_nt(h, wg_ref[...].astype(BF16))
    iw = _dot_nt(h, wiw_ref[...].astype(BF16))
    lane = lax.broadcasted_iota(I32, gate.shape, 1)
    small_ref[...] = jnp.where(lane < LANE_IW, gate, jnp.where(lane < LANE_IW + IDX_HEADS, iw, 0.0))


def _proj_tail_call(h, wt3, tm):
    m, k = h.shape
    kvh = ATT_KV_HEADS
    out_blocks = ((kvh * tm, ATT_DH), (kvh * tm, ATT_DH), (tm, IDX_DH), (tm, 128))
    out_arrays = ((kvh * m, ATT_DH), (kvh * m, ATT_DH), (m, IDX_DH), (m, 128))
    beta0, kv0, ik0, iw0 = SEG_OFFS[2], SEG_OFFS[5], SEG_OFFS[9], SEG_OFFS[10]
    assert SEG_OFFS[3] - beta0 == LANE_DECAY and SEG_OFFS[4] - beta0 == LANE_IW and SEG_OFFS[6] - kv0 == B_KV

    def rows(start, size):
        return pl.BlockSpec((pl.Element(size), pl.Element(k)), lambda i: (start, 0))

    return pl.pallas_call(
        _proj_tail_kernel,
        grid=(m // tm,),
        in_specs=[pl.BlockSpec((tm, k), lambda i: (i, 0)),
                  rows(kv0, 2 * B_KV), rows(ik0, IDX_DH), rows(beta0, 128), rows(iw0 - LANE_IW, 128)],
        out_specs=[pl.BlockSpec(blk, lambda i: (i, 0)) for blk in out_blocks],
        out_shape=[jax.ShapeDtypeStruct(shp, F32) for shp in out_arrays],
        compiler_params=_cparams(("parallel",)),
        name="in_proj_tail",
    )(h, wt3, wt3, wt3, wt3)


def _gates_kernel(sm_ref, alog_ref, dtb_ref, o_ref, *, chunk, sub):
    ri = lax.broadcasted_iota(I32, (sub, sub), 0)
    ci = lax.broadcasted_iota(I32, (sub, sub), 1)
    shift = int(math.log2(chunk))
    tri = jnp.where((ri >= ci) & ((ri >> shift) == (ci >> shift)), 1.0, 0.0).astype(BF16)
    lane = lax.broadcasted_iota(I32, (sub, 128), 1)
    for r0 in range(0, sm_ref.shape[0], sub):
        x = sm_ref[r0:r0 + sub, :]
        beta = jax.nn.sigmoid(x)
        z = x + dtb_ref[...]
        softplus = jnp.maximum(z, 0.0) + jnp.log1p(jnp.exp(-jnp.abs(z)))
        g = -jnp.exp(alog_ref[...]) * softplus
        g = jnp.where((lane >= LANE_DECAY) & (lane < LANE_DECAY + GDN_V_HEADS), g, 0.0)
        g1, g2, g3 = _split3(g)
        gc = _dot(tri, g1) + _dot(tri, g2) + _dot(tri, g3)
        o_ref[r0:r0 + sub, :] = jnp.where(lane < GDN_V_HEADS, beta, gc)


def _gates_call(small, alog_row, dtb_row, chunk):
    m = small.shape[0]
    sub = 256
    tm = min(m, 1024)
    return pl.pallas_call(
        functools.partial(_gates_kernel, chunk=chunk, sub=sub),
        grid=(m // tm,),
        in_specs=[pl.BlockSpec((tm, 128), lambda i: (i, 0)),
                  pl.BlockSpec((1, 128), lambda i: (0, 0)),
                  pl.BlockSpec((1, 128), lambda i: (0, 0))],
        out_specs=pl.BlockSpec((tm, 128), lambda i: (i, 0)),
        out_shape=jax.ShapeDtypeStruct((m, 128), F32),
        compiler_params=_cparams(("parallel",)),
        name="gdn_gates",
    )(small, alog_row, dtb_row)


def _unrolled_loop(n, unroll, fn):
    unroll = min(unroll, n)
    assert n % unroll == 0

    def body(t, carry):
        for u in range(unroll):
            fn(t * unroll + u)
        return carry

    lax.fori_loop(0, n // unroll, body, 0)


def _gdn_kernel(*refs, L, C, HP, NSEG, has_state):
    ins = 13 if has_state else 12
    (q_ref, k_ref, v_ref, z_ref, bq_ref, bk_ref, bv_ref, wq_ref, wk_ref, wv_ref, gates_ref, gn_ref) = refs[:12]
    s0_ref = refs[12] if has_state else None
    (y_ref, sout_ref, xq_s, xk_s, xv_s, beta_s, g_s, st_s, a_s, p_s, rhs_s, kt_s, qkd_s, qd_s,
     m_s, n_s, qe_s, o_s) = refs[ins:]
    nh = 2 * HP
    nc = L // C
    hq0 = pl.program_id(1) * HP

    def first_segment():
        xq_s[0:8, :] = bq_ref[...]
        xk_s[0:8, :] = bk_ref[...]
        xv_s[0:8, :] = bv_ref[...]
        if has_state:
            st_s[...] = s0_ref[...]
        else:
            st_s[...] = jnp.zeros(st_s.shape, F32)

    def next_segment():
        xq_s[0:8, :] = xq_s[L:L + 8, :]
        xk_s[0:8, :] = xk_s[L:L + 8, :]
        xv_s[0:8, :] = xv_s[L:L + 8, :]

    if NSEG == 1:
        first_segment()
    else:
        pl.when(pl.program_id(2) == 0)(first_segment)
        pl.when(pl.program_id(2) > 0)(next_segment)
    xq_s[8:L + 8, :] = q_ref[...].astype(F32)
    xk_s[8:L + 8, :] = k_ref[...].astype(F32)
    xv_s[8:L + 8, :] = v_ref[...].astype(F32)

    gates = gates_ref[...]
    lane = lax.broadcasted_iota(I32, gates.shape, 1)
    for idx in range(nh):
        hv = 2 * hq0 + idx
        bcol = jnp.sum(jnp.where(lane == hv + LANE_BETA, gates, 0.0), axis=-1, keepdims=True)
        gcol = jnp.sum(jnp.where(lane == hv + LANE_DECAY, gates, 0.0), axis=-1, keepdims=True)
        beta_s[idx] = jnp.broadcast_to(bcol, (L, 128))
        g_s[idx] = jnp.broadcast_to(gcol, (L, 128))

    ii = lax.broadcasted_iota(I32, (C, C), 0)
    jj = lax.broadcasted_iota(I32, (C, C), 1)
    gn = gn_ref[...]

    def conv(x_s, r0, cols, w):
        win = x_s[pl.ds(r0, C + 8), cols]
        acc = (win[5:5 + C] * w[0:1] + win[6:6 + C] * w[1:2] + win[7:7 + C] * w[2:3] + win[8:8 + C] * w[3:4])
        return _silu(acc)

    def l2n(x):
        return x * lax.rsqrt(jnp.sum(x * x, axis=-1, keepdims=True) + L2_EPS)

    def lanes(i, width):
        if isinstance(i, int):
            return slice(i * width, (i + 1) * width)
        return pl.ds(pl.multiple_of(i * width, width), width)

    def pair_terms(j):
        c, p = (j, 0) if HP == 1 else (j // HP, j % HP)
        r0 = pl.multiple_of(c * C, C)
        rows = pl.ds(r0, C)
        lq, lv = lanes(p, 128), lanes(p, 256)
        q = l2n(conv(xq_s, r0, lq, wq_ref[:, lq])) * (GDN_DK ** -0.5)
        k = l2n(conv(xk_s, r0, lq, wk_ref[:, lq]))
        vc = conv(xv_s, r0, lv, wv_ref[:, lv])
        k16 = k.astype(BF16)
        kk = _dot_nt(k16, k16)
        qk = _dot_nt(q.astype(BF16), k16)
        for r in range(2):
            idx = 2 * p + r
            e = c * nh + idx
            gb = g_s[idx, rows, :]
            bb = beta_s[idx, rows, :]
            e_g = jnp.exp(gb)
            e_tail = jnp.exp(gb[C - 1:C, :] - gb)
            if C < 128:
                gpad = jnp.concatenate([gb, jnp.zeros((128 - C, 128), F32)], axis=0)
            else:
                gpad = gb
            dexp = gb[:, 0:C] - gpad.T[0:C, 0:C]
            dm = jnp.exp(jnp.where(ii >= jj, dexp, NEG_INF))
            a = jnp.where(ii > jj, kk * dm, 0.0) * bb[:, 0:C]
            a_s[e] = a.astype(a_s.dtype)
            p_s[e] = jnp.where(ii == jj, 1.0, 0.0) - a
            rhs_s[e] = jnp.concatenate([vc[:, r * 128:(r + 1) * 128] * bb, k * bb * e_g], axis=1).astype(rhs_s.dtype)
            kt_s[e] = (k * e_tail).astype(kt_s.dtype)
            qkd_s[e] = (qk * dm).astype(qkd_s.dtype)
            qd_s[e] = q * e_g

    def square(e):
        ak = a_s[e].astype(BF16)
        a_s[e] = _dot(ak, ak).astype(a_s.dtype)

    def inverse_stage(e, last):
        ak = a_s[e].astype(BF16)
        pk = p_s[e]
        p_s[e] = pk + _dot(pk.astype(BF16), ak)
        if not last:
            a_s[e] = _dot(ak, ak).astype(a_s.dtype)

    def solve(e):
        rhs_s[e] = _dot(p_s[e].astype(BF16), rhs_s[e].astype(BF16)).astype(rhs_s.dtype)

    def state_terms(e):
        sol16 = rhs_s[e].astype(BF16)
        mn = _dot_tn(kt_s[e].astype(BF16), sol16)
        qo = _dot(qkd_s[e].astype(BF16), sol16)
        n_s[e] = mn[:, 0:GDN_DV]
        m_s[e] = mn[:, GDN_DV:].astype(m_s.dtype)
        o_s[e] = qo[:, 0:GDN_DV]
        qe_s[e] = (qd_s[e] - qo[:, GDN_DV:]).astype(qe_s.dtype)

    def head_step(c, idx):
        e = c * nh + idx
        r0 = pl.multiple_of(c * C, C)
        rows = pl.ds(r0, C)
        s = st_s[idx]
        s16 = s.astype(BF16)
        o = _dot(qe_s[e].astype(BF16), s16) + o_s[e]
        e_last = jnp.exp(g_s[idx, pl.ds(r0 + C - 1, 1), :])
        st_s[idx] = s * e_last - _dot(m_s[e].astype(BF16), s16) + n_s[e]
        on = o * lax.rsqrt(jnp.mean(o * o, axis=-1, keepdims=True) + NORM_EPS) * gn
        zz = z_ref[rows, lanes(idx, 128)].astype(F32)
        y_ref[rows, lanes(idx, 128)] = (on * _silu(zz)).astype(y_ref.dtype)

    ne = nc * nh
    stages = int(math.log2(C)) - 1
    _unrolled_loop(nc * HP, 2 if C >= 64 else 4, pair_terms)
    _unrolled_loop(ne, 32, square)
    for st in range(stages):
        _unrolled_loop(ne, 32, functools.partial(inverse_stage, last=(st == stages - 1)))
    _unrolled_loop(ne, 16, solve)
    _unrolled_loop(ne, 16, state_terms)

    def chunk_step(c, carry):
        if nh <= 4:
            for idx in range(nh):
                head_step(c, idx)
        else:
            _unrolled_loop(nh, 16, functools.partial(head_step, c))
        return carry

    lax.fori_loop(0, nc, chunk_step, 0)
    sout_ref[...] = st_s[...]


def _gdn_call(big, convbuf8, conv_w, gates, gn_row, s0, *, B, L, C, HP, NSEG, y_dtype):
    has_state = s0 is not None
    ls = L // NSEG
    ne = (ls // C) * 2 * HP
    half = BF16 if C % 16 == 0 else F32
    nq = GDN_QK_HEADS // HP
    wq, wv = HP * 128, HP * 256
    kq0 = A_QK // wq
    v0 = (2 * A_QK) // wv
    z0 = OFF_AZ // wv
    in_specs = [
        pl.BlockSpec((ls, wq), lambda b, h, s: (b * NSEG + s, h)),
        pl.BlockSpec((ls, wq), lambda b, h, s: (b * NSEG + s, kq0 + h)),
        pl.BlockSpec((ls, wv), lambda b, h, s: (b * NSEG + s, v0 + h)),
        pl.BlockSpec((ls, wv), lambda b, h, s: (b * NSEG + s, z0 + h)),
        pl.BlockSpec((None, 8, wq), lambda b, h, s: (b, 0, h)),
        pl.BlockSpec((None, 8, wq), lambda b, h, s: (b, 0, kq0 + h)),
        pl.BlockSpec((None, 8, wv), lambda b, h, s: (b, 0, v0 + h)),
        pl.BlockSpec((CONV_W, wq), lambda b, h, s: (0, h)),
        pl.BlockSpec((CONV_W, wq), lambda b, h, s: (0, kq0 + h)),
        pl.BlockSpec((CONV_W, wv), lambda b, h, s: (0, v0 + h)),
        pl.BlockSpec((ls, 128), lambda b, h, s: (b * NSEG + s, 0)),
        pl.BlockSpec((1, 128), lambda b, h, s: (0, 0)),
    ]
    args = [big, big, big, big, convbuf8, convbuf8, convbuf8, conv_w, conv_w, conv_w, gates, gn_row]
    if has_state:
        in_specs.append(pl.BlockSpec((None, 2 * HP, GDN_DK, GDN_DV), lambda b, h, s: (b, h, 0, 0)))
        args.append(s0)
    return pl.pallas_call(
        functools.partial(_gdn_kernel, L=ls, C=C, HP=HP, NSEG=NSEG, has_state=has_state),
        grid=(B, nq, NSEG),
        in_specs=in_specs,
        out_specs=[pl.BlockSpec((ls, wv), lambda b, h, s: (b * NSEG + s, h)),
                   pl.BlockSpec((None, 2 * HP, GDN_DK, GDN_DV), lambda b, h, s: (b, h, 0, 0))],
        out_shape=[jax.ShapeDtypeStruct((B * L, A_V), y_dtype),
                   jax.ShapeDtypeStruct((B, GDN_V_HEADS, GDN_DK, GDN_DV), F32)],
        scratch_shapes=[pltpu.VMEM((ls + 8, wq), F32), pltpu.VMEM((ls + 8, wq), F32), pltpu.VMEM((ls + 8, wv), F32),
                        pltpu.VMEM((2 * HP, ls, 128), F32), pltpu.VMEM((2 * HP, ls, 128), F32),
                        pltpu.VMEM((2 * HP, GDN_DK, GDN_DV), F32),
                        pltpu.VMEM((ne, C, C), half), pltpu.VMEM((ne, C, C), F32),
                        pltpu.VMEM((ne, C, GDN_DV + GDN_DK), half), pltpu.VMEM((ne, C, GDN_DK), half),
                        pltpu.VMEM((ne, C, C), half), pltpu.VMEM((ne, C, GDN_DK), F32),
                        pltpu.VMEM((ne, GDN_DK, GDN_DV), BF16), pltpu.VMEM((ne, GDN_DK, GDN_DV), F32),
                        pltpu.VMEM((ne, C, GDN_DK), half), pltpu.VMEM((ne, C, GDN_DV), F32)],
        compiler_params=_cparams(("parallel", "arbitrary", "arbitrary")),
        name="gdn_prompt" if not has_state else "gdn_sample",
    )(*args)


KEY_NEG_INF = -2 ** 31 + 0x7FFFFF


def _key_to_float(key):
    bits = key ^ ((key >> 31) & jnp.int32(0x7FFFFFFF))
    return pltpu.bitcast(bits, F32)


def _count_true(mask):
    ones = jnp.where(mask, 1.0, 0.0)
    parts = [ones[:, i * 128:(i + 1) * 128] for i in range(mask.shape[1] // 128)]
    while len(parts) > 1:
        parts = [a + b for a, b in zip(parts[0::2], parts[1::2])] + ([parts[-1]] if len(parts) % 2 else [])
    return jnp.sum(parts[0], axis=-1, keepdims=True)


def _selection_bias(score_ref, bias_ref, qpos0, rows, width, topk, bits=1):
    def step(it, t):
        best = t
        for digit in range(1, 2 ** bits):
            cand = t + jnp.left_shift(jnp.int32(digit), 32 - bits * (it + 1))
            cnt = _count_true(score_ref[:, 0:width] >= _key_to_float(cand))
            best = jnp.where((cnt >= float(topk)) | (cand <= KEY_NEG_INF), cand, best)
        return best

    thr = _key_to_float(lax.fori_loop(0, 32 // bits, step, jnp.full((rows, 1), INT_MIN, I32)))
    score = score_ref[:, 0:width]
    causal = (lax.broadcasted_iota(I32, (rows, width), 1)
              <= qpos0 + lax.broadcasted_iota(I32, (rows, width), 0))
    sel = (score >= thr) & causal
    bias_ref[:, 0:width] = jnp.where(sel, 0.0, NEG_INF)
    n_sel = _count_true(sel)
    has_ties = jnp.max(n_sel) > float(topk)

    @pl.when(has_ties)
    def _():
        need = float(topk) - _count_true(score > thr)
        eq16 = jnp.where(score == thr, 1.0, 0.0).astype(BF16)

        def tile(j, carry):
            c0 = pl.multiple_of(j * 128, 128)
            ri = lax.broadcasted_iota(I32, (width, 128), 0)
            ci = lax.broadcasted_iota(I32, (width, 128), 1) + c0
            upper = jnp.where(ri <= ci, 1.0, 0.0).astype(BF16)
            prefix = _dot(eq16, upper)
            s_t = score_ref[:, pl.ds(c0, 128)]
            keep = (s_t > thr) | ((s_t == thr) & (prefix <= need))
            old = bias_ref[:, pl.ds(c0, 128)]
            bias_ref[:, pl.ds(c0, 128)] = jnp.where(keep, old, NEG_INF)
            return carry

        lax.fori_loop(0, width // 128, tile, 0)


SOFTMAX_LOG2_SCALE = (ATT_DH ** -0.5) * math.log2(math.e)


def _dsa_prompt_kernel(qi_ref, wi_ref, kidx_ref, q_ref, k_ref, v_ref, z_ref, o_ref,
                       score_s, bias_s, kidx16_s, k16_s, vext_s, m_s, acc_s, *, QB, SUB, KT, L, topk):
    i = pl.program_id(1)
    q0 = i * QB
    nt = (q0 + QB - 1) // KT + 1
    groups = ATT_HEADS // ATT_KV_HEADS

    @pl.when(i == 0)
    def _():
        kidx16_s[...] = kidx_ref[...].astype(BF16)
        for n in range(ATT_KV_HEADS):
            head_rows = pl.ds(n, L, stride=ATT_KV_HEADS)
            k16_s[:, n * ATT_DH:(n + 1) * ATT_DH] = k_ref[head_rows, :].astype(BF16)
            vext_s[n, :, 0:ATT_DH] = v_ref[head_rows, :].astype(BF16)
            vext_s[n, :, ATT_DH:] = jnp.ones((L, ATT_DH), BF16)

    rowpos = q0 + lax.broadcasted_iota(I32, (SUB, KT), 0)
    col = lax.broadcasted_iota(I32, (SUB, KT), 1)

    def tile_rows(t):
        return pl.ds(pl.multiple_of(t * KT, KT), KT)

    subs = [slice(r, r + SUB) for r in range(0, QB, SUB)]

    def index_tile(t, carry):
        kid = kidx16_s[tile_rows(t), :]
        for rs in subs:
            wi = wi_ref[rs, :] * ((IDX_DH ** -0.5) * (IDX_HEADS ** -0.5))
            acc = jnp.zeros((SUB, KT), F32)
            for h in range(IDX_HEADS):
                lg = _dot_nt(qi_ref[rs, h * IDX_DH:(h + 1) * IDX_DH], kid)
                acc = acc + jnp.maximum(lg, 0.0) * wi[:, LANE_IW + h:LANE_IW + h + 1]
            causal = col + t * KT <= rowpos + rs.start
            score_s[rs, tile_rows(t)] = jnp.where(causal, acc, NEG_INF)
        return carry

    lax.fori_loop(0, nt, index_tile, 0)
    for w in range(1, L // KT + 1):
        @pl.when(nt == w)
        def _():
            _selection_bias(score_s, bias_s, q0, QB, w * KT, topk)

    m_s[...] = jnp.full(m_s.shape, float(jnp.finfo(F32).min), F32)
    acc_s[...] = jnp.zeros(acc_s.shape, F32)

    def attend_tile(t, carry):
        for n in range(ATT_KV_HEADS):
            kn = k16_s[tile_rows(t), n * ATT_DH:(n + 1) * ATT_DH]
            vn = vext_s[n, tile_rows(t), :]
            for g in range(groups):
                h = n * groups + g
                for rs in subs:
                    sb = _dot_nt(q_ref[rs, h * ATT_DH:(h + 1) * ATT_DH], kn) + bias_s[rs, tile_rows(t)]
                    m_old = m_s[h, rs, :]
                    m_new = jnp.maximum(m_old, jnp.max(sb, axis=-1, keepdims=True))
                    p = jnp.exp2((sb - jnp.concatenate([m_new] * (KT // 128), axis=1)) * SOFTMAX_LOG2_SCALE)
                    alpha = jnp.exp2((m_old - m_new) * SOFTMAX_LOG2_SCALE)
                    m_s[h, rs, :] = m_new
                    acc_s[h, rs, :] = (acc_s[h, rs, :] * jnp.concatenate([alpha, alpha], axis=1)
                                       + _dot(p.astype(BF16), vn))
        return carry

    lax.fori_loop(0, nt, attend_tile, 0)
    for h in range(ATT_HEADS):
        cols = slice(h * ATT_DH, (h + 1) * ATT_DH)
        zz = z_ref[:, cols].astype(F32)
        acc = acc_s[h]
        o_ref[:, cols] = (acc[:, 0:ATT_DH] / acc[:, ATT_DH:] * _silu(zz)).astype(o_ref.dtype)


def _dsa_prompt_call(big, small, kidx, k, v, *, B, L, QB):
    topk = min(TOPK_MAX, L // 4)
    nqb = L // QB
    iq0 = OFF_IQ // B_Q
    bq0 = OFF_BQ // B_Q
    bz0 = OFF_BZ // B_Q
    return pl.pallas_call(
        functools.partial(_dsa_prompt_kernel, QB=QB, SUB=128, KT=256, L=L, topk=topk),
        grid=(B, nqb),
        in_specs=[pl.BlockSpec((QB, IDX_Q), lambda b, i: (b * nqb + i, iq0)),
                  pl.BlockSpec((QB, 128), lambda b, i: (b * nqb + i, 0)),
                  pl.BlockSpec((L, IDX_DH), lambda b, i: (b, 0)),
                  pl.BlockSpec((QB, B_Q), lambda b, i: (b * nqb + i, bq0)),
                  pl.BlockSpec((ATT_KV_HEADS * L, ATT_DH), lambda b, i: (b, 0)),
                  pl.BlockSpec((ATT_KV_HEADS * L, ATT_DH), lambda b, i: (b, 0)),
                  pl.BlockSpec((QB, B_Q), lambda b, i: (b * nqb + i, bz0))],
        out_specs=pl.BlockSpec((QB, B_Q), lambda b, i: (b * nqb + i, 0)),
        out_shape=jax.ShapeDtypeStruct((B * L, B_Q), BF16),
        scratch_shapes=[pltpu.VMEM((QB, L), F32), pltpu.VMEM((QB, L), F32),
                        pltpu.VMEM((L, IDX_DH), BF16), pltpu.VMEM((L, B_KV), BF16),
                        pltpu.VMEM((ATT_KV_HEADS, L, 2 * ATT_DH), BF16),
                        pltpu.VMEM((ATT_HEADS, QB, 128), F32), pltpu.VMEM((ATT_HEADS, QB, 2 * ATT_DH), F32)],
        compiler_params=_cparams(("parallel", "arbitrary")),
        name="dsa_prompt",
    )(big, small, kidx, big, k, v, big)


def _dsa_sample_kernel(pt_ref, qi_ref, wi_ref, kin_ref, q_ref, kn_ref, vn_ref, z_ref, ck_hbm, cv_hbm, ci_hbm,
                       o_ref, kbuf, vbuf, ibuf, sems, score_s, bias_s, *, NP, T, topk):
    b = pl.program_id(0)
    nb = pl.num_programs(0)
    slot = b % 2
    past = NP * PAGE_SIZE
    width = past + 128

    kvp = ATT_KV_HEADS * PAGE_SIZE

    def page_copies(bb, sl, p):
        page = pt_ref[bb * NP + p]
        rows = pl.ds(pl.multiple_of(p * PAGE_SIZE, PAGE_SIZE), PAGE_SIZE)
        src2 = pl.ds(pl.multiple_of(page * kvp, kvp), kvp)
        dst2 = pl.ds(pl.multiple_of(p * kvp, kvp), kvp)
        return (pltpu.make_async_copy(ck_hbm.at[src2], kbuf.at[sl, dst2], sems.at[0, sl]),
                pltpu.make_async_copy(cv_hbm.at[src2], vbuf.at[sl, dst2], sems.at[1, sl]),
                pltpu.make_async_copy(ci_hbm.at[page], ibuf.at[sl, rows], sems.at[2, sl]))

    def start_fetch(bb, sl):
        def body(p, carry):
            for cp in page_copies(bb, sl, p):
                cp.start()
            return carry
        lax.fori_loop(0, NP, body, 0, unroll=8)

    def wait_fetch(bb, sl):
        def body(p, carry):
            for cp in page_copies(bb, sl, p):
                cp.wait()
            return carry
        lax.fori_loop(0, NP, body, 0, unroll=8)

    @pl.when(b == 0)
    def _():
        for sl in range(2):
            kbuf[sl, ATT_KV_HEADS * past:, :] = jnp.zeros((ATT_KV_HEADS * 128, ATT_DH), F32)
            vbuf[sl, ATT_KV_HEADS * past:, :] = jnp.zeros((ATT_KV_HEADS * 128, ATT_DH), F32)
            ibuf[sl, past:, :] = jnp.zeros((128, IDX_DH), F32)
        start_fetch(0, 0)

    @pl.when(b + 1 < nb)
    def _():
        start_fetch(b + 1, 1 - slot)

    new_rows = pl.ds(ATT_KV_HEADS * past, ATT_KV_HEADS * T)
    kbuf[slot, new_rows, :] = kn_ref[...]
    vbuf[slot, new_rows, :] = vn_ref[...]
    ibuf[slot, past:past + T, :] = kin_ref[...]
    wait_fetch(b, slot)

    qi = qi_ref[...]
    qis = jnp.concatenate([qi[:, h * IDX_DH:(h + 1) * IDX_DH] for h in range(IDX_HEADS)], axis=0).astype(BF16)
    wi = wi_ref[...] * ((IDX_DH ** -0.5) * (IDX_HEADS ** -0.5))
    lg = _dot_nt(qis, ibuf[slot].astype(BF16))
    score = jnp.zeros((T, width), F32)
    for h in range(IDX_HEADS):
        score = score + jnp.maximum(lg[h * T:(h + 1) * T, :], 0.0) * wi[:, LANE_IW + h:LANE_IW + h + 1]
    qpos = past + lax.broadcasted_iota(I32, (T, width), 0)
    kpos = lax.broadcasted_iota(I32, (T, width), 1)
    causal = kpos <= qpos
    score_s[...] = jnp.where(causal, score, NEG_INF)
    _selection_bias(score_s, bias_s, past, T, width, topk, bits=2)

    groups = ATT_HEADS // ATT_KV_HEADS
    q = q_ref[...]
    bias = jnp.concatenate([bias_s[...]] * groups, axis=0)
    for n in range(ATT_KV_HEADS):
        head_rows = pl.ds(n, width, stride=ATT_KV_HEADS)
        kn = kbuf[slot, head_rows, :].astype(BF16)
        vn = vbuf[slot, head_rows, :].astype(BF16)
        qs = jnp.concatenate([q[:, (n * groups + g) * ATT_DH:(n * groups + g + 1) * ATT_DH] for g in range(groups)],
                             axis=0).astype(BF16)
        s = _dot_nt(qs, kn) * (ATT_DH ** -0.5) + bias
        m = jnp.max(s, axis=-1, keepdims=True)
        p = jnp.exp(s - m)
        denom = jnp.sum(p, axis=-1, keepdims=True)
        o = _dot(p.astype(BF16), vn) / denom
        for g in range(groups):
            h = n * groups + g
            cols = slice(h * ATT_DH, (h + 1) * ATT_DH)
            zz = z_ref[:, cols]
            o_ref[:, cols] = o[g * T:(g + 1) * T, :] * _silu(zz)


def _dsa_sample_call(big, small, kidx, k, v, cache_k, cache_v, cache_kidx, page_table, *, B, T):
    npages = page_table.shape[1]
    past = npages * PAGE_SIZE
    topk = min(TOPK_MAX, (past + T) // 4)
    width = past + 128
    iq0 = OFF_IQ // B_Q
    bq0 = OFF_BQ // B_Q
    bz0 = OFF_BZ // B_Q
    grid_spec = pltpu.PrefetchScalarGridSpec(
        num_scalar_prefetch=1,
        grid=(B,),
        in_specs=[pl.BlockSpec((T, IDX_Q), lambda b, pt: (b, iq0)),
                  pl.BlockSpec((T, 128), lambda b, pt: (b, 0)),
                  pl.BlockSpec((T, IDX_DH), lambda b, pt: (b, 0)),
                  pl.BlockSpec((T, B_Q), lambda b, pt: (b, bq0)),
                  pl.BlockSpec((ATT_KV_HEADS * T, ATT_DH), lambda b, pt: (b, 0)),
                  pl.BlockSpec((ATT_KV_HEADS * T, ATT_DH), lambda b, pt: (b, 0)),
                  pl.BlockSpec((T, B_Q), lambda b, pt: (b, bz0)),
                  pl.BlockSpec(memory_space=pl.ANY),
                  pl.BlockSpec(memory_space=pl.ANY),
                  pl.BlockSpec(memory_space=pl.ANY)],
        out_specs=pl.BlockSpec((T, B_Q), lambda b, pt: (b, 0)),
        scratch_shapes=[pltpu.VMEM((2, ATT_KV_HEADS * width, ATT_DH), F32),
                        pltpu.VMEM((2, ATT_KV_HEADS * width, ATT_DH), F32),
                        pltpu.VMEM((2, width, IDX_DH), F32), pltpu.SemaphoreType.DMA((3, 2)),
                        pltpu.VMEM((T, width), F32), pltpu.VMEM((T, width), F32)])
    return pl.pallas_call(
        functools.partial(_dsa_sample_kernel, NP=npages, T=T, topk=topk),
        grid_spec=grid_spec,
        out_shape=jax.ShapeDtypeStruct((B * T, B_Q), F32),
        compiler_params=_cparams(("arbitrary",)),
        name="dsa_sample",
    )(page_table.reshape(-1), big, small, kidx, big, k, v, big, cache_k, cache_v, cache_kidx)


def _merge_kernel(ya_ref, yb_ref, ga_ref, gb_ref, wpa_ref, wpb_ref, o_ref):
    pa = _dot(ya_ref[...].astype(BF16), wpa_ref[...])
    pb = _dot(yb_ref[...].astype(BF16), wpb_ref[...])
    o = jax.nn.sigmoid(ga_ref[...].astype(F32)) * pa + jax.nn.sigmoid(gb_ref[...].astype(F32)) * pb
    o_ref[...] = o.astype(o_ref.dtype)


def _merge_call(ya, yb, big, wpa, wpb, tm, tn):
    m = ya.shape[0]
    n = wpa.shape[1]
    ga0 = OFF_GA // tn
    gb0 = OFF_GB // tn
    return pl.pallas_call(
        _merge_kernel,
        grid=(m // tm, n // tn),
        in_specs=[pl.BlockSpec((tm, A_V), lambda i, j: (i, 0)),
                  pl.BlockSpec((tm, B_Q), lambda i, j: (i, 0)),
                  pl.BlockSpec((tm, tn), lambda i, j: (i, ga0 + j)),
                  pl.BlockSpec((tm, tn), lambda i, j: (i, gb0 + j)),
                  pl.BlockSpec((A_V, tn), lambda i, j: (0, j)),
                  pl.BlockSpec((B_Q, tn), lambda i, j: (0, j))],
        out_specs=pl.BlockSpec((tm, tn), lambda i, j: (i, j)),
        out_shape=jax.ShapeDtypeStruct((m, n), BF16),
        compiler_params=_cparams(("parallel", "arbitrary")),
        name="merge_proj",
    )(ya, yb, big, big, wpa, wpb)


def _final_kernel(m_ref, w_ref, x_ref, gate_ref, g_ref, o_ref):
    t = _dot(m_ref[...], w_ref[...])
    y = t * lax.rsqrt(jnp.mean(t * t, axis=-1, keepdims=True) + NORM_EPS) * g_ref[...]
    o_ref[...] = x_ref[...] + gate_ref[...] * y


def _final_call(merged, w_out, x2d, gate3, g_row, tm, tiles_per_group):
    m, d = x2d.shape
    r = gate3.shape[1]
    return pl.pallas_call(
        _final_kernel,
        grid=(m // tm,),
        in_specs=[pl.BlockSpec((tm, d), lambda i: (i, 0)),
                  pl.BlockSpec((d, d), lambda i: (0, 0)),
                  pl.BlockSpec((tm, d), lambda i: (i, 0)),
                  pl.BlockSpec((None, r, d), lambda i: (i // tiles_per_group, 0, 0)),
                  pl.BlockSpec((1, d), lambda i: (0, 0))],
        out_specs=pl.BlockSpec((tm, d), lambda i: (i, 0)),
        out_shape=jax.ShapeDtypeStruct((m, d), F32),
        compiler_params=_cparams(("parallel",)),
        name="out_proj",
    )(merged, w_out, x2d, gate3, g_row)


def _cast_kernel(x_ref, o_ref):
    o_ref[...] = x_ref[...].astype(o_ref.dtype)


def _cast_bf16_call(w3):
    _, r, n = w3.shape
    tr = 256
    return pl.pallas_call(
        _cast_kernel,
        grid=(r // tr,),
        in_specs=[pl.BlockSpec((None, tr, n), lambda i: (0, i, 0))],
        out_specs=pl.BlockSpec((tr, n), lambda i: (i, 0)),
        out_shape=jax.ShapeDtypeStruct((r, n), BF16),
        compiler_params=_cparams(("parallel",)),
        name="w_cast",
    )(w3)


def _lane_row(vec, lane0):
    return jnp.zeros((1, 128), F32).at[0, lane0:lane0 + vec.shape[0]].set(vec.astype(F32))


def _modulated_input(x, mod, pre_g):
    B, L, D = x.shape
    M = B * L
    x2d = x.reshape(M, D)
    shift, scale, gate = mod[:, 0:D], mod[:, D:2 * D], mod[:, 2 * D:3 * D]
    if L >= 512:
        tm_n = 512
        per = L // tm_n
        scale3, shift3, gate3 = scale[:, None, :], shift[:, None, :], gate[:, None, :]
    else:
        tm_n = M
        per = 1
        scale3, shift3, gate3 = (jnp.repeat(t, L, axis=0)[None] for t in (scale, shift, gate))
    return _norm_call(x2d, pre_g, scale3, shift3, tm_n, per), x2d, gate3


def _mixers(x2d, gate3, big_a, big_b, h, conv_hist, s0, attend, wts, *, B, L, gdn_chunk, gdn_hp, gdn_nseg,
            act_dtype):
    (wt3, conv_w, alog_row, dtb_row, gn_row, wpa, wpb, wout, post_g) = wts
    M = B * L
    k, v, kidx, small = _proj_tail_call(h, wt3, min(M, 1024))

    gates = _gates_call(small, alog_row, dtb_row, gdn_chunk)
    if conv_hist is None:
        convbuf8 = jnp.zeros((B, 8, A_CONV_CH), F32)
    else:
        convbuf8 = jnp.concatenate([jnp.zeros((B, 5, A_CONV_CH), F32), conv_hist.astype(F32)], axis=1)
    y_a, s_new = _gdn_call(big_a, convbuf8, conv_w, gates, gn_row, s0, B=B, L=L, C=gdn_chunk, HP=gdn_hp,
                           NSEG=gdn_nseg, y_dtype=act_dtype)
    y_b = attend(big_b, small, kidx, k, v)
    tm_m = min(1024, M)
    merged = _merge_call(y_a, y_b, big_b, wpa, wpb, tm_m, 512)
    tm_f = min(512, M)
    per_f = (L // tm_f) if gate3.shape[1] == 1 else 1
    y = _final_call(merged, wout, x2d, gate3, post_g, tm_f, per_f)
    return y.reshape(B, L, x2d.shape[1]), k, v, kidx, s_new


def kernel(x_prompt, x_sample, c_prompt, c_sample, cache_k, cache_v, cache_kidx, state_gdn, state_conv,
           page_table, w_ada, b_ada, pre_norm_g, w_in, conv_w, a_log, dt_bias, gdn_norm_g, w_pa, w_pb, w_out,
           post_norm_g):
    depth = w_in.shape[0]
    assert depth == 1, "single-layer trunk"
    BP, LP, D = x_prompt.shape
    BS, LS, _ = x_sample.shape
    n_pool = cache_k.shape[1]

    wt3 = jnp.swapaxes(w_in, 1, 2)[0]
    pre_g = pre_norm_g[0][None, :]
    wts = (wt3, conv_w[0],
           _lane_row(a_log[0], LANE_DECAY), _lane_row(dt_bias[0], LANE_DECAY), gdn_norm_g[0][None, :],
           _cast_bf16_call(w_pa), _cast_bf16_call(w_pb), _cast_bf16_call(w_out), post_norm_g[0][None, :])

    rows = BP + BS
    rows_pad = -(-rows // 16) * 16
    c_all = jnp.concatenate([c_prompt, c_sample, jnp.zeros((rows_pad - rows, D), F32)], axis=0)
    mod = _mod_call(c_all, w_ada[0], b_ada[0][None, :])

    h_p, x2d_p, gate3_p = _modulated_input(x_prompt, mod[0:BP], pre_g)
    h_s, x2d_s, gate3_s = _modulated_input(x_sample, mod[BP:BP + BS], pre_g)
    tn = 512
    big_a_p, big_a_s, aqkv_tail = _proj_nt_call(h_p, h_s, wt3, lambda j: j * tn, N_A, LP, tn, BF16, F32,
                                                "in_proj_a", tail_cols=A_CONV_CH)
    big_b_p, big_b_s = _proj_nt_call(h_p, h_s, wt3, functools.partial(_group_b_row, tn=tn), N_B, LP, tn, BF16,
                                     F32, "in_proj_b")
    conv_p = aqkv_tail[:, 8 - (CONV_W - 1):, 0:A_CONV_CH]

    attend_p = functools.partial(_dsa_prompt_call, B=BP, L=LP, QB=256)
    y_p, k_p, v_p, kidx_p, s_p = _mixers(
        x2d_p, gate3_p, big_a_p, big_b_p, h_p, None, None, attend_p, wts, B=BP, L=LP,
        gdn_chunk=min(GDN_CHUNK, LP), gdn_hp=8, gdn_nseg=8, act_dtype=BF16)

    ck = cache_k[0].reshape(n_pool * PAGE_SIZE * ATT_KV_HEADS, ATT_DH)
    cv = cache_v[0].reshape(n_pool * PAGE_SIZE * ATT_KV_HEADS, ATT_DH)
    ci = cache_kidx[0]
    attend_s = functools.partial(_dsa_sample_call, cache_k=ck, cache_v=cv, cache_kidx=ci, page_table=page_table,
                                 B=BS, T=LS)
    y_s, k_s, v_s, kidx_s, s_s = _mixers(
        x2d_s, gate3_s, big_a_s, big_b_s, h_s, state_conv[0], state_gdn[0], attend_s, wts, B=BS, L=LS,
        gdn_chunk=min(GDN_CHUNK, LS), gdn_hp=16, gdn_nseg=1, act_dtype=F32)
    conv_s = big_a_s[:, 0:A_CONV_CH].reshape(BS, LS, A_CONV_CH)[:, LS - (CONV_W - 1):, :]

    kv_shape_p = (1, BP, LP, ATT_KV_HEADS, ATT_DH)
    kv_shape_s = (1, BS, LS, ATT_KV_HEADS, ATT_DH)
    return (y_p, y_s,
            k_p.reshape(kv_shape_p), v_p.reshape(kv_shape_p), kidx_p.reshape(1, BP, LP, IDX_DH),
            s_p[None], conv_p[None],
            k_s.reshape(kv_shape_s), v_s.reshape(kv_shape_s), kidx_s.reshape(1, BS, LS, IDX_DH),
            s_s[None], conv_s[None])
```

```python
import functools
import math

import jax
import jax.numpy as jnp
from jax import lax
from jax.experimental import pallas as pl
from jax.experimental.pallas import tpu as pltpu

F32 = jnp.float32
BF16 = jnp.bfloat16
I32 = jnp.int32

D_MODEL = 2048
PAGE_SIZE = 128
GDN_QK_HEADS = 16
GDN_V_HEADS = 32
GDN_DK = 128
GDN_DV = 128
CONV_W = 4
GDN_CHUNK = 64
ATT_HEADS = 16
ATT_KV_HEADS = 2
ATT_DH = 128
IDX_HEADS = 16
IDX_DH = 128
TOPK_MAX = 256
NORM_EPS = 1e-6
L2_EPS = 1e-6

A_QK = GDN_QK_HEADS * GDN_DK
A_V = GDN_V_HEADS * GDN_DV
A_CONV_CH = 2 * A_QK + A_V
B_Q = ATT_HEADS * ATT_DH
B_KV = ATT_KV_HEADS * ATT_DH
IDX_Q = IDX_HEADS * IDX_DH
SPLIT_SIZES = (A_CONV_CH, A_V, GDN_V_HEADS, GDN_V_HEADS, B_Q, B_KV, B_KV, B_Q, IDX_Q, IDX_DH, IDX_HEADS,
               D_MODEL, D_MODEL)
B_SEGS = (4, 7, 8, 11, 12)
OFF_AQKV = 0
OFF_AZ = OFF_AQKV + A_CONV_CH
N_A = OFF_AZ + A_V
OFF_BQ = 0
OFF_BZ = OFF_BQ + B_Q
OFF_IQ = OFF_BZ + B_Q
OFF_GA = OFF_IQ + IDX_Q
OFF_GB = OFF_GA + D_MODEL
N_B = OFF_GB + D_MODEL
LANE_BETA = 0
LANE_DECAY = GDN_V_HEADS
LANE_IW = 2 * GDN_V_HEADS

VMEM_LIMIT = 56 * 1024 * 1024
NEG_INF = float("-inf")
INT_MIN = -2 ** 31


def _cparams(sem):
    return pltpu.CompilerParams(dimension_semantics=sem, vmem_limit_bytes=VMEM_LIMIT)


def _dot(a, b):
    return jnp.dot(a, b, preferred_element_type=F32)


def _dot_nt(a, b):
    return lax.dot_general(a, b, (((1,), (1,)), ((), ())), preferred_element_type=F32)


def _dot_tn(a, b):
    return lax.dot_general(a, b, (((0,), (0,)), ((), ())), preferred_element_type=F32)


def _split2(x):
    hi = x.astype(BF16)
    lo = (x - hi.astype(F32)).astype(BF16)
    return hi, lo


def _split3(x):
    x1 = x.astype(BF16)
    r = x - x1.astype(F32)
    x2 = r.astype(BF16)
    x3 = (r - x2.astype(F32)).astype(BF16)
    return x1, x2, x3


def _dot_3pass(a, b):
    ah, al = _split2(a)
    bh, bl = _split2(b)
    return _dot(ah, bh) + _dot(al, bh) + _dot(ah, bl)


def _silu(x):
    h = 0.5 * x
    return h + h * jnp.tanh(h)


def _mod_kernel(c_ref, w_ref, b_ref, o_ref):
    o_ref[...] = _dot_3pass(_silu(c_ref[...]), w_ref[...]) + b_ref[...]


def _mod_call(c_pad, w_ada, b_ada):
    rows, d = c_pad.shape
    n = w_ada.shape[1]
    tn = 768
    return pl.pallas_call(
        _mod_kernel,
        grid=(n // tn,),
        in_specs=[pl.BlockSpec((rows, d), lambda j: (0, 0)),
                  pl.BlockSpec((d, tn), lambda j: (0, j)),
                  pl.BlockSpec((1, tn), lambda j: (0, j))],
        out_specs=pl.BlockSpec((rows, tn), lambda j: (0, j)),
        out_shape=jax.ShapeDtypeStruct((rows, n), F32),
        compiler_params=_cparams(("arbitrary",)),
        name="adaln_mod",
    )(c_pad, w_ada, b_ada)


def _norm_kernel(x_ref, g_ref, scale_ref, shift_ref, o_ref):
    x = x_ref[...]
    y = x * lax.rsqrt(jnp.mean(x * x, axis=-1, keepdims=True) + NORM_EPS) * g_ref[...]
    o_ref[...] = (y * (1.0 + scale_ref[...]) + shift_ref[...]).astype(o_ref.dtype)


def _norm_call(x2d, g_row, scale3, shift3, tm, tiles_per_group):
    m, d = x2d.shape
    r = scale3.shape[1]
    mod_spec = pl.BlockSpec((None, r, d), lambda i: (i // tiles_per_group, 0, 0))
    return pl.pallas_call(
        _norm_kernel,
        grid=(m // tm,),
        in_specs=[pl.BlockSpec((tm, d), lambda i: (i, 0)),
                  pl.BlockSpec((1, d), lambda i: (0, 0)),
                  mod_spec, mod_spec],
        out_specs=pl.BlockSpec((tm, d), lambda i: (i, 0)),
        out_shape=jax.ShapeDtypeStruct((m, d), BF16),
        compiler_params=_cparams(("parallel",)),
        name="norm_modulate",
    )(x2d, g_row, scale3, shift3)


SEG_OFFS = tuple(sum(SPLIT_SIZES[:i]) for i in range(len(SPLIT_SIZES) + 1))


def _proj_nt_kernel(h_ref, w_ref, h2_ref, o_ref, o2_ref, *tail_ref, tail_tiles):
    w16 = w_ref[...].astype(BF16)
    acc = _dot_nt(h_ref[...], w16)
    o_ref[...] = acc.astype(o_ref.dtype)
    if tail_tiles:
        wanted = pl.program_id(1) < tail_tiles
        tail_ref[0][...] = jnp.where(wanted, acc[acc.shape[0] - 8:, :], 0.0)

    is_last = pl.program_id(0) == pl.num_programs(0) - 1

    @pl.when(is_last)
    def _():
        o2_ref[...] = _dot_nt(h2_ref[...], w16).astype(o2_ref.dtype)

    @pl.when(jnp.logical_not(is_last))
    def _():
        o2_ref[...] = jnp.zeros(o2_ref.shape, o2_ref.dtype)


def _proj_nt_call(h, h2, wt3, row_of_tile, n_cols, tm, tn, out_dtype, out2_dtype, name, tail_cols=0):
    m, k = h.shape
    m2 = h2.shape[0]
    n_i, n_j = m // tm, n_cols // tn
    tail_tiles = tail_cols // tn
    out_specs = [pl.BlockSpec((tm, tn), lambda i, j: (i, j)),
                 pl.BlockSpec((m2, tn), lambda i, j: (0, jnp.where(i == n_i - 1, j, n_j)))]
    out_shape = [jax.ShapeDtypeStruct((m, n_cols), out_dtype),
                 jax.ShapeDtypeStruct((m2, n_cols + tn), out2_dtype)]
    if tail_tiles:
        out_specs.append(pl.BlockSpec((None, 8, tn), lambda i, j: (i, 0, jnp.minimum(j, tail_tiles))))
        out_shape.append(jax.ShapeDtypeStruct((n_i, 8, tail_cols + tn), F32))
    return pl.pallas_call(
        functools.partial(_proj_nt_kernel, tail_tiles=tail_tiles),
        grid=(n_i, n_j),
        in_specs=[pl.BlockSpec((tm, k), lambda i, j: (i, 0)),
                  pl.BlockSpec((pl.Element(tn), pl.Element(k)),
                               lambda i, j: (pl.multiple_of(row_of_tile(j), 16), 0)),
                  pl.BlockSpec((m2, k), lambda i, j: (0, 0))],
        out_specs=out_specs,
        out_shape=out_shape,
        compiler_params=_cparams(("arbitrary", "arbitrary")),
        name=name,
    )(h, wt3, h2)


def _group_b_row(j, tn):
    per = B_Q // tn
    seg = j // per
    start = SEG_OFFS[B_SEGS[-1]]
    for s in range(len(B_SEGS) - 2, -1, -1):
        start = jnp.where(seg == s, SEG_OFFS[B_SEGS[s]], start)
    return start + (j % per) * tn


def _proj_tail_kernel(h_ref, wkv_ref, wik_ref, wg_ref, wiw_ref, k_ref, v_ref, kidx_ref, small_ref):
    h = h_ref[...]
    kv = _dot_nt(h, wkv_ref[...].astype(BF16))
    rows = h.shape[0]
    for n in range(ATT_KV_HEADS):
        head_rows = pl.ds(n, rows, stride=ATT_KV_HEADS)
        k_ref[head_rows, :] = kv[:, n * ATT_DH:(n + 1) * ATT_DH]
        v_ref[head_rows, :] = kv[:, B_KV + n * ATT_DH:B_KV + (n + 1) * ATT_DH]
    kidx_ref[...] = _dot_nt(h, wik_ref[...].astype(BF16))
    gate = _dot_nt(h, wg_ref[...].astype(BF16))
    iw = _dot_nt(h, wiw_ref[...].astype(BF16))
    lane = lax.broadcasted_iota(I32, gate.shape, 1)
    small_ref[...] = jnp.where(lane < LANE_IW, gate, jnp.where(lane < LANE_IW + IDX_HEADS, iw, 0.0))


def _proj_tail_call(h, wt3, tm):
    m, k = h.shape
    kvh = ATT_KV_HEADS
    out_blocks = ((kvh * tm, ATT_DH), (kvh * tm, ATT_DH), (tm, IDX_DH), (tm, 128))
    out_arrays = ((kvh * m, ATT_DH), (kvh * m, ATT_DH), (m, IDX_DH), (m, 128))
    beta0, kv0, ik0, iw0 = SEG_OFFS[2], SEG_OFFS[5], SEG_OFFS[9], SEG_OFFS[10]
    assert SEG_OFFS[3] - beta0 == LANE_DECAY and SEG_OFFS[4] - beta0 == LANE_IW and SEG_OFFS[6] - kv0 == B_KV

    def rows(start, size):
        return pl.BlockSpec((pl.Element(size), pl.Element(k)), lambda i: (start, 0))

    return pl.pallas_call(
        _proj_tail_kernel,
        grid=(m // tm,),
        in_specs=[pl.BlockSpec((tm, k), lambda i: (i, 0)),
                  rows(kv0, 2 * B_KV), rows(ik0, IDX_DH), rows(beta0, 128), rows(iw0 - LANE_IW, 128)],
        out_specs=[pl.BlockSpec(blk, lambda i: (i, 0)) for blk in out_blocks],
        out_shape=[jax.ShapeDtypeStruct(shp, F32) for shp in out_arrays],
        compiler_params=_cparams(("parallel",)),
        name="in_proj_tail",
    )(h, wt3, wt3, wt3, wt3)


def _gates_kernel(sm_ref, alog_ref, dtb_ref, o_ref, *, chunk, sub):
    ri = lax.broadcasted_iota(I32, (sub, sub), 0)
    ci = lax.broadcasted_iota(I32, (sub, sub), 1)
    shift = int(math.log2(chunk))
    tri = jnp.where((ri >= ci) & ((ri >> shift) == (ci >> shift)), 1.0, 0.0).astype(BF16)
    lane = lax.broadcasted_iota(I32, (sub, 128), 1)
    for r0 in range(0, sm_ref.shape[0], sub):
        x = sm_ref[r0:r0 + sub, :]
        beta = jax.nn.sigmoid(x)
        z = x + dtb_ref[...]
        softplus = jnp.maximum(z, 0.0) + jnp.log1p(jnp.exp(-jnp.abs(z)))
        g = -jnp.exp(alog_ref[...]) * softplus
        g = jnp.where((lane >= LANE_DECAY) & (lane < LANE_DECAY + GDN_V_HEADS), g, 0.0)
        g1, g2, g3 = _split3(g)
        gc = _dot(tri, g1) + _dot(tri, g2) + _dot(tri, g3)
        o_ref[r0:r0 + sub, :] = jnp.where(lane < GDN_V_HEADS, beta, gc)


def _gates_call(small, alog_row, dtb_row, chunk):
    m = small.shape[0]
    sub = 256
    tm = min(m, 1024)
    return pl.pallas_call(
        functools.partial(_gates_kernel, chunk=chunk, sub=sub),
        grid=(m // tm,),
        in_specs=[pl.BlockSpec((tm, 128), lambda i: (i, 0)),
                  pl.BlockSpec((1, 128), lambda i: (0, 0)),
                  pl.BlockSpec((1, 128), lambda i: (0, 0))],
        out_specs=pl.BlockSpec((tm, 128), lambda i: (i, 0)),
        out_shape=jax.ShapeDtypeStruct((m, 128), F32),
        compiler_params=_cparams(("parallel",)),
        name="gdn_gates",
    )(small, alog_row, dtb_row)


def _unrolled_loop(n, unroll, fn):
    unroll = min(unroll, n)
    assert n % unroll == 0

    def body(t, carry):
        for u in range(unroll):
            fn(t * unroll + u)
        return carry

    lax.fori_loop(0, n // unroll, body, 0)


def _gdn_kernel(*refs, L, C, HP, NSEG, has_state):
    ins = 13 if has_state else 12
    (q_ref, k_ref, v_ref, z_ref, bq_ref, bk_ref, bv_ref, wq_ref, wk_ref, wv_ref, gates_ref, gn_ref) = refs[:12]
    s0_ref = refs[12] if has_state else None
    (y_ref, sout_ref, xq_s, xk_s, xv_s, beta_s, g_s, st_s, a_s, p_s, rhs_s, kt_s, qkd_s, qd_s,
     m_s, n_s, qe_s, o_s) = refs[ins:]
    nh = 2 * HP
    nc = L // C
    hq0 = pl.program_id(1) * HP

    def first_segment():
        xq_s[0:8, :] = bq_ref[...]
        xk_s[0:8, :] = bk_ref[...]
        xv_s[0:8, :] = bv_ref[...]
        if has_state:
            st_s[...] = s0_ref[...]
        else:
            st_s[...] = jnp.zeros(st_s.shape, F32)

    def next_segment():
        xq_s[0:8, :] = xq_s[L:L + 8, :]
        xk_s[0:8, :] = xk_s[L:L + 8, :]
        xv_s[0:8, :] = xv_s[L:L + 8, :]

    if NSEG == 1:
        first_segment()
    else:
        pl.when(pl.program_id(2) == 0)(first_segment)
        pl.when(pl.program_id(2) > 0)(next_segment)
    xq_s[8:L + 8, :] = q_ref[...].astype(F32)
    xk_s[8:L + 8, :] = k_ref[...].astype(F32)
    xv_s[8:L + 8, :] = v_ref[...].astype(F32)

    gates = gates_ref[...]
    lane = lax.broadcasted_iota(I32, gates.shape, 1)
    for idx in range(nh):
        hv = 2 * hq0 + idx
        bcol = jnp.sum(jnp.where(lane == hv + LANE_BETA, gates, 0.0), axis=-1, keepdims=True)
        gcol = jnp.sum(jnp.where(lane == hv + LANE_DECAY, gates, 0.0), axis=-1, keepdims=True)
        beta_s[idx] = jnp.broadcast_to(bcol, (L, 128))
        g_s[idx] = jnp.broadcast_to(gcol, (L, 128))

    ii = lax.broadcasted_iota(I32, (C, C), 0)
    jj = lax.broadcasted_iota(I32, (C, C), 1)
    gn = gn_ref[...]

    def conv(x_s, r0, cols, w):
        win = x_s[pl.ds(r0, C + 8), cols]
        acc = (win[5:5 + C] * w[0:1] + win[6:6 + C] * w[1:2] + win[7:7 + C] * w[2:3] + win[8:8 + C] * w[3:4])
        return _silu(acc)

    def l2n(x):
        return x * lax.rsqrt(jnp.sum(x * x, axis=-1, keepdims=True) + L2_EPS)

    def lanes(i, width):
        if isinstance(i, int):
            return slice(i * width, (i + 1) * width)
        return pl.ds(pl.multiple_of(i * width, width), width)

    def pair_terms(j):
        c, p = (j, 0) if HP == 1 else (j // HP, j % HP)
        r0 = pl.multiple_of(c * C, C)
        rows = pl.ds(r0, C)
        lq, lv = lanes(p, 128), lanes(p, 256)
        q = l2n(conv(xq_s, r0, lq, wq_ref[:, lq])) * (GDN_DK ** -0.5)
        k = l2n(conv(xk_s, r0, lq, wk_ref[:, lq]))
        vc = conv(xv_s, r0, lv, wv_ref[:, lv])
        k16 = k.astype(BF16)
        kk = _dot_nt(k16, k16)
        qk = _dot_nt(q.astype(BF16), k16)
        for r in range(2):
            idx = 2 * p + r
            e = c * nh + idx
            gb = g_s[idx, rows, :]
            bb = beta_s[idx, rows, :]
            e_g = jnp.exp(gb)
            e_tail = jnp.exp(gb[C - 1:C, :] - gb)
            if C < 128:
                gpad = jnp.concatenate([gb, jnp.zeros((128 - C, 128), F32)], axis=0)
            else:
                gpad = gb
            dexp = gb[:, 0:C] - gpad.T[0:C, 0:C]
            dm = jnp.exp(jnp.where(ii >= jj, dexp, NEG_INF))
            a = jnp.where(ii > jj, kk * dm, 0.0) * bb[:, 0:C]
            a_s[e] = a.astype(a_s.dtype)
            p_s[e] = jnp.where(ii == jj, 1.0, 0.0) - a
            rhs_s[e] = jnp.concatenate([vc[:, r * 128:(r + 1) * 128] * bb, k * bb * e_g], axis=1).astype(rhs_s.dtype)
            kt_s[e] = (k * e_tail).astype(kt_s.dtype)
            qkd_s[e] = (qk * dm).astype(qkd_s.dtype)
            qd_s[e] = q * e_g

    def square(e):
        ak = a_s[e].astype(BF16)
        a_s[e] = _dot(ak, ak).astype(a_s.dtype)

    def inverse_stage(e, last):
        ak = a_s[e].astype(BF16)
        pk = p_s[e]
        p_s[e] = pk + _dot(pk.astype(BF16), ak)
        if not last:
            a_s[e] = _dot(ak, ak).astype(a_s.dtype)

    def solve(e):
        rhs_s[e] = _dot(p_s[e].astype(BF16), rhs_s[e].astype(BF16)).astype(rhs_s.dtype)

    def state_terms(e):
        sol16 = rhs_s[e].astype(BF16)
        mn = _dot_tn(kt_s[e].astype(BF16), sol16)
        qo = _dot(qkd_s[e].astype(BF16), sol16)
        n_s[e] = mn[:, 0:GDN_DV]
        m_s[e] = mn[:, GDN_DV:].astype(m_s.dtype)
        o_s[e] = qo[:, 0:GDN_DV]
        qe_s[e] = (qd_s[e] - qo[:, GDN_DV:]).astype(qe_s.dtype)

    def head_step(c, idx):
        e = c * nh + idx
        r0 = pl.multiple_of(c * C, C)
        rows = pl.ds(r0, C)
        s = st_s[idx]
        s16 = s.astype(BF16)
        o = _dot(qe_s[e].astype(BF16), s16) + o_s[e]
        e_last = jnp.exp(g_s[idx, pl.ds(r0 + C - 1, 1), :])
        st_s[idx] = s * e_last - _dot(m_s[e].astype(BF16), s16) + n_s[e]
        on = o * lax.rsqrt(jnp.mean(o * o, axis=-1, keepdims=True) + NORM_EPS) * gn
        zz = z_ref[rows, lanes(idx, 128)].astype(F32)
        y_ref[rows, lanes(idx, 128)] = (on * _silu(zz)).astype(y_ref.dtype)

    ne = nc * nh
    stages = int(math.log2(C)) - 1
    _unrolled_loop(nc * HP, 2 if C >= 64 else 4, pair_terms)
    _unrolled_loop(ne, 32, square)
    for st in range(stages):
        _unrolled_loop(ne, 32, functools.partial(inverse_stage, last=(st == stages - 1)))
    _unrolled_loop(ne, 32, solve)
    _unrolled_loop(ne, 32, state_terms)

    def chunk_step(c, carry):
        if nh <= 4:
            for idx in range(nh):
                head_step(c, idx)
        else:
            _unrolled_loop(nh, 16, functools.partial(head_step, c))
        return carry

    lax.fori_loop(0, nc, chunk_step, 0)
    sout_ref[...] = st_s[...]


def _gdn_call(big, convbuf8, conv_w, gates, gn_row, s0, *, B, L, C, HP, NSEG, y_dtype):
    has_state = s0 is not None
    ls = L // NSEG
    ne = (ls // C) * 2 * HP
    half = BF16 if C % 16 == 0 else F32
    nq = GDN_QK_HEADS // HP
    wq, wv = HP * 128, HP * 256
    kq0 = A_QK // wq
    v0 = (2 * A_QK) // wv
    z0 = OFF_AZ // wv
    in_specs = [
        pl.BlockSpec((ls, wq), lambda b, h, s: (b * NSEG + s, h)),
        pl.BlockSpec((ls, wq), lambda b, h, s: (b * NSEG + s, kq0 + h)),
        pl.BlockSpec((ls, wv), lambda b, h, s: (b * NSEG + s, v0 + h)),
        pl.BlockSpec((ls, wv), lambda b, h, s: (b * NSEG + s, z0 + h)),
        pl.BlockSpec((None, 8, wq), lambda b, h, s: (b, 0, h)),
        pl.BlockSpec((None, 8, wq), lambda b, h, s: (b, 0, kq0 + h)),
        pl.BlockSpec((None, 8, wv), lambda b, h, s: (b, 0, v0 + h)),
        pl.BlockSpec((CONV_W, wq), lambda b, h, s: (0, h)),
        pl.BlockSpec((CONV_W, wq), lambda b, h, s: (0, kq0 + h)),
        pl.BlockSpec((CONV_W, wv), lambda b, h, s: (0, v0 + h)),
        pl.BlockSpec((ls, 128), lambda b, h, s: (b * NSEG + s, 0)),
        pl.BlockSpec((1, 128), lambda b, h, s: (0, 0)),
    ]
    args = [big, big, big, big, convbuf8, convbuf8, convbuf8, conv_w, conv_w, conv_w, gates, gn_row]
    if has_state:
        in_specs.append(pl.BlockSpec((None, 2 * HP, GDN_DK, GDN_DV), lambda b, h, s: (b, h, 0, 0)))
        args.append(s0)
    return pl.pallas_call(
        functools.partial(_gdn_kernel, L=ls, C=C, HP=HP, NSEG=NSEG, has_state=has_state),
        grid=(B, nq, NSEG),
        in_specs=in_specs,
        out_specs=[pl.BlockSpec((ls, wv), lambda b, h, s: (b * NSEG + s, h)),
                   pl.BlockSpec((None, 2 * HP, GDN_DK, GDN_DV), lambda b, h, s: (b, h, 0, 0))],
        out_shape=[jax.ShapeDtypeStruct((B * L, A_V), y_dtype),
                   jax.ShapeDtypeStruct((B, GDN_V_HEADS, GDN_DK, GDN_DV), F32)],
        scratch_shapes=[pltpu.VMEM((ls + 8, wq), F32), pltpu.VMEM((ls + 8, wq), F32), pltpu.VMEM((ls + 8, wv), F32),
                        pltpu.VMEM((2 * HP, ls, 128), F32), pltpu.VMEM((2 * HP, ls, 128), F32),
                        pltpu.VMEM((2 * HP, GDN_DK, GDN_DV), F32),
                        pltpu.VMEM((ne, C, C), half), pltpu.VMEM((ne, C, C), F32),
                        pltpu.VMEM((ne, C, GDN_DV + GDN_DK), half), pltpu.VMEM((ne, C, GDN_DK), half),
                        pltpu.VMEM((ne, C, C), half), pltpu.VMEM((ne, C, GDN_DK), F32),
                        pltpu.VMEM((ne, GDN_DK, GDN_DV), BF16), pltpu.VMEM((ne, GDN_DK, GDN_DV), F32),
                        pltpu.VMEM((ne, C, GDN_DK), half), pltpu.VMEM((ne, C, GDN_DV), F32)],
        compiler_params=_cparams(("parallel", "arbitrary", "arbitrary")),
        name="gdn_prompt" if not has_state else "gdn_sample",
    )(*args)


KEY_NEG_INF = -2 ** 31 + 0x7FFFFF


def _key_to_float(key):
    bits = key ^ ((key >> 31) & jnp.int32(0x7FFFFFFF))
    return pltpu.bitcast(bits, F32)


def _count_true(mask):
    ones = jnp.where(mask, 1.0, 0.0)
    parts = [ones[:, i * 128:(i + 1) * 128] for i in range(mask.shape[1] // 128)]
    while len(parts) > 1:
        parts = [a + b for a, b in zip(parts[0::2], parts[1::2])] + ([parts[-1]] if len(parts) % 2 else [])
    return jnp.sum(parts[0], axis=-1, keepdims=True)


def _selection_bias(score_ref, bias_ref, qpos0, rows, width, topk, bits=1):
    def step(it, t):
        best = t
        for digit in range(1, 2 ** bits):
            cand = t + jnp.left_shift(jnp.int32(digit), 32 - bits * (it + 1))
            cnt = _count_true(score_ref[:, 0:width] >= _key_to_float(cand))
            best = jnp.where((cnt >= float(topk)) | (cand <= KEY_NEG_INF), cand, best)
        return best

    thr = _key_to_float(lax.fori_loop(0, 32 // bits, step, jnp.full((rows, 1), INT_MIN, I32)))
    score = score_ref[:, 0:width]
    causal = (lax.broadcasted_iota(I32, (rows, width), 1)
              <= qpos0 + lax.broadcasted_iota(I32, (rows, width), 0))
    sel = (score >= thr) & causal
    bias_ref[:, 0:width] = jnp.where(sel, 0.0, NEG_INF)
    n_sel = _count_true(sel)
    has_ties = jnp.max(n_sel) > float(topk)

    @pl.when(has_ties)
    def _():
        need = float(topk) - _count_true(score > thr)
        eq16 = jnp.where(score == thr, 1.0, 0.0).astype(BF16)

        def tile(j, carry):
            c0 = pl.multiple_of(j * 128, 128)
            ri = lax.broadcasted_iota(I32, (width, 128), 0)
            ci = lax.broadcasted_iota(I32, (width, 128), 1) + c0
            upper = jnp.where(ri <= ci, 1.0, 0.0).astype(BF16)
            prefix = _dot(eq16, upper)
            s_t = score_ref[:, pl.ds(c0, 128)]
            keep = (s_t > thr) | ((s_t == thr) & (prefix <= need))
            old = bias_ref[:, pl.ds(c0, 128)]
            bias_ref[:, pl.ds(c0, 128)] = jnp.where(keep, old, NEG_INF)
            return carry

        lax.fori_loop(0, width // 128, tile, 0)


SOFTMAX_LOG2_SCALE = (ATT_DH ** -0.5) * math.log2(math.e)


def _dsa_prompt_kernel(qi_ref, wi_ref, kidx_ref, q_ref, k_ref, v_ref, z_ref, o_ref,
                       score_s, bias_s, kidx16_s, k16_s, vext_s, m_s, acc_s, *, QB, SUB, KT, L, topk):
    i = pl.program_id(1)
    q0 = i * QB
    nt = (q0 + QB - 1) // KT + 1
    groups = ATT_HEADS // ATT_KV_HEADS

    @pl.when(i == 0)
    def _():
        kidx16_s[...] = kidx_ref[...].astype(BF16)
        for n in range(ATT_KV_HEADS):
            head_rows = pl.ds(n, L, stride=ATT_KV_HEADS)
            k16_s[:, n * ATT_DH:(n + 1) * ATT_DH] = k_ref[head_rows, :].astype(BF16)
            vext_s[n, :, 0:ATT_DH] = v_ref[head_rows, :].astype(BF16)
            vext_s[n, :, ATT_DH:] = jnp.ones((L, ATT_DH), BF16)

    rowpos = q0 + lax.broadcasted_iota(I32, (SUB, KT), 0)
    col = lax.broadcasted_iota(I32, (SUB, KT), 1)

    def tile_rows(t):
        return pl.ds(pl.multiple_of(t * KT, KT), KT)

    subs = [slice(r, r + SUB) for r in range(0, QB, SUB)]

    def index_tile(t, carry):
        kid = kidx16_s[tile_rows(t), :]
        for rs in subs:
            wi = wi_ref[rs, :] * ((IDX_DH ** -0.5) * (IDX_HEADS ** -0.5))
            acc = jnp.zeros((SUB, KT), F32)
            for h in range(IDX_HEADS):
                lg = _dot_nt(qi_ref[rs, h * IDX_DH:(h + 1) * IDX_DH], kid)
                acc = acc + jnp.maximum(lg, 0.0) * wi[:, LANE_IW + h:LANE_IW + h + 1]
            causal = col + t * KT <= rowpos + rs.start
            score_s[rs, tile_rows(t)] = jnp.where(causal, acc, NEG_INF)
        return carry

    lax.fori_loop(0, nt, index_tile, 0)
    for w in range(1, L // KT + 1):
        @pl.when(nt == w)
        def _():
            _selection_bias(score_s, bias_s, q0, QB, w * KT, topk)

    m_s[...] = jnp.full(m_s.shape, float(jnp.finfo(F32).min), F32)
    acc_s[...] = jnp.zeros(acc_s.shape, F32)

    def attend_tile(t, carry):
        for n in range(ATT_KV_HEADS):
            kn = k16_s[tile_rows(t), n * ATT_DH:(n + 1) * ATT_DH]
            vn = vext_s[n, tile_rows(t), :]
            for g in range(groups):
                h = n * groups + g
                for rs in subs:
                    sb = _dot_nt(q_ref[rs, h * ATT_DH:(h + 1) * ATT_DH], kn) + bias_s[rs, tile_rows(t)]
                    m_old = m_s[h, rs, :]
                    m_new = jnp.maximum(m_old, jnp.max(sb, axis=-1, keepdims=True))
                    p = jnp.exp2((sb - jnp.concatenate([m_new] * (KT // 128), axis=1)) * SOFTMAX_LOG2_SCALE)
                    alpha = jnp.exp2((m_old - m_new) * SOFTMAX_LOG2_SCALE)
                    m_s[h, rs, :] = m_new
                    acc_s[h, rs, :] = (acc_s[h, rs, :] * jnp.concatenate([alpha, alpha], axis=1)
                                       + _dot(p.astype(BF16), vn))
        return carry

    lax.fori_loop(0, nt, attend_tile, 0)
    for h in range(ATT_HEADS):
        cols = slice(h * ATT_DH, (h + 1) * ATT_DH)
        zz = z_ref[:, cols].astype(F32)
        acc = acc_s[h]
        o_ref[:, cols] = (acc[:, 0:ATT_DH] / acc[:, ATT_DH:] * _silu(zz)).astype(o_ref.dtype)


def _dsa_prompt_call(big, small, kidx, k, v, *, B, L, QB):
    topk = min(TOPK_MAX, L // 4)
    nqb = L // QB
    iq0 = OFF_IQ // B_Q
    bq0 = OFF_BQ // B_Q
    bz0 = OFF_BZ // B_Q
    return pl.pallas_call(
        functools.partial(_dsa_prompt_kernel, QB=QB, SUB=128, KT=256, L=L, topk=topk),
        grid=(B, nqb),
        in_specs=[pl.BlockSpec((QB, IDX_Q), lambda b, i: (b * nqb + i, iq0)),
                  pl.BlockSpec((QB, 128), lambda b, i: (b * nqb + i, 0)),
                  pl.BlockSpec((L, IDX_DH), lambda b, i: (b, 0)),
                  pl.BlockSpec((QB, B_Q), lambda b, i: (b * nqb + i, bq0)),
                  pl.BlockSpec((ATT_KV_HEADS * L, ATT_DH), lambda b, i: (b, 0)),
                  pl.BlockSpec((ATT_KV_HEADS * L, ATT_DH), lambda b, i: (b, 0)),
                  pl.BlockSpec((QB, B_Q), lambda b, i: (b * nqb + i, bz0))],
        out_specs=pl.BlockSpec((QB, B_Q), lambda b, i: (b * nqb + i, 0)),
        out_shape=jax.ShapeDtypeStruct((B * L, B_Q), BF16),
        scratch_shapes=[pltpu.VMEM((QB, L), F32), pltpu.VMEM((QB, L), F32),
                        pltpu.VMEM((L, IDX_DH), BF16), pltpu.VMEM((L, B_KV), BF16),
                        pltpu.VMEM((ATT_KV_HEADS, L, 2 * ATT_DH), BF16),
                        pltpu.VMEM((ATT_HEADS, QB, 128), F32), pltpu.VMEM((ATT_HEADS, QB, 2 * ATT_DH), F32)],
        compiler_params=_cparams(("parallel", "arbitrary")),
        name="dsa_prompt",
    )(big, small, kidx, big, k, v, big)


def _dsa_sample_kernel(pt_ref, qi_ref, wi_ref, kin_ref, q_ref, kn_ref, vn_ref, z_ref, ck_hbm, cv_hbm, ci_hbm,
                       o_ref, kbuf, vbuf, ibuf, sems, score_s, bias_s, *, NP, T, topk):
    b = pl.program_id(0)
    nb = pl.num_programs(0)
    slot = b % 2
    past = NP * PAGE_SIZE
    width = past + 128

    kvp = ATT_KV_HEADS * PAGE_SIZE

    def page_copies(bb, sl, p):
        page = pt_ref[bb * NP + p]
        rows = pl.ds(pl.multiple_of(p * PAGE_SIZE, PAGE_SIZE), PAGE_SIZE)
        src2 = pl.ds(pl.multiple_of(page * kvp, kvp), kvp)
        dst2 = pl.ds(pl.multiple_of(p * kvp, kvp), kvp)
        return (pltpu.make_async_copy(ck_hbm.at[src2], kbuf.at[sl, dst2], sems.at[0, sl]),
                pltpu.make_async_copy(cv_hbm.at[src2], vbuf.at[sl, dst2], sems.at[1, sl]),
                pltpu.make_async_copy(ci_hbm.at[page], ibuf.at[sl, rows], sems.at[2, sl]))

    def start_fetch(bb, sl):
        def body(p, carry):
            for cp in page_copies(bb, sl, p):
                cp.start()
            return carry
        lax.fori_loop(0, NP, body, 0, unroll=8)

    def wait_fetch(bb, sl):
        def body(p, carry):
            for cp in page_copies(bb, sl, p):
                cp.wait()
            return carry
        lax.fori_loop(0, NP, body, 0, unroll=8)

    @pl.when(b == 0)
    def _():
        for sl in range(2):
            kbuf[sl, ATT_KV_HEADS * past:, :] = jnp.zeros((ATT_KV_HEADS * 128, ATT_DH), F32)
            vbuf[sl, ATT_KV_HEADS * past:, :] = jnp.zeros((ATT_KV_HEADS * 128, ATT_DH), F32)
            ibuf[sl, past:, :] = jnp.zeros((128, IDX_DH), F32)
        start_fetch(0, 0)

    @pl.when(b + 1 < nb)
    def _():
        start_fetch(b + 1, 1 - slot)

    new_rows = pl.ds(ATT_KV_HEADS * past, ATT_KV_HEADS * T)
    kbuf[slot, new_rows, :] = kn_ref[...]
    vbuf[slot, new_rows, :] = vn_ref[...]
    ibuf[slot, past:past + T, :] = kin_ref[...]
    wait_fetch(b, slot)

    qi = qi_ref[...]
    qis = jnp.concatenate([qi[:, h * IDX_DH:(h + 1) * IDX_DH] for h in range(IDX_HEADS)], axis=0).astype(BF16)
    wi = wi_ref[...] * ((IDX_DH ** -0.5) * (IDX_HEADS ** -0.5))
    lg = _dot_nt(qis, ibuf[slot].astype(BF16))
    score = jnp.zeros((T, width), F32)
    for h in range(IDX_HEADS):
        score = score + jnp.maximum(lg[h * T:(h + 1) * T, :], 0.0) * wi[:, LANE_IW + h:LANE_IW + h + 1]
    qpos = past + lax.broadcasted_iota(I32, (T, width), 0)
    kpos = lax.broadcasted_iota(I32, (T, width), 1)
    causal = kpos <= qpos
    score_s[...] = jnp.where(causal, score, NEG_INF)
    _selection_bias(score_s, bias_s, past, T, width, topk, bits=2)

    groups = ATT_HEADS // ATT_KV_HEADS
    q = q_ref[...]
    bias = jnp.concatenate([bias_s[...]] * groups, axis=0)
    for n in range(ATT_KV_HEADS):
        head_rows = pl.ds(n, width, stride=ATT_KV_HEADS)
        kn = kbuf[slot, head_rows, :].astype(BF16)
        vn = vbuf[slot, head_rows, :].astype(BF16)
        qs = jnp.concatenate([q[:, (n * groups + g) * ATT_DH:(n * groups + g + 1) * ATT_DH] for g in range(groups)],
                             axis=0).astype(BF16)
        s = _dot_nt(qs, kn) * (ATT_DH ** -0.5) + bias
        m = jnp.max(s, axis=-1, keepdims=True)
        p = jnp.exp(s - m)
        denom = jnp.sum(p, axis=-1, keepdims=True)
        o = _dot(p.astype(BF16), vn) / denom
        for g in range(groups):
            h = n * groups + g
            cols = slice(h * ATT_DH, (h + 1) * ATT_DH)
            zz = z_ref[:, cols]
            o_ref[:, cols] = o[g * T:(g + 1) * T, :] * _silu(zz)


def _dsa_sample_call(big, small, kidx, k, v, cache_k, cache_v, cache_kidx, page_table, *, B, T):
    npages = page_table.shape[1]
    past = npages * PAGE_SIZE
    topk = min(TOPK_MAX, (past + T) // 4)
    width = past + 128
    iq0 = OFF_IQ // B_Q
    bq0 = OFF_BQ // B_Q
    bz0 = OFF_BZ // B_Q
    grid_spec = pltpu.PrefetchScalarGridSpec(
        num_scalar_prefetch=1,
        grid=(B,),
        in_specs=[pl.BlockSpec((T, IDX_Q), lambda b, pt: (b, iq0)),
                  pl.BlockSpec((T, 128), lambda b, pt: (b, 0)),
                  pl.BlockSpec((T, IDX_DH), lambda b, pt: (b, 0)),
                  pl.BlockSpec((T, B_Q), lambda b, pt: (b, bq0)),
                  pl.BlockSpec((ATT_KV_HEADS * T, ATT_DH), lambda b, pt: (b, 0)),
                  pl.BlockSpec((ATT_KV_HEADS * T, ATT_DH), lambda b, pt: (b, 0)),
                  pl.BlockSpec((T, B_Q), lambda b, pt: (b, bz0)),
                  pl.BlockSpec(memory_space=pl.ANY),
                  pl.BlockSpec(memory_space=pl.ANY),
                  pl.BlockSpec(memory_space=pl.ANY)],
        out_specs=pl.BlockSpec((T, B_Q), lambda b, pt: (b, 0)),
        scratch_shapes=[pltpu.VMEM((2, ATT_KV_HEADS * width, ATT_DH), F32),
                        pltpu.VMEM((2, ATT_KV_HEADS * width, ATT_DH), F32),
                        pltpu.VMEM((2, width, IDX_DH), F32), pltpu.SemaphoreType.DMA((3, 2)),
                        pltpu.VMEM((T, width), F32), pltpu.VMEM((T, width), F32)])
    return pl.pallas_call(
        functools.partial(_dsa_sample_kernel, NP=npages, T=T, topk=topk),
        grid_spec=grid_spec,
        out_shape=jax.ShapeDtypeStruct((B * T, B_Q), F32),
        compiler_params=_cparams(("arbitrary",)),
        name="dsa_sample",
    )(page_table.reshape(-1), big, small, kidx, big, k, v, big, cache_k, cache_v, cache_kidx)


def _merge_kernel(ya_ref, yb_ref, ga_ref, gb_ref, wpa_ref, wpb_ref, o_ref):
    pa = _dot(ya_ref[...].astype(BF16), wpa_ref[...])
    pb = _dot(yb_ref[...].astype(BF16), wpb_ref[...])
    o = jax.nn.sigmoid(ga_ref[...].astype(F32)) * pa + jax.nn.sigmoid(gb_ref[...].astype(F32)) * pb
    o_ref[...] = o.astype(o_ref.dtype)


def _merge_call(ya, yb, big, wpa, wpb, tm, tn):
    m = ya.shape[0]
    n = wpa.shape[1]
    ga0 = OFF_GA // tn
    gb0 = OFF_GB // tn
    return pl.pallas_call(
        _merge_kernel,
        grid=(m // tm, n // tn),
        in_specs=[pl.BlockSpec((tm, A_V), lambda i, j: (i, 0)),
                  pl.BlockSpec((tm, B_Q), lambda i, j: (i, 0)),
                  pl.BlockSpec((tm, tn), lambda i, j: (i, ga0 + j)),
                  pl.BlockSpec((tm, tn), lambda i, j: (i, gb0 + j)),
                  pl.BlockSpec((A_V, tn), lambda i, j: (0, j)),
                  pl.BlockSpec((B_Q, tn), lambda i, j: (0, j))],
        out_specs=pl.BlockSpec((tm, tn), lambda i, j: (i, j)),
        out_shape=jax.ShapeDtypeStruct((m, n), BF16),
        compiler_params=_cparams(("parallel", "arbitrary")),
        name="merge_proj",
    )(ya, yb, big, big, wpa, wpb)


def _final_kernel(m_ref, w_ref, x_ref, gate_ref, g_ref, o_ref):
    t = _dot(m_ref[...], w_ref[...])
    y = t * lax.rsqrt(jnp.mean(t * t, axis=-1, keepdims=True) + NORM_EPS) * g_ref[...]
    o_ref[...] = x_ref[...] + gate_ref[...] * y


def _final_call(merged, w_out, x2d, gate3, g_row, tm, tiles_per_group):
    m, d = x2d.shape
    r = gate3.shape[1]
    return pl.pallas_call(
        _final_kernel,
        grid=(m // tm,),
        in_specs=[pl.BlockSpec((tm, d), lambda i: (i, 0)),
                  pl.BlockSpec((d, d), lambda i: (0, 0)),
                  pl.BlockSpec((tm, d), lambda i: (i, 0)),
                  pl.BlockSpec((None, r, d), lambda i: (i // tiles_per_group, 0, 0)),
                  pl.BlockSpec((1, d), lambda i: (0, 0))],
        out_specs=pl.BlockSpec((tm, d), lambda i: (i, 0)),
        out_shape=jax.ShapeDtypeStruct((m, d), F32),
        compiler_params=_cparams(("parallel",)),
        name="out_proj",
    )(merged, w_out, x2d, gate3, g_row)


def _cast_kernel(x_ref, o_ref):
    o_ref[...] = x_ref[...].astype(o_ref.dtype)


def _cast_bf16_call(w3):
    _, r, n = w3.shape
    tr = 256
    return pl.pallas_call(
        _cast_kernel,
        grid=(r // tr,),
        in_specs=[pl.BlockSpec((None, tr, n), lambda i: (0, i, 0))],
        out_specs=pl.BlockSpec((tr, n), lambda i: (i, 0)),
        out_shape=jax.ShapeDtypeStruct((r, n), BF16),
        compiler_params=_cparams(("parallel",)),
        name="w_cast",
    )(w3)


def _lane_row(vec, lane0):
    return jnp.zeros((1, 128), F32).at[0, lane0:lane0 + vec.shape[0]].set(vec.astype(F32))


def _modulated_input(x, mod, pre_g):
    B, L, D = x.shape
    M = B * L
    x2d = x.reshape(M, D)
    shift, scale, gate = mod[:, 0:D], mod[:, D:2 * D], mod[:, 2 * D:3 * D]
    if L >= 512:
        tm_n = 512
        per = L // tm_n
        scale3, shift3, gate3 = scale[:, None, :], shift[:, None, :], gate[:, None, :]
    else:
        tm_n = M
        per = 1
        scale3, shift3, gate3 = (jnp.repeat(t, L, axis=0)[None] for t in (scale, shift, gate))
    return _norm_call(x2d, pre_g, scale3, shift3, tm_n, per), x2d, gate3


def _mixers(x2d, gate3, big_a, big_b, h, conv_hist, s0, attend, wts, *, B, L, gdn_chunk, gdn_hp, gdn_nseg,
            act_dtype):
    (wt3, conv_w, alog_row, dtb_row, gn_row, wpa, wpb, wout, post_g) = wts
    M = B * L
    k, v, kidx, small = _proj_tail_call(h, wt3, min(M, 1024))

    gates = _gates_call(small, alog_row, dtb_row, gdn_chunk)
    if conv_hist is None:
        convbuf8 = jnp.zeros((B, 8, A_CONV_CH), F32)
    else:
        convbuf8 = jnp.concatenate([jnp.zeros((B, 5, A_CONV_CH), F32), conv_hist.astype(F32)], axis=1)
    y_a, s_new = _gdn_call(big_a, convbuf8, conv_w, gates, gn_row, s0, B=B, L=L, C=gdn_chunk, HP=gdn_hp,
                           NSEG=gdn_nseg, y_dtype=act_dtype)
    y_b = attend(big_b, small, kidx, k, v)
    tm_m = min(1024, M)
    merged = _merge_call(y_a, y_b, big_b, wpa, wpb, tm_m, 512)
    tm_f = min(512, M)
    per_f = (L // tm_f) if gate3.shape[1] == 1 else 1
    y = _final_call(merged, wout, x2d, gate3, post_g, tm_f, per_f)
    return y.reshape(B, L, x2d.shape[1]), k, v, kidx, s_new


def kernel(x_prompt, x_sample, c_prompt, c_sample, cache_k, cache_v, cache_kidx, state_gdn, state_conv,
           page_table, w_ada, b_ada, pre_norm_g, w_in, conv_w, a_log, dt_bias, gdn_norm_g, w_pa, w_pb, w_out,
           post_norm_g):
    depth = w_in.shape[0]
    assert depth == 1, "single-layer trunk"
    BP, LP, D = x_prompt.shape
    BS, LS, _ = x_sample.shape
    n_pool = cache_k.shape[1]

    wt3 = jnp.swapaxes(w_in, 1, 2)[0]
    pre_g = pre_norm_g[0][None, :]
    wts = (wt3, conv_w[0],
           _lane_row(a_log[0], LANE_DECAY), _lane_row(dt_bias[0], LANE_DECAY), gdn_norm_g[0][None, :],
           _cast_bf16_call(w_pa), _cast_bf16_call(w_pb), _cast_bf16_call(w_out), post_norm_g[0][None, :])

    rows = BP + BS
    rows_pad = -(-rows // 16) * 16
    c_all = jnp.concatenate([c_prompt, c_sample, jnp.zeros((rows_pad - rows, D), F32)], axis=0)
    mod = _mod_call(c_all, w_ada[0], b_ada[0][None, :])

    h_p, x2d_p, gate3_p = _modulated_input(x_prompt, mod[0:BP], pre_g)
    h_s, x2d_s, gate3_s = _modulated_input(x_sample, mod[BP:BP + BS], pre_g)
    tn = 512
    big_a_p, big_a_s, aqkv_tail = _proj_nt_call(h_p, h_s, wt3, lambda j: j * tn, N_A, LP, tn, BF16, F32,
                                                "in_proj_a", tail_cols=A_CONV_CH)
    big_b_p, big_b_s = _proj_nt_call(h_p, h_s, wt3, functools.partial(_group_b_row, tn=tn), N_B, LP, tn, BF16,
                                     F32, "in_proj_b")
    conv_p = aqkv_tail[:, 8 - (CONV_W - 1):, 0:A_CONV_CH]

    attend_p = functools.partial(_dsa_prompt_call, B=BP, L=LP, QB=256)
    y_p, k_p, v_p, kidx_p, s_p = _mixers(
        x2d_p, gate3_p, big_a_p, big_b_p, h_p, None, None, attend_p, wts, B=BP, L=LP,
        gdn_chunk=min(GDN_CHUNK, LP), gdn_hp=8, gdn_nseg=8, act_dtype=BF16)

    ck = cache_k[0].reshape(n_pool * PAGE_SIZE * ATT_KV_HEADS, ATT_DH)
    cv = cache_v[0].reshape(n_pool * PAGE_SIZE * ATT_KV_HEADS, ATT_DH)
    ci = cache_kidx[0]
    attend_s = functools.partial(_dsa_sample_call, cache_k=ck, cache_v=cv, cache_kidx=ci, page_table=page_table,
                                 B=BS, T=LS)
    y_s, k_s, v_s, kidx_s, s_s = _mixers(
        x2d_s, gate3_s, big_a_s, big_b_s, h_s, state_conv[0], state_gdn[0], attend_s, wts, B=BS, L=LS,
        gdn_chunk=min(GDN_CHUNK, LS), gdn_hp=16, gdn_nseg=1, act_dtype=F32)
    conv_s = big_a_s[:, 0:A_CONV_CH].reshape(BS, LS, A_CONV_CH)[:, LS - (CONV_W - 1):, :]

    kv_shape_p = (1, BP, LP, ATT_KV_HEADS, ATT_DH)
    kv_shape_s = (1, BS, LS, ATT_KV_HEADS, ATT_DH)
    return (y_p, y_s,
            k_p.reshape(kv_shape_p), v_p.reshape(kv_shape_p), kidx_p.reshape(1, BP, LP, IDX_DH),
            s_p[None], conv_p[None],
            k_s.reshape(kv_shape_s), v_s.reshape(kv_shape_s), kidx_s.reshape(1, BS, LS, IDX_DH),
            s_s[None], conv_s[None])
```

```python
import functools
import math

import jax
import jax.numpy as jnp
from jax import lax
from jax.experimental import pallas as pl
from jax.experimental.pallas import tpu as pltpu

F32 = jnp.float32
BF16 = jnp.bfloat16
I32 = jnp.int32

D_MODEL = 2048
PAGE_SIZE = 128
GDN_QK_HEADS = 16
GDN_V_HEADS = 32
GDN_DK = 128
GDN_DV = 128
CONV_W = 4
GDN_CHUNK = 64
ATT_HEADS = 16
ATT_KV_HEADS = 2
ATT_DH = 128
IDX_HEADS = 16
IDX_DH = 128
TOPK_MAX = 256
NORM_EPS = 1e-6
L2_EPS = 1e-6

A_QK = GDN_QK_HEADS * GDN_DK
A_V = GDN_V_HEADS * GDN_DV
A_CONV_CH = 2 * A_QK + A_V
B_Q = ATT_HEADS * ATT_DH
B_KV = ATT_KV_HEADS * ATT_DH
IDX_Q = IDX_HEADS * IDX_DH
SPLIT_SIZES = (A_CONV_CH, A_V, GDN_V_HEADS, GDN_V_HEADS, B_Q, B_KV, B_KV, B_Q, IDX_Q, IDX_DH, IDX_HEADS,
               D_MODEL, D_MODEL)
B_SEGS = (4, 7, 8, 11, 12)
OFF_AQKV = 0
OFF_AZ = OFF_AQKV + A_CONV_CH
N_A = OFF_AZ + A_V
OFF_BQ = 0
OFF_BZ = OFF_BQ + B_Q
OFF_IQ = OFF_BZ + B_Q
OFF_GA = OFF_IQ + IDX_Q
OFF_GB = OFF_GA + D_MODEL
N_B = OFF_GB + D_MODEL
LANE_BETA = 0
LANE_DECAY = GDN_V_HEADS
LANE_IW = 2 * GDN_V_HEADS

VMEM_LIMIT = 56 * 1024 * 1024
NEG_INF = float("-inf")
INT_MIN = -2 ** 31


def _cparams(sem):
    return pltpu.CompilerParams(dimension_semantics=sem, vmem_limit_bytes=VMEM_LIMIT)


def _dot(a, b):
    return jnp.dot(a, b, preferred_element_type=F32)


def _dot_nt(a, b):
    return lax.dot_general(a, b, (((1,), (1,)), ((), ())), preferred_element_type=F32)


def _dot_tn(a, b):
    return lax.dot_general(a, b, (((0,), (0,)), ((), ())), preferred_element_type=F32)


def _split2(x):
    hi = x.astype(BF16)
    lo = (x - hi.astype(F32)).astype(BF16)
    return hi, lo


def _split3(x):
    x1 = x.astype(BF16)
    r = x - x1.astype(F32)
    x2 = r.astype(BF16)
    x3 = (r - x2.astype(F32)).astype(BF16)
    return x1, x2, x3


def _dot_3pass(a, b):
    ah, al = _split2(a)
    bh, bl = _split2(b)
    return _dot(ah, bh) + _dot(al, bh) + _dot(ah, bl)


def _silu(x):
    h = 0.5 * x
    return h + h * jnp.tanh(h)


def _mod_kernel(c_ref, w_ref, b_ref, o_ref):
    o_ref[...] = _dot_3pass(_silu(c_ref[...]), w_ref[...]) + b_ref[...]


def _mod_call(c_pad, w_ada, b_ada):
    rows, d = c_pad.shape
    n = w_ada.shape[1]
    tn = 768
    return pl.pallas_call(
        _mod_kernel,
        grid=(n // tn,),
        in_specs=[pl.BlockSpec((rows, d), lambda j: (0, 0)),
                  pl.BlockSpec((d, tn), lambda j: (0, j)),
                  pl.BlockSpec((1, tn), lambda j: (0, j))],
        out_specs=pl.BlockSpec((rows, tn), lambda j: (0, j)),
        out_shape=jax.ShapeDtypeStruct((rows, n), F32),
        compiler_params=_cparams(("arbitrary",)),
        name="adaln_mod",
    )(c_pad, w_ada, b_ada)


def _norm_kernel(x_ref, g_ref, scale_ref, shift_ref, o_ref):
    x = x_ref[...]
    y = x * lax.rsqrt(jnp.mean(x * x, axis=-1, keepdims=True) + NORM_EPS) * g_ref[...]
    o_ref[...] = (y * (1.0 + scale_ref[...]) + shift_ref[...]).astype(o_ref.dtype)


def _norm_call(x2d, g_row, scale3, shift3, tm, tiles_per_group):
    m, d = x2d.shape
    r = scale3.shape[1]
    mod_spec = pl.BlockSpec((None, r, d), lambda i: (i // tiles_per_group, 0, 0))
    return pl.pallas_call(
        _norm_kernel,
        grid=(m // tm,),
        in_specs=[pl.BlockSpec((tm, d), lambda i: (i, 0)),
                  pl.BlockSpec((1, d), lambda i: (0, 0)),
                  mod_spec, mod_spec],
        out_specs=pl.BlockSpec((tm, d), lambda i: (i, 0)),
        out_shape=jax.ShapeDtypeStruct((m, d), BF16),
        compiler_params=_cparams(("parallel",)),
        name="norm_modulate",
    )(x2d, g_row, scale3, shift3)


SEG_OFFS = tuple(sum(SPLIT_SIZES[:i]) for i in range(len(SPLIT_SIZES) + 1))


def _proj_nt_kernel(h_ref, w_ref, h2_ref, o_ref, o2_ref, *tail_ref, tail_tiles):
    w16 = w_ref[...].astype(BF16)
    acc = _dot_nt(h_ref[...], w16)
    o_ref[...] = acc.astype(o_ref.dtype)
    if tail_tiles:
        wanted = pl.program_id(1) < tail_tiles
        tail_ref[0][...] = jnp.where(wanted, acc[acc.shape[0] - 8:, :], 0.0)

    is_last = pl.program_id(0) == pl.num_programs(0) - 1

    @pl.when(is_last)
    def _():
        o2_ref[...] = _dot_nt(h2_ref[...], w16).astype(o2_ref.dtype)

    @pl.when(jnp.logical_not(is_last))
    def _():
        o2_ref[...] = jnp.zeros(o2_ref.shape, o2_ref.dtype)


def _proj_nt_call(h, h2, wt3, row_of_tile, n_cols, tm, tn, out_dtype, out2_dtype, name, tail_cols=0):
    m, k = h.shape
    m2 = h2.shape[0]
    n_i, n_j = m // tm, n_cols // tn
    tail_tiles = tail_cols // tn
    out_specs = [pl.BlockSpec((tm, tn), lambda i, j: (i, j)),
                 pl.BlockSpec((m2, tn), lambda i, j: (0, jnp.where(i == n_i - 1, j, n_j)))]
    out_shape = [jax.ShapeDtypeStruct((m, n_cols), out_dtype),
                 jax.ShapeDtypeStruct((m2, n_cols + tn), out2_dtype)]
    if tail_tiles:
        out_specs.append(pl.BlockSpec((None, 8, tn), lambda i, j: (i, 0, jnp.minimum(j, tail_tiles))))
        out_shape.append(jax.ShapeDtypeStruct((n_i, 8, tail_cols + tn), F32))
    return pl.pallas_call(
        functools.partial(_proj_nt_kernel, tail_tiles=tail_tiles),
        grid=(n_i, n_j),
        in_specs=[pl.BlockSpec((tm, k), lambda i, j: (i, 0)),
                  pl.BlockSpec((pl.Element(tn), pl.Element(k)),
                               lambda i, j: (pl.multiple_of(row_of_tile(j), 16), 0)),
                  pl.BlockSpec((m2, k), lambda i, j: (0, 0))],
        out_specs=out_specs,
        out_shape=out_shape,
        compiler_params=_cparams(("arbitrary", "arbitrary")),
        name=name,
    )(h, wt3, h2)


def _group_b_row(j, tn):
    per = B_Q // tn
    seg = j // per
    start = SEG_OFFS[B_SEGS[-1]]
    for s in range(len(B_SEGS) - 2, -1, -1):
        start = jnp.where(seg == s, SEG_OFFS[B_SEGS[s]], start)
    return start + (j % per) * tn


def _proj_tail_kernel(h_ref, wkv_ref, wik_ref, wg_ref, wiw_ref, k_ref, v_ref, kidx_ref, small_ref):
    h = h_ref[...]
    kv = _dot_nt(h, wkv_ref[...].astype(BF16))
    rows = h.shape[0]
    for n in range(ATT_KV_HEADS):
        head_rows = pl.ds(n, rows, stride=ATT_KV_HEADS)
        k_ref[head_rows, :] = kv[:, n * ATT_DH:(n + 1) * ATT_DH]
        v_ref[head_rows, :] = kv[:, B_KV + n * ATT_DH:B_KV + (n + 1) * ATT_DH]
    kidx_ref[...] = _dot_nt(h, wik_ref[...].astype(BF16))
    gate = _dot_nt(h, wg_ref[...].astype(BF16))
    iw = _dot_nt(h, wiw_ref[...].astype(BF16))
    lane = lax.broadcasted_iota(I32, gate.shape, 1)
    small_ref[...] = jnp.where(lane < LANE_IW, gate, jnp.where(lane < LANE_IW + IDX_HEADS, iw, 0.0))


def _proj_tail_call(h, wt3, tm):
    m, k = h.shape
    kvh = ATT_KV_HEADS
    out_blocks = ((kvh * tm, ATT_DH), (kvh * tm, ATT_DH), (tm, IDX_DH), (tm, 128))
    out_arrays = ((kvh * m, ATT_DH), (kvh * m, ATT_DH), (m, IDX_DH), (m, 128))
    beta0, kv0, ik0, iw0 = SEG_OFFS[2], SEG_OFFS[5], SEG_OFFS[9], SEG_OFFS[10]
    assert SEG_OFFS[3] - beta0 == LANE_DECAY and SEG_OFFS[4] - beta0 == LANE_IW and SEG_OFFS[6] - kv0 == B_KV

    def rows(start, size):
        return pl.BlockSpec((pl.Element(size), pl.Element(k)), lambda i: (start, 0))

    return pl.pallas_call(
        _proj_tail_kernel,
        grid=(m // tm,),
        in_specs=[pl.BlockSpec((tm, k), lambda i: (i, 0)),
                  rows(kv0, 2 * B_KV), rows(ik0, IDX_DH), rows(beta0, 128), rows(iw0 - LANE_IW, 128)],
        out_specs=[pl.BlockSpec(blk, lambda i: (i, 0)) for blk in out_blocks],
        out_shape=[jax.ShapeDtypeStruct(shp, F32) for shp in out_arrays],
        compiler_params=_cparams(("parallel",)),
        name="in_proj_tail",
    )(h, wt3, wt3, wt3, wt3)


def _gates_kernel(sm_ref, alog_ref, dtb_ref, o_ref, *, chunk, sub):
    ri = lax.broadcasted_iota(I32, (sub, sub), 0)
    ci = lax.broadcasted_iota(I32, (sub, sub), 1)
    shift = int(math.log2(chunk))
    tri = jnp.where((ri >= ci) & ((ri >> shift) == (ci >> shift)), 1.0, 0.0).astype(BF16)
    lane = lax.broadcasted_iota(I32, (sub, 128), 1)
    for r0 in range(0, sm_ref.shape[0], sub):
        x = sm_ref[r0:r0 + sub, :]
        beta = jax.nn.sigmoid(x)
        z = x + dtb_ref[...]
        softplus = jnp.maximum(z, 0.0) + jnp.log1p(jnp.exp(-jnp.abs(z)))
        g = -jnp.exp(alog_ref[...]) * softplus
        g = jnp.where((lane >= LANE_DECAY) & (lane < LANE_DECAY + GDN_V_HEADS), g, 0.0)
        g1, g2, g3 = _split3(g)
        gc = _dot(tri, g1) + _dot(tri, g2) + _dot(tri, g3)
        o_ref[r0:r0 + sub, :] = jnp.where(lane < GDN_V_HEADS, beta, gc)


def _gates_call(small, alog_row, dtb_row, chunk):
    m = small.shape[0]
    sub = 256
    tm = min(m, 1024)
    return pl.pallas_call(
        functools.partial(_gates_kernel, chunk=chunk, sub=sub),
        grid=(m // tm,),
        in_specs=[pl.BlockSpec((tm, 128), lambda i: (i, 0)),
                  pl.BlockSpec((1, 128), lambda i: (0, 0)),
                  pl.BlockSpec((1, 128), lambda i: (0, 0))],
        out_specs=pl.BlockSpec((tm, 128), lambda i: (i, 0)),
        out_shape=jax.ShapeDtypeStruct((m, 128), F32),
        compiler_params=_cparams(("parallel",)),
        name="gdn_gates",
    )(small, alog_row, dtb_row)


def _unrolled_loop(n, unroll, fn):
    unroll = min(unroll, n)
    assert n % unroll == 0

    def body(t, carry):
        for u in range(unroll):
            fn(t * unroll + u)
        return carry

    lax.fori_loop(0, n // unroll, body, 0)


def _gdn_kernel(*refs, L, C, HP, NSEG, has_state):
    ins = 13 if has_state else 12
    (q_ref, k_ref, v_ref, z_ref, bq_ref, bk_ref, bv_ref, wq_ref, wk_ref, wv_ref, gates_ref, gn_ref) = refs[:12]
    s0_ref = refs[12] if has_state else None
    (y_ref, sout_ref, xq_s, xk_s, xv_s, beta_s, g_s, st_s, a_s, p_s, rhs_s, kt_s, qkd_s, qd_s,
     m_s, n_s, qe_s, o_s) = refs[ins:]
    nh = 2 * HP
    nc = L // C
    hq0 = pl.program_id(1) * HP

    def first_segment():
        xq_s[0:8, :] = bq_ref[...]
        xk_s[0:8, :] = bk_ref[...]
        xv_s[0:8, :] = bv_ref[...]
        if has_state:
            st_s[...] = s0_ref[...]
        else:
            st_s[...] = jnp.zeros(st_s.shape, F32)

    def next_segment():
        xq_s[0:8, :] = xq_s[L:L + 8, :]
        xk_s[0:8, :] = xk_s[L:L + 8, :]
        xv_s[0:8, :] = xv_s[L:L + 8, :]

    if NSEG == 1:
        first_segment()
    else:
        pl.when(pl.program_id(2) == 0)(first_segment)
        pl.when(pl.program_id(2) > 0)(next_segment)
    xq_s[8:L + 8, :] = q_ref[...].astype(F32)
    xk_s[8:L + 8, :] = k_ref[...].astype(F32)
    xv_s[8:L + 8, :] = v_ref[...].astype(F32)

    gates = gates_ref[...]
    lane = lax.broadcasted_iota(I32, gates.shape, 1)
    for idx in range(nh):
        hv = 2 * hq0 + idx
        bcol = jnp.sum(jnp.where(lane == hv + LANE_BETA, gates, 0.0), axis=-1, keepdims=True)
        gcol = jnp.sum(jnp.where(lane == hv + LANE_DECAY, gates, 0.0), axis=-1, keepdims=True)
        beta_s[idx] = jnp.broadcast_to(bcol, (L, 128))
        g_s[idx] = jnp.broadcast_to(gcol, (L, 128))

    ii = lax.broadcasted_iota(I32, (C, C), 0)
    jj = lax.broadcasted_iota(I32, (C, C), 1)
    gn = gn_ref[...]

    def conv(x_s, r0, cols, w):
        win = x_s[pl.ds(r0, C + 8), cols]
        acc = (win[5:5 + C] * w[0:1] + win[6:6 + C] * w[1:2] + win[7:7 + C] * w[2:3] + win[8:8 + C] * w[3:4])
        return _silu(acc)

    def l2n(x):
        return x * lax.rsqrt(jnp.sum(x * x, axis=-1, keepdims=True) + L2_EPS)

    def lanes(i, width):
        if isinstance(i, int):
            return slice(i * width, (i + 1) * width)
        return pl.ds(pl.multiple_of(i * width, width), width)

    def pair_terms(j):
        c, p = (j, 0) if HP == 1 else (j // HP, j % HP)
        r0 = pl.multiple_of(c * C, C)
        rows = pl.ds(r0, C)
        lq, lv = lanes(p, 128), lanes(p, 256)
        q = l2n(conv(xq_s, r0, lq, wq_ref[:, lq])) * (GDN_DK ** -0.5)
        k = l2n(conv(xk_s, r0, lq, wk_ref[:, lq]))
        vc = conv(xv_s, r0, lv, wv_ref[:, lv])
        k16 = k.astype(BF16)
        kk = _dot_nt(k16, k16)
        qk = _dot_nt(q.astype(BF16), k16)
        for r in range(2):
            idx = 2 * p + r
            e = c * nh + idx
            gb = g_s[idx, rows, :]
            bb = beta_s[idx, rows, :]
            e_g = jnp.exp(gb)
            e_tail = jnp.exp(gb[C - 1:C, :] - gb)
            if C < 128:
                gpad = jnp.concatenate([gb, jnp.zeros((128 - C, 128), F32)], axis=0)
            else:
                gpad = gb
            dexp = gb[:, 0:C] - gpad.T[0:C, 0:C]
            dm = jnp.exp(jnp.where(ii >= jj, dexp, NEG_INF))
            a = jnp.where(ii > jj, kk * dm, 0.0) * bb[:, 0:C]
            a_s[e] = a.astype(a_s.dtype)
            p_s[e] = jnp.where(ii == jj, 1.0, 0.0) - a
            rhs_s[e] = jnp.concatenate([vc[:, r * 128:(r + 1) * 128] * bb, k * bb * e_g], axis=1).astype(rhs_s.dtype)
            kt_s[e] = (k * e_tail).astype(kt_s.dtype)
            qkd_s[e] = (qk * dm).astype(qkd_s.dtype)
            qd_s[e] = q * e_g

    def square(e):
        ak = a_s[e].astype(BF16)
        a_s[e] = _dot(ak, ak).astype(a_s.dtype)

    def inverse_stage(e, last):
        ak = a_s[e].astype(BF16)
        pk = p_s[e]
        p_s[e] = pk + _dot(pk.astype(BF16), ak)
        if not last:
            a_s[e] = _dot(ak, ak).astype(a_s.dtype)

    def solve(e):
        rhs_s[e] = _dot(p_s[e].astype(BF16), rhs_s[e].astype(BF16)).astype(rhs_s.dtype)

    def state_terms(e):
        sol16 = rhs_s[e].astype(BF16)
        mn = _dot_tn(kt_s[e].astype(BF16), sol16)
        qo = _dot(qkd_s[e].astype(BF16), sol16)
        n_s[e] = mn[:, 0:GDN_DV]
        m_s[e] = mn[:, GDN_DV:].astype(m_s.dtype)
        o_s[e] = qo[:, 0:GDN_DV]
        qe_s[e] = (qd_s[e] - qo[:, GDN_DV:]).astype(qe_s.dtype)

    def head_step(c, idx):
        e = c * nh + idx
        r0 = pl.multiple_of(c * C, C)
        rows = pl.ds(r0, C)
        s = st_s[idx]
        s16 = s.astype(BF16)
        o = _dot(qe_s[e].astype(BF16), s16) + o_s[e]
        e_last = jnp.exp(g_s[idx, pl.ds(r0 + C - 1, 1), :])
        st_s[idx] = s * e_last - _dot(m_s[e].astype(BF16), s16) + n_s[e]
        on = o * lax.rsqrt(jnp.mean(o * o, axis=-1, keepdims=True) + NORM_EPS) * gn
        zz = z_ref[rows, lanes(idx, 128)].astype(F32)
        y_ref[rows, lanes(idx, 128)] = (on * _silu(zz)).astype(y_ref.dtype)

    ne = nc * nh
    stages = int(math.log2(C)) - 1
    _unrolled_loop(nc * HP, 2 if C >= 64 else 4, pair_terms)
    _unrolled_loop(ne, 32, square)
    for st in range(stages):
        _unrolled_loop(ne, 32, functools.partial(inverse_stage, last=(st == stages - 1)))
    _unrolled_loop(ne, 32, solve)
    _unrolled_loop(ne, 32, state_terms)

    def chunk_step(c, carry):
        if nh <= 4:
            for idx in range(nh):
                head_step(c, idx)
        else:
            _unrolled_loop(nh, 16, functools.partial(head_step, c))
        return carry

    lax.fori_loop(0, nc, chunk_step, 0)
    sout_ref[...] = st_s[...]


def _gdn_call(big, convbuf8, conv_w, gates, gn_row, s0, *, B, L, C, HP, NSEG, y_dtype):
    has_state = s0 is not None
    ls = L // NSEG
    ne = (ls // C) * 2 * HP
    half = BF16 if C % 16 == 0 else F32
    nq = GDN_QK_HEADS // HP
    wq, wv = HP * 128, HP * 256
    kq0 = A_QK // wq
    v0 = (2 * A_QK) // wv
    z0 = OFF_AZ // wv
    in_specs = [
        pl.BlockSpec((ls, wq), lambda b, h, s: (b * NSEG + s, h)),
        pl.BlockSpec((ls, wq), lambda b, h, s: (b * NSEG + s, kq0 + h)),
        pl.BlockSpec((ls, wv), lambda b, h, s: (b * NSEG + s, v0 + h)),
        pl.BlockSpec((ls, wv), lambda b, h, s: (b * NSEG + s, z0 + h)),
        pl.BlockSpec((None, 8, wq), lambda b, h, s: (b, 0, h)),
        pl.BlockSpec((None, 8, wq), lambda b, h, s: (b, 0, kq0 + h)),
        pl.BlockSpec((None, 8, wv), lambda b, h, s: (b, 0, v0 + h)),
        pl.BlockSpec((CONV_W, wq), lambda b, h, s: (0, h)),
        pl.BlockSpec((CONV_W, wq), lambda b, h, s: (0, kq0 + h)),
        pl.BlockSpec((CONV_W, wv), lambda b, h, s: (0, v0 + h)),
        pl.BlockSpec((ls, 128), lambda b, h, s: (b * NSEG + s, 0)),
        pl.BlockSpec((1, 128), lambda b, h, s: (0, 0)),
    ]
    args = [big, big, big, big, convbuf8, convbuf8, convbuf8, conv_w, conv_w, conv_w, gates, gn_row]
    if has_state:
        in_specs.append(pl.BlockSpec((None, 2 * HP, GDN_DK, GDN_DV), lambda b, h, s: (b, h, 0, 0)))
        args.append(s0)
    return pl.pallas_call(
        functools.partial(_gdn_kernel, L=ls, C=C, HP=HP, NSEG=NSEG, has_state=has_state),
        grid=(B, nq, NSEG),
        in_specs=in_specs,
        out_specs=[pl.BlockSpec((ls, wv), lambda b, h, s: (b * NSEG + s, h)),
                   pl.BlockSpec((None, 2 * HP, GDN_DK, GDN_DV), lambda b, h, s: (b, h, 0, 0))],
        out_shape=[jax.ShapeDtypeStruct((B * L, A_V), y_dtype),
                   jax.ShapeDtypeStruct((B, GDN_V_HEADS, GDN_DK, GDN_DV), F32)],
        scratch_shapes=[pltpu.VMEM((ls + 8, wq), F32), pltpu.VMEM((ls + 8, wq), F32), pltpu.VMEM((ls + 8, wv), F32),
                        pltpu.VMEM((2 * HP, ls, 128), F32), pltpu.VMEM((2 * HP, ls, 128), F32),
                        pltpu.VMEM((2 * HP, GDN_DK, GDN_DV), F32),
                        pltpu.VMEM((ne, C, C), half), pltpu.VMEM((ne, C, C), F32),
                        pltpu.VMEM((ne, C, GDN_DV + GDN_DK), half), pltpu.VMEM((ne, C, GDN_DK), half),
                        pltpu.VMEM((ne, C, C), half), pltpu.VMEM((ne, C, GDN_DK), F32),
                        pltpu.VMEM((ne, GDN_DK, GDN_DV), BF16), pltpu.VMEM((ne, GDN_DK, GDN_DV), F32),
                        pltpu.VMEM((ne, C, GDN_DK), half), pltpu.VMEM((ne, C, GDN_DV), F32)],
        compiler_params=_cparams(("parallel", "arbitrary", "arbitrary")),
        name="gdn_prompt" if not has_state else "gdn_sample",
    )(*args)


KEY_NEG_INF = -2 ** 31 + 0x7FFFFF


def _key_to_float(key):
    bits = key ^ ((key >> 31) & jnp.int32(0x7FFFFFFF))
    return pltpu.bitcast(bits, F32)


def _count_true(mask):
    ones = jnp.where(mask, 1.0, 0.0)
    parts = [ones[:, i * 128:(i + 1) * 128] for i in range(mask.shape[1] // 128)]
    while len(parts) > 1:
        parts = [a + b for a, b in zip(parts[0::2], parts[1::2])] + ([parts[-1]] if len(parts) % 2 else [])
    return jnp.sum(parts[0], axis=-1, keepdims=True)


def _selection_bias(score_ref, bias_ref, qpos0, rows, width, topk, bits=1):
    def step(it, t):
        best = t
        for digit in range(1, 2 ** bits):
            cand = t + jnp.left_shift(jnp.int32(digit), 32 - bits * (it + 1))
            cnt = _count_true(score_ref[:, 0:width] >= _key_to_float(cand))
            best = jnp.where((cnt >= float(topk)) | (cand <= KEY_NEG_INF), cand, best)
        return best

    thr = _key_to_float(lax.fori_loop(0, 32 // bits, step, jnp.full((rows, 1), INT_MIN, I32)))
    score = score_ref[:, 0:width]
    causal = (lax.broadcasted_iota(I32, (rows, width), 1)
              <= qpos0 + lax.broadcasted_iota(I32, (rows, width), 0))
    sel = (score >= thr) & causal
    bias_ref[:, 0:width] = jnp.where(sel, 0.0, NEG_INF)
    n_sel = _count_true(sel)
    has_ties = jnp.max(n_sel) > float(topk)

    @pl.when(has_ties)
    def _():
        need = float(topk) - _count_true(score > thr)
        eq16 = jnp.where(score == thr, 1.0, 0.0).astype(BF16)

        def tile(j, carry):
            c0 = pl.multiple_of(j * 128, 128)
            ri = lax.broadcasted_iota(I32, (width, 128), 0)
            ci = lax.broadcasted_iota(I32, (width, 128), 1) + c0
            upper = jnp.where(ri <= ci, 1.0, 0.0).astype(BF16)
            prefix = _dot(eq16, upper)
            s_t = score_ref[:, pl.ds(c0, 128)]
            keep = (s_t > thr) | ((s_t == thr) & (prefix <= need))
            old = bias_ref[:, pl.ds(c0, 128)]
            bias_ref[:, pl.ds(c0, 128)] = jnp.where(keep, old, NEG_INF)
            return carry

        lax.fori_loop(0, width // 128, tile, 0)


SOFTMAX_LOG2_SCALE = (ATT_DH ** -0.5) * math.log2(math.e)


def _dsa_prompt_kernel(qi_ref, wi_ref, kidx_ref, q_ref, k_ref, v_ref, z_ref, o_ref,
                       score_s, bias_s, kidx16_s, k16_s, vext_s, m_s, acc_s, *, QB, SUB, KT, L, topk):
    i = pl.program_id(1)
    q0 = i * QB
    nt = (q0 + QB - 1) // KT + 1
    groups = ATT_HEADS // ATT_KV_HEADS

    @pl.when(i == 0)
    def _():
        kidx16_s[...] = kidx_ref[...].astype(BF16)
        for n in range(ATT_KV_HEADS):
            head_rows = pl.ds(n, L, stride=ATT_KV_HEADS)
            k16_s[:, n * ATT_DH:(n + 1) * ATT_DH] = k_ref[head_rows, :].astype(BF16)
            vext_s[n, :, 0:ATT_DH] = v_ref[head_rows, :].astype(BF16)
            vext_s[n, :, ATT_DH:] = jnp.ones((L, ATT_DH), BF16)

    rowpos = q0 + lax.broadcasted_iota(I32, (SUB, KT), 0)
    col = lax.broadcasted_iota(I32, (SUB, KT), 1)

    def tile_rows(t):
        return pl.ds(pl.multiple_of(t * KT, KT), KT)

    subs = [slice(r, r + SUB) for r in range(0, QB, SUB)]

    def index_tile(t, carry):
        kid = kidx16_s[tile_rows(t), :]
        for rs in subs:
            wi = wi_ref[rs, :] * ((IDX_DH ** -0.5) * (IDX_HEADS ** -0.5))
            acc = jnp.zeros((SUB, KT), F32)
            for h in range(IDX_HEADS):
                lg = _dot_nt(qi_ref[rs, h * IDX_DH:(h + 1) * IDX_DH], kid)
                acc = acc + jnp.maximum(lg, 0.0) * wi[:, LANE_IW + h:LANE_IW + h + 1]
            causal = col + t * KT <= rowpos + rs.start
            score_s[rs, tile_rows(t)] = jnp.where(causal, acc, NEG_INF)
        return carry

    lax.fori_loop(0, nt, index_tile, 0)
    for w in range(1, L // KT + 1):
        @pl.when(nt == w)
        def _():
            _selection_bias(score_s, bias_s, q0, QB, w * KT, topk)

    m_s[...] = jnp.full(m_s.shape, float(jnp.finfo(F32).min), F32)
    acc_s[...] = jnp.zeros(acc_s.shape, F32)

    def attend_tile(t, carry):
        for n in range(ATT_KV_HEADS):
            kn = k16_s[tile_rows(t), n * ATT_DH:(n + 1) * ATT_DH]
            vn = vext_s[n, tile_rows(t), :]
            for g in range(groups):
                h = n * groups + g
                for rs in subs:
                    sb = _dot_nt(q_ref[rs, h * ATT_DH:(h + 1) * ATT_DH], kn) + bias_s[rs, tile_rows(t)]
                    m_old = m_s[h, rs, :]
                    m_new = jnp.maximum(m_old, jnp.max(sb, axis=-1, keepdims=True))
                    p = jnp.exp2((sb - jnp.concatenate([m_new] * (KT // 128), axis=1)) * SOFTMAX_LOG2_SCALE)
                    alpha = jnp.exp2((m_old - m_new) * SOFTMAX_LOG2_SCALE)
                    m_s[h, rs, :] = m_new
                    acc_s[h, rs, :] = (acc_s[h, rs, :] * jnp.concatenate([alpha, alpha], axis=1)
                                       + _dot(p.astype(BF16), vn))
        return carry

    lax.fori_loop(0, nt, attend_tile, 0)
    for h in range(ATT_HEADS):
        cols = slice(h * ATT_DH, (h + 1) * ATT_DH)
        zz = z_ref[:, cols].astype(F32)
        acc = acc_s[h]
        o_ref[:, cols] = (acc[:, 0:ATT_DH] / acc[:, ATT_DH:] * _silu(zz)).astype(o_ref.dtype)


def _dsa_prompt_call(big, small, kidx, k, v, *, B, L, QB):
    topk = min(TOPK_MAX, L // 4)
    nqb = L // QB
    iq0 = OFF_IQ // B_Q
    bq0 = OFF_BQ // B_Q
    bz0 = OFF_BZ // B_Q
    return pl.pallas_call(
        functools.partial(_dsa_prompt_kernel, QB=QB, SUB=128, KT=256, L=L, topk=topk),
        grid=(B, nqb),
        in_specs=[pl.BlockSpec((QB, IDX_Q), lambda b, i: (b * nqb + i, iq0)),
                  pl.BlockSpec((QB, 128), lambda b, i: (b * nqb + i, 0)),
                  pl.BlockSpec((L, IDX_DH), lambda b, i: (b, 0)),
                  pl.BlockSpec((QB, B_Q), lambda b, i: (b * nqb + i, bq0)),
                  pl.BlockSpec((ATT_KV_HEADS * L, ATT_DH), lambda b, i: (b, 0)),
                  pl.BlockSpec((ATT_KV_HEADS * L, ATT_DH), lambda b, i: (b, 0)),
                  pl.BlockSpec((QB, B_Q), lambda b, i: (b * nqb + i, bz0))],
        out_specs=pl.BlockSpec((QB, B_Q), lambda b, i: (b * nqb + i, 0)),
        out_shape=jax.ShapeDtypeStruct((B * L, B_Q), BF16),
        scratch_shapes=[pltpu.VMEM((QB, L), F32), pltpu.VMEM((QB, L), F32),
                        pltpu.VMEM((L, IDX_DH), BF16), pltpu.VMEM((L, B_KV), BF16),
                        pltpu.VMEM((ATT_KV_HEADS, L, 2 * ATT_DH), BF16),
                        pltpu.VMEM((ATT_HEADS, QB, 128), F32), pltpu.VMEM((ATT_HEADS, QB, 2 * ATT_DH), F32)],
        compiler_params=_cparams(("parallel", "arbitrary")),
        name="dsa_prompt",
    )(big, small, kidx, big, k, v, big)


def _dsa_sample_kernel(pt_ref, qi_ref, wi_ref, kin_ref, q_ref, kn_ref, vn_ref, z_ref, ck_hbm, cv_hbm, ci_hbm,
                       o_ref, kbuf, vbuf, ibuf, sems, score_s, bias_s, *, NP, T, topk):
    b = pl.program_id(0)
    nb = pl.num_programs(0)
    slot = b % 2
    past = NP * PAGE_SIZE
    width = past + 128

    kvp = ATT_KV_HEADS * PAGE_SIZE

    def page_copies(bb, sl, p):
        page = pt_ref[bb * NP + p]
        rows = pl.ds(pl.multiple_of(p * PAGE_SIZE, PAGE_SIZE), PAGE_SIZE)
        src2 = pl.ds(pl.multiple_of(page * kvp, kvp), kvp)
        dst2 = pl.ds(pl.multiple_of(p * kvp, kvp), kvp)
        return (pltpu.make_async_copy(ck_hbm.at[src2], kbuf.at[sl, dst2], sems.at[0, sl]),
                pltpu.make_async_copy(cv_hbm.at[src2], vbuf.at[sl, dst2], sems.at[1, sl]),
                pltpu.make_async_copy(ci_hbm.at[page], ibuf.at[sl, rows], sems.at[2, sl]))

    def start_fetch(bb, sl):
        def body(p, carry):
            for cp, prio in zip(page_copies(bb, sl, p), (0, 1, 0)):
                cp.start(priority=prio)
            return carry
        lax.fori_loop(0, NP, body, 0, unroll=8)

    def wait_fetch(bb, sl):
        def body(p, carry):
            for cp in page_copies(bb, sl, p):
                cp.wait()
            return carry
        lax.fori_loop(0, NP, body, 0, unroll=8)

    @pl.when(b == 0)
    def _():
        for sl in range(2):
            kbuf[sl, ATT_KV_HEADS * past:, :] = jnp.zeros((ATT_KV_HEADS * 128, ATT_DH), F32)
            vbuf[sl, ATT_KV_HEADS * past:, :] = jnp.zeros((ATT_KV_HEADS * 128, ATT_DH), F32)
            ibuf[sl, past:, :] = jnp.zeros((128, IDX_DH), F32)
        start_fetch(0, 0)

    @pl.when(b + 1 < nb)
    def _():
        start_fetch(b + 1, 1 - slot)

    new_rows = pl.ds(ATT_KV_HEADS * past, ATT_KV_HEADS * T)
    kbuf[slot, new_rows, :] = kn_ref[...]
    vbuf[slot, new_rows, :] = vn_ref[...]
    ibuf[slot, past:past + T, :] = kin_ref[...]
    wait_fetch(b, slot)

    qi = qi_ref[...]
    qis = jnp.concatenate([qi[:, h * IDX_DH:(h + 1) * IDX_DH] for h in range(IDX_HEADS)], axis=0).astype(BF16)
    wi = wi_ref[...] * ((IDX_DH ** -0.5) * (IDX_HEADS ** -0.5))
    lg = _dot_nt(qis, ibuf[slot].astype(BF16))
    score = jnp.zeros((T, width), F32)
    for h in range(IDX_HEADS):
        score = score + jnp.maximum(lg[h * T:(h + 1) * T, :], 0.0) * wi[:, LANE_IW + h:LANE_IW + h + 1]
    qpos = past + lax.broadcasted_iota(I32, (T, width), 0)
    kpos = lax.broadcasted_iota(I32, (T, width), 1)
    causal = kpos <= qpos
    score_s[...] = jnp.where(causal, score, NEG_INF)
    _selection_bias(score_s, bias_s, past, T, width, topk, bits=2)

    groups = ATT_HEADS // ATT_KV_HEADS
    q = q_ref[...]
    bias = jnp.concatenate([bias_s[...]] * groups, axis=0)
    for n in range(ATT_KV_HEADS):
        head_rows = pl.ds(n, width, stride=ATT_KV_HEADS)
        kn = kbuf[slot, head_rows, :].astype(BF16)
        vn = vbuf[slot, head_rows, :].astype(BF16)
        qs = jnp.concatenate([q[:, (n * groups + g) * ATT_DH:(n * groups + g + 1) * ATT_DH] for g in range(groups)],
                             axis=0).astype(BF16)
        s = _dot_nt(qs, kn) * (ATT_DH ** -0.5) + bias
        m = jnp.max(s, axis=-1, keepdims=True)
        p = jnp.exp(s - m)
        denom = jnp.sum(p, axis=-1, keepdims=True)
        o = _dot(p.astype(BF16), vn) / denom
        for g in range(groups):
            h = n * groups + g
            cols = slice(h * ATT_DH, (h + 1) * ATT_DH)
            zz = z_ref[:, cols]
            o_ref[:, cols] = o[g * T:(g + 1) * T, :] * _silu(zz)


def _dsa_sample_call(big, small, kidx, k, v, cache_k, cache_v, cache_kidx, page_table, *, B, T):
    npages = page_table.shape[1]
    past = npages * PAGE_SIZE
    topk = min(TOPK_MAX, (past + T) // 4)
    width = past + 128
    iq0 = OFF_IQ // B_Q
    bq0 = OFF_BQ // B_Q
    bz0 = OFF_BZ // B_Q
    grid_spec = pltpu.PrefetchScalarGridSpec(
        num_scalar_prefetch=1,
        grid=(B,),
        in_specs=[pl.BlockSpec((T, IDX_Q), lambda b, pt: (b, iq0)),
                  pl.BlockSpec((T, 128), lambda b, pt: (b, 0)),
                  pl.BlockSpec((T, IDX_DH), lambda b, pt: (b, 0)),
                  pl.BlockSpec((T, B_Q), lambda b, pt: (b, bq0)),
                  pl.BlockSpec((ATT_KV_HEADS * T, ATT_DH), lambda b, pt: (b, 0)),
                  pl.BlockSpec((ATT_KV_HEADS * T, ATT_DH), lambda b, pt: (b, 0)),
                  pl.BlockSpec((T, B_Q), lambda b, pt: (b, bz0)),
                  pl.BlockSpec(memory_space=pl.ANY),
                  pl.BlockSpec(memory_space=pl.ANY),
                  pl.BlockSpec(memory_space=pl.ANY)],
        out_specs=pl.BlockSpec((T, B_Q), lambda b, pt: (b, 0)),
        scratch_shapes=[pltpu.VMEM((2, ATT_KV_HEADS * width, ATT_DH), F32),
                        pltpu.VMEM((2, ATT_KV_HEADS * width, ATT_DH), F32),
                        pltpu.VMEM((2, width, IDX_DH), F32), pltpu.SemaphoreType.DMA((3, 2)),
                        pltpu.VMEM((T, width), F32), pltpu.VMEM((T, width), F32)])
    return pl.pallas_call(
        functools.partial(_dsa_sample_kernel, NP=npages, T=T, topk=topk),
        grid_spec=grid_spec,
        out_shape=jax.ShapeDtypeStruct((B * T, B_Q), F32),
        compiler_params=_cparams(("arbitrary",)),
        name="dsa_sample",
    )(page_table.reshape(-1), big, small, kidx, big, k, v, big, cache_k, cache_v, cache_kidx)


def _merge_kernel(ya_ref, yb_ref, ga_ref, gb_ref, wpa_ref, wpb_ref, o_ref):
    pa = _dot(ya_ref[...].astype(BF16), wpa_ref[...])
    pb = _dot(yb_ref[...].astype(BF16), wpb_ref[...])
    o = jax.nn.sigmoid(ga_ref[...].astype(F32)) * pa + jax.nn.sigmoid(gb_ref[...].astype(F32)) * pb
    o_ref[...] = o.astype(o_ref.dtype)


def _merge_call(ya, yb, big, wpa, wpb, tm, tn):
    m = ya.shape[0]
    n = wpa.shape[1]
    ga0 = OFF_GA // tn
    gb0 = OFF_GB // tn
    return pl.pallas_call(
        _merge_kernel,
        grid=(m // tm, n // tn),
        in_specs=[pl.BlockSpec((tm, A_V), lambda i, j: (i, 0)),
                  pl.BlockSpec((tm, B_Q), lambda i, j: (i, 0)),
                  pl.BlockSpec((tm, tn), lambda i, j: (i, ga0 + j)),
                  pl.BlockSpec((tm, tn), lambda i, j: (i, gb0 + j)),
                  pl.BlockSpec((A_V, tn), lambda i, j: (0, j)),
                  pl.BlockSpec((B_Q, tn), lambda i, j: (0, j))],
        out_specs=pl.BlockSpec((tm, tn), lambda i, j: (i, j)),
        out_shape=jax.ShapeDtypeStruct((m, n), BF16),
        compiler_params=_cparams(("parallel", "arbitrary")),
        name="merge_proj",
    )(ya, yb, big, big, wpa, wpb)


def _final_kernel(m_ref, w_ref, x_ref, gate_ref, g_ref, o_ref):
    t = _dot(m_ref[...], w_ref[...])
    y = t * lax.rsqrt(jnp.mean(t * t, axis=-1, keepdims=True) + NORM_EPS) * g_ref[...]
    o_ref[...] = x_ref[...] + gate_ref[...] * y


def _final_call(merged, w_out, x2d, gate3, g_row, tm, tiles_per_group):
    m, d = x2d.shape
    r = gate3.shape[1]
    return pl.pallas_call(
        _final_kernel,
        grid=(m // tm,),
        in_specs=[pl.BlockSpec((tm, d), lambda i: (i, 0)),
                  pl.BlockSpec((d, d), lambda i: (0, 0)),
                  pl.BlockSpec((tm, d), lambda i: (i, 0)),
                  pl.BlockSpec((None, r, d), lambda i: (i // tiles_per_group, 0, 0)),
                  pl.BlockSpec((1, d), lambda i: (0, 0))],
        out_specs=pl.BlockSpec((tm, d), lambda i: (i, 0)),
        out_shape=jax.ShapeDtypeStruct((m, d), F32),
        compiler_params=_cparams(("parallel",)),
        name="out_proj",
    )(merged, w_out, x2d, gate3, g_row)


def _cast_kernel(x_ref, o_ref):
    o_ref[...] = x_ref[...].astype(o_ref.dtype)


def _cast_bf16_call(w3):
    _, r, n = w3.shape
    tr = 256
    return pl.pallas_call(
        _cast_kernel,
        grid=(r // tr,),
        in_specs=[pl.BlockSpec((None, tr, n), lambda i: (0, i, 0))],
        out_specs=pl.BlockSpec((tr, n), lambda i: (i, 0)),
        out_shape=jax.ShapeDtypeStruct((r, n), BF16),
        compiler_params=_cparams(("parallel",)),
        name="w_cast",
    )(w3)


def _lane_row(vec, lane0):
    return jnp.zeros((1, 128), F32).at[0, lane0:lane0 + vec.shape[0]].set(vec.astype(F32))


def _modulated_input(x, mod, pre_g):
    B, L, D = x.shape
    M = B * L
    x2d = x.reshape(M, D)
    shift, scale, gate = mod[:, 0:D], mod[:, D:2 * D], mod[:, 2 * D:3 * D]
    if L >= 512:
        tm_n = 512
        per = L // tm_n
        scale3, shift3, gate3 = scale[:, None, :], shift[:, None, :], gate[:, None, :]
    else:
        tm_n = M
        per = 1
        scale3, shift3, gate3 = (jnp.repeat(t, L, axis=0)[None] for t in (scale, shift, gate))
    return _norm_call(x2d, pre_g, scale3, shift3, tm_n, per), x2d, gate3


def _mixers(x2d, gate3, big_a, big_b, h, conv_hist, s0, attend, wts, *, B, L, gdn_chunk, gdn_hp, gdn_nseg,
            act_dtype):
    (wt3, conv_w, alog_row, dtb_row, gn_row, wpa, wpb, wout, post_g) = wts
    M = B * L
    k, v, kidx, small = _proj_tail_call(h, wt3, min(M, 1024))

    gates = _gates_call(small, alog_row, dtb_row, gdn_chunk)
    if conv_hist is None:
        convbuf8 = jnp.zeros((B, 8, A_CONV_CH), F32)
    else:
        convbuf8 = jnp.concatenate([jnp.zeros((B, 5, A_CONV_CH), F32), conv_hist.astype(F32)], axis=1)
    y_a, s_new = _gdn_call(big_a, convbuf8, conv_w, gates, gn_row, s0, B=B, L=L, C=gdn_chunk, HP=gdn_hp,
                           NSEG=gdn_nseg, y_dtype=act_dtype)
    y_b = attend(big_b, small, kidx, k, v)
    tm_m = min(1024, M)
    merged = _merge_call(y_a, y_b, big_b, wpa, wpb, tm_m, 512)
    tm_f = min(512, M)
    per_f = (L // tm_f) if gate3.shape[1] == 1 else 1
    y = _final_call(merged, wout, x2d, gate3, post_g, tm_f, per_f)
    return y.reshape(B, L, x2d.shape[1]), k, v, kidx, s_new


def kernel(x_prompt, x_sample, c_prompt, c_sample, cache_k, cache_v, cache_kidx, state_gdn, state_conv,
           page_table, w_ada, b_ada, pre_norm_g, w_in, conv_w, a_log, dt_bias, gdn_norm_g, w_pa, w_pb, w_out,
           post_norm_g):
    depth = w_in.shape[0]
    assert depth == 1, "single-layer trunk"
    BP, LP, D = x_prompt.shape
    BS, LS, _ = x_sample.shape
    n_pool = cache_k.shape[1]

    wt3 = jnp.swapaxes(w_in, 1, 2)[0]
    pre_g = pre_norm_g[0][None, :]
    wts = (wt3, conv_w[0],
           _lane_row(a_log[0], LANE_DECAY), _lane_row(dt_bias[0], LANE_DECAY), gdn_norm_g[0][None, :],
           _cast_bf16_call(w_pa), _cast_bf16_call(w_pb), _cast_bf16_call(w_out), post_norm_g[0][None, :])

    rows = BP + BS
    rows_pad = -(-rows // 16) * 16
    c_all = jnp.concatenate([c_prompt, c_sample, jnp.zeros((rows_pad - rows, D), F32)], axis=0)
    mod = _mod_call(c_all, w_ada[0], b_ada[0][None, :])

    h_p, x2d_p, gate3_p = _modulated_input(x_prompt, mod[0:BP], pre_g)
    h_s, x2d_s, gate3_s = _modulated_input(x_sample, mod[BP:BP + BS], pre_g)
    tn = 512
    big_a_p, big_a_s, aqkv_tail = _proj_nt_call(h_p, h_s, wt3, lambda j: j * tn, N_A, LP, tn, BF16, F32,
                                                "in_proj_a", tail_cols=A_CONV_CH)
    big_b_p, big_b_s = _proj_nt_call(h_p, h_s, wt3, functools.partial(_group_b_row, tn=tn), N_B, LP, tn, BF16,
                                     F32, "in_proj_b")
    conv_p = aqkv_tail[:, 8 - (CONV_W - 1):, 0:A_CONV_CH]

    attend_p = functools.partial(_dsa_prompt_call, B=BP, L=LP, QB=256)
    y_p, k_p, v_p, kidx_p, s_p = _mixers(
        x2d_p, gate3_p, big_a_p, big_b_p, h_p, None, None, attend_p, wts, B=BP, L=LP,
        gdn_chunk=min(GDN_CHUNK, LP), gdn_hp=8, gdn_nseg=8, act_dtype=BF16)

    ck = cache_k[0].reshape(n_pool * PAGE_SIZE * ATT_KV_HEADS, ATT_DH)
    cv = cache_v[0].reshape(n_pool * PAGE_SIZE * ATT_KV_HEADS, ATT_DH)
    ci = cache_kidx[0]
    attend_s = functools.partial(_dsa_sample_call, cache_k=ck, cache_v=cv, cache_kidx=ci, page_table=page_table,
                                 B=BS, T=LS)
    y_s, k_s, v_s, kidx_s, s_s = _mixers(
        x2d_s, gate3_s, big_a_s, big_b_s, h_s, state_conv[0], state_gdn[0], attend_s, wts, B=BS, L=LS,
        gdn_chunk=min(GDN_CHUNK, LS), gdn_hp=16, gdn_nseg=1, act_dtype=F32)
    conv_s = big_a_s[:, 0:A_CONV_CH].reshape(BS, LS, A_CONV_CH)[:, LS - (CONV_W - 1):, :]

    kv_shape_p = (1, BP, LP, ATT_KV_HEADS, ATT_DH)
    kv_shape_s = (1, BS, LS, ATT_KV_HEADS, ATT_DH)
    return (y_p, y_s,
            k_p.reshape(kv_shape_p), v_p.reshape(kv_shape_p), kidx_p.reshape(1, BP, LP, IDX_DH),
            s_p[None], conv_p[None],
            k_s.reshape(kv_shape_s), v_s.reshape(kv_shape_s), kidx_s.reshape(1, BS, LS, IDX_DH),
            s_s[None], conv_s[None])
```
